```python
import math
import jax, jax.numpy as jnp
from jax import lax
import numpy as np

D_MODEL = 1024
BATCH = 8
SEQ = 2048
DEPTH = 4
DEC_BATCH = 128
DEC_SEQ = 1
PAST_LEN = 16384
PAGE_SIZE = 128

A_WIDTH = D_MODEL
A_GROUPS = 8
A_GROUP_DIM = A_WIDTH // A_GROUPS
A_CHUNK = 128
B_HEADS = 8
B_HEAD_DIM = D_MODEL // B_HEADS
B_KEY = B_HEADS * B_HEAD_DIM
B_VAL = B_HEADS * B_HEAD_DIM
CONV_W = 4
CONV_DIM = 2 * B_KEY + B_VAL
B_CHUNK = 64
IN_SIZES = (A_WIDTH, A_WIDTH, A_WIDTH, CONV_DIM, B_VAL, B_HEADS, B_HEADS, D_MODEL, D_MODEL)
IN_DIM = 3 * A_WIDTH + CONV_DIM + B_VAL + 2 * B_HEADS + 2 * D_MODEL
EPS = 1e-6

kernel_name = "hybrid_gmlp_gated_deltanet_decode_step"


def rmsnorm(x, g):
    xf = x.astype(jnp.float32)
    y = xf * lax.rsqrt(jnp.mean(xf * xf, axis=-1, keepdims=True) + EPS)
    return (y * g.astype(jnp.float32)).astype(x.dtype)


def l2norm(x):
    return x * lax.rsqrt(jnp.sum(x * x, axis=-1, keepdims=True) + EPS)


def chunk_spatial_gating(u, v, z, norm_g, w_s, b_s):
    bsz, seq, _ = v.shape
    lc = min(A_CHUNK, seq)
    pad = (-seq) % lc
    vn = rmsnorm(v, norm_g)
    vp = jnp.pad(vn, ((0, 0), (0, pad), (0, 0))) if pad else vn
    n = (seq + pad) // lc
    vc = vp.reshape(bsz, n, lc, A_GROUPS, A_GROUP_DIM)
    causal = jnp.tril(jnp.ones((lc, lc), dtype=bool))
    ws = jnp.where(causal, w_s[:, :lc, :lc], 0).astype(v.dtype)
    bias = b_s[:, :lc].T[None, None, :, :, None].astype(v.dtype)
    mixed = jnp.einsum('gts,bnsgc->bntgc', ws, vc) + bias
    mixed = mixed.reshape(bsz, seq + pad, A_WIDTH)[:, :seq]
    return u * mixed * jax.nn.silu(z), vn


def short_conv(x, buf, w):
    seq = x.shape[1]
    xe = jnp.concatenate([buf.astype(x.dtype), x], axis=1)
    out = sum(xe[:, i:i + seq] * w[i] for i in range(CONV_W))
    return jax.nn.silu(out), xe[:, -(CONV_W - 1):]


def to_chunks(t, n, c):
    b, _, h = t.shape[:3]
    t = t.reshape((b, n, c, h) + t.shape[3:])
    return jnp.moveaxis(t, 3, 1)


def gated_delta(q, k, v, beta, g, s0):
    bsz, seq, nh, _ = q.shape
    dv = v.shape[-1]
    c = min(B_CHUNK, seq)
    pad = (-seq) % c
    if pad:
        padw = lambda t: jnp.pad(t, ((0, 0), (0, pad)) + ((0, 0),) * (t.ndim - 2))
        q, k, v, beta, g = (padw(t) for t in (q, k, v, beta, g))
    n = (seq + pad) // c
    q, k, v, beta, g = (to_chunks(t, n, c) for t in (q, k, v, beta, g))
    gc = jnp.cumsum(g, axis=-1)
    incl = jnp.tril(jnp.ones((c, c), dtype=bool))
    strict = jnp.tril(jnp.ones((c, c), dtype=bool), -1)
    decay = jnp.exp(jnp.where(incl, gc[..., :, None] - gc[..., None, :], -jnp.inf))
    kk = jnp.einsum('bhnid,bhnjd->bhnij', k, k)
    t_mat = jnp.where(strict, beta[..., :, None] * kk * decay, 0.0) + jnp.eye(c, dtype=jnp.float32)
    rhs = jnp.concatenate([v * beta[..., None], k * (beta * jnp.exp(gc))[..., None]], axis=-1)
    sol = lax.linalg.triangular_solve(t_mat, rhs, left_side=True, lower=True, unit_diagonal=True)
    u, w = sol[..., :dv], sol[..., dv:]
    attn = jnp.einsum('bhnid,bhnjd->bhnij', q, k) * decay
    q_dec = q * jnp.exp(gc)[..., None]
    k_dec = k * jnp.exp(gc[..., -1:] - gc)[..., None]
    g_last = jnp.exp(gc[..., -1])

    def step(s, xs):
        u_c, w_c, qd_c, kd_c, a_c, gl_c = xs
        v_new = u_c - jnp.einsum('bhck,bhkv->bhcv', w_c, s)
        o_c = jnp.einsum('bhck,bhkv->bhcv', qd_c, s) + jnp.einsum('bhij,bhjv->bhiv', a_c, v_new)
        s = s * gl_c[..., None, None] + jnp.einsum('bhck,bhcv->bhkv', kd_c, v_new)
        return s, o_c

    xs = tuple(jnp.moveaxis(t, 2, 0) for t in (u, w, q_dec, k_dec, attn, g_last))
    s_fin, o = lax.scan(step, s0, xs)
    o = o.transpose(1, 0, 3, 2, 4).reshape(bsz, n * c, nh, dv)[:, :seq]
    return o, s_fin


def mixer_layer(x, conv_buf, s0, pre_g, w_in, gmlp_g, w_s, b_s, conv_w, a_log, dt_bias, gdn_g, w_pa, w_pb, w_out, post_g):
    bsz, seq, _ = x.shape
    h = rmsnorm(x, pre_g)
    proj = h @ w_in
    splits = np.cumsum(IN_SIZES)[:-1].tolist()
    u_a, v_a, z_a, qkv, z_b, a_b, b_b, gate_a, gate_b = jnp.split(proj, splits, axis=-1)
    y_a, v_rows = chunk_spatial_gating(u_a, v_a, z_a, gmlp_g, w_s, b_s)
    qkv, new_buf = short_conv(qkv, conv_buf, conv_w)
    qkv = qkv.astype(jnp.float32)
    hs = (bsz, seq, B_HEADS, B_HEAD_DIM)
    q = l2norm(qkv[..., :B_KEY].reshape(hs)) * (B_HEAD_DIM ** -0.5)
    k = l2norm(qkv[..., B_KEY:2 * B_KEY].reshape(hs))
    v = qkv[..., 2 * B_KEY:].reshape(hs)
    beta = jax.nn.sigmoid(b_b.astype(jnp.float32))
    g = -jnp.exp(a_log.astype(jnp.float32)) * jax.nn.softplus(a_b.astype(jnp.float32) + dt_bias.astype(jnp.float32))
    o, s_new = gated_delta(q, k, v, beta, g, s0.astype(jnp.float32))
    o = rmsnorm(o, gdn_g) * jax.nn.silu(z_b.astype(jnp.float32).reshape(hs))
    y_b = o.reshape(bsz, seq, B_VAL).astype(x.dtype)
    merged = jax.nn.sigmoid(gate_a) * (y_a @ w_pa) + jax.nn.sigmoid(gate_b) * (y_b @ w_pb)
    out = merged @ w_out
    return x + rmsnorm(out, post_g), new_buf, s_new, v_rows


def setup_inputs(seed: int = 0) -> dict:
    key = jax.random.key(seed)
    ks = jax.random.split(key, 17)
    nrm = lambda k, shape, scale: jax.random.normal(k, shape, jnp.float32) * scale
    dt = jnp.exp(jax.random.uniform(ks[11], (DEPTH, B_HEADS), jnp.float32, math.log(1e-3), math.log(1e-1)))
    return {
        "x_prompt": nrm(ks[0], (BATCH, SEQ, D_MODEL), 1.0),
        "x_sample": nrm(ks[1], (DEC_BATCH, DEC_SEQ, D_MODEL), 1.0),
        "state_gdn": nrm(ks[2], (DEPTH, DEC_BATCH, B_HEADS, B_HEAD_DIM, B_HEAD_DIM), 0.1),
        "state_conv": nrm(ks[3], (DEPTH, DEC_BATCH, CONV_W - 1, CONV_DIM), 1.0),
        "pre_norm": 1.0 + nrm(ks[4], (DEPTH, D_MODEL), 0.02),
        "w_in": nrm(ks[5], (DEPTH, D_MODEL, IN_DIM), D_MODEL ** -0.5),
        "gmlp_norm": 1.0 + nrm(ks[6], (DEPTH, A_WIDTH), 0.02),
        "w_spatial": nrm(ks[7], (DEPTH, A_GROUPS, A_CHUNK, A_CHUNK), A_CHUNK ** -0.5),
        "b_spatial": 1.0 + nrm(ks[8], (DEPTH, A_GROUPS, A_CHUNK), 0.02),
        "conv_w": nrm(ks[9], (DEPTH, CONV_W, CONV_DIM), CONV_W ** -0.5),
        "a_log": jnp.log(jax.random.uniform(ks[10], (DEPTH, B_HEADS), jnp.float32, 1.0, 16.0)),
        "dt_bias": dt + jnp.log(-jnp.expm1(-dt)),
        "gdn_norm": 1.0 + nrm(ks[12], (DEPTH, B_HEAD_DIM), 0.02),
        "w_proj_a": nrm(ks[13], (DEPTH, A_WIDTH, D_MODEL), A_WIDTH ** -0.5),
        "w_proj_b": nrm(ks[14], (DEPTH, B_VAL, D_MODEL), B_VAL ** -0.5),
        "w_out": nrm(ks[15], (DEPTH, D_MODEL, D_MODEL), D_MODEL ** -0.5),
        "post_norm": 1.0 + nrm(ks[16], (DEPTH, D_MODEL), 0.02),
    }


def reference(x_prompt, x_sample, state_gdn, state_conv, pre_norm, w_in, gmlp_norm, w_spatial, b_spatial, conv_w, a_log, dt_bias, gdn_norm, w_proj_a, w_proj_b, w_out, post_norm):
    bp = x_prompt.shape[0]
    conv_zero = jnp.zeros((bp, CONV_W - 1, CONV_DIM), x_prompt.dtype)
    s_zero = jnp.zeros((bp, B_HEADS, B_HEAD_DIM, B_HEAD_DIM), jnp.float32)
    xp, xs = x_prompt, x_sample
    gdn_p, conv_p, gdn_s, conv_s, vrows_s = [], [], [], [], []
    for l in range(DEPTH):
        w = (pre_norm[l], w_in[l], gmlp_norm[l], w_spatial[l], b_spatial[l], conv_w[l], a_log[l], dt_bias[l],
             gdn_norm[l], w_proj_a[l], w_proj_b[l], w_out[l], post_norm[l])
        xp, cb_p, sg_p, _ = mixer_layer(xp, conv_zero, s_zero, *w)
        xs, cb_s, sg_s, vr_s = mixer_layer(xs, state_conv[l], state_gdn[l], *w)
        gdn_p.append(sg_p)
        conv_p.append(cb_p)
        gdn_s.append(sg_s)
        conv_s.append(cb_s)
        vrows_s.append(vr_s)
    return (xp, xs, jnp.stack(gdn_p), jnp.stack(conv_p), jnp.stack(gdn_s), jnp.stack(conv_s), jnp.stack(vrows_s))
```

```python
import functools

import jax
import jax.numpy as jnp
from jax import lax
from jax.experimental import pallas as pl
from jax.experimental.pallas import tpu as pltpu

F32 = jnp.float32
BF16 = jnp.bfloat16
EPS = 1e-6

LANES = 128
HEAD_DIM = 128
N_HEADS = 8
CONV_W = 4
CHUNK = 128
TILE_M = 256
CARRY_ROW = 8
VMEM_LIMIT = 56 * 1024 * 1024


def _dot(a, b):
    return jnp.dot(a.astype(BF16), b.astype(BF16), preferred_element_type=F32)


def _dot_nt(a, b):
    return lax.dot_general(a.astype(BF16), b.astype(BF16), (((1,), (1,)), ((), ())),
                           preferred_element_type=F32)


def _dot_tn(a, b):
    return lax.dot_general(a.astype(BF16), b.astype(BF16), (((0,), (0,)), ((), ())),
                           preferred_element_type=F32)


def _split(a):
    hi = a.astype(BF16)
    lo = (a - hi.astype(F32)).astype(BF16)
    return hi, lo


def _dot_exact(a, b):
    return jnp.dot(a, b, preferred_element_type=F32, precision=lax.Precision.HIGHEST)


def _rms(x, g):
    return x * lax.rsqrt(jnp.mean(x * x, axis=-1, keepdims=True) + EPS) * g


def _silu(x):
    return x * jax.nn.sigmoid(x)


def _softplus(x):
    return jnp.maximum(x, 0.0) + jnp.log(1.0 + jnp.exp(-jnp.abs(x)))


def _l2norm(x):
    return x * lax.rsqrt(jnp.sum(x * x, axis=-1, keepdims=True) + EPS)


def _iota2(shape, dim):
    return lax.broadcasted_iota(jnp.int32, shape, dim)


def _prompt_kernel(x_ref, pre_g_ref, w_main_ref, w_ab_ref, gmlp_g_ref, ws_ref, bs_ref,
                   convw_ref, alog_ref, dtb_ref, gdn_g_ref, w_pa_ref, w_pb_ref, w_out_ref,
                   post_g_ref,
                   y_ref, s_out_ref, conv_out_ref,
                   s_scr, cbuf_scr, ya_scr, yb_scr, qkv_scr):
    t = pl.program_id(1)
    nt = pl.num_programs(1)
    tm = x_ref.shape[1]
    d = x_ref.shape[2]
    n_chunks = tm // CHUNK
    kdim = N_HEADS * HEAD_DIM

    @pl.when(t == 0)
    def _():
        s_scr[...] = jnp.zeros_like(s_scr)
        cbuf_scr[0:CARRY_ROW, :] = jnp.zeros((CARRY_ROW, cbuf_scr.shape[1]), F32)

    x = x_ref[0]
    h = _rms(x, pre_g_ref[...]).astype(BF16)

    row = _iota2((CHUNK, CHUNK), 0)
    col = _iota2((CHUNK, CHUNK), 1)
    incl = row >= col
    strict = row > col

    uvz = jnp.dot(h, w_main_ref[:, 0:3 * d], preferred_element_type=F32)
    vn = _rms(uvz[:, d:2 * d], gmlp_g_ref[...])
    vnb = vn.astype(BF16)
    for g in range(N_HEADS):
        wsg = jnp.where(incl, ws_ref[g], 0.0).astype(BF16)
        lo, hi = g * HEAD_DIM, (g + 1) * HEAD_DIM
        for c in range(n_chunks):
            r0, r1 = c * CHUNK, (c + 1) * CHUNK
            mixed = jnp.dot(wsg, vnb[r0:r1, lo:hi], preferred_element_type=F32) + bs_ref[g]
            u = uvz[r0:r1, lo:hi]
            z = uvz[r0:r1, 2 * d + lo:2 * d + hi]
            ya_scr[r0:r1, lo:hi] = (u * mixed * _silu(z)).astype(BF16)
    pa = jnp.dot(ya_scr[...], w_pa_ref[...], preferred_element_type=F32)

    qkvz = jnp.dot(h, w_main_ref[:, 3 * d:3 * d + 4 * kdim], preferred_element_type=F32)
    cbuf_scr[CARRY_ROW:CARRY_ROW + tm, :] = qkvz[:, 0:3 * kdim]
    conv = cbuf_scr[CARRY_ROW - 3:CARRY_ROW - 3 + tm, :] * convw_ref[0:1, :]
    for i in range(1, CONV_W):
        conv = conv + cbuf_scr[CARRY_ROW - 3 + i:CARRY_ROW - 3 + i + tm, :] * convw_ref[i:i + 1, :]
    tail = cbuf_scr[CARRY_ROW + tm - 3:CARRY_ROW + tm, :]
    cbuf_scr[CARRY_ROW - 3:CARRY_ROW, :] = tail

    @pl.when(t == nt - 1)
    def _():
        conv_out_ref[0] = tail

    qkv_scr[...] = _silu(conv)

    ab = jnp.dot(h, w_ab_ref[...], preferred_element_type=F32)
    g_all = -jnp.exp(alog_ref[...]) * _softplus(ab + dtb_ref[...])
    beta_all = jax.nn.sigmoid(ab)
    g_all_t = g_all.T
    ltri = incl.astype(F32)
    utri = (row <= col).astype(F32)
    eye = (row == col).astype(F32)
    scale = HEAD_DIM ** -0.5

    for c in range(n_chunks):
        r0, r1 = c * CHUNK, (c + 1) * CHUNK
        gc = _dot_exact(ltri, g_all[r0:r1, :])
        gct = _dot_exact(g_all_t[0:N_HEADS, r0:r1], utri)
        egc = jnp.exp(gc)
        for hd in range(N_HEADS):
            lo, hi = hd * HEAD_DIM, (hd + 1) * HEAD_DIM
            gcol = gc[:, hd:hd + 1]
            grow = gct[hd:hd + 1, :]
            egcol = egc[:, hd:hd + 1]
            bcol = beta_all[r0:r1, N_HEADS + hd:N_HEADS + hd + 1]
            dec = jnp.where(incl, jnp.exp(jnp.where(incl, gcol - grow, 0.0)), 0.0)
            qh = _l2norm(qkv_scr[r0:r1, lo:hi]) * scale
            kh = _l2norm(qkv_scr[r0:r1, kdim + lo:kdim + hi])
            vh = qkv_scr[r0:r1, 2 * kdim + lo:2 * kdim + hi]
            kkqk = _dot_nt(jnp.concatenate([kh, qh], axis=0), kh)
            a_mat = jnp.where(strict, bcol * kkqk[0:CHUNK] * dec, 0.0)
            attn = kkqk[CHUNK:2 * CHUNK] * dec
            p_mat = eye - a_mat
            a_pow = a_mat
            n_sq = (CHUNK - 1).bit_length() - 1
            for _ in range(n_sq):
                a_pow = _dot(a_pow, a_pow)
                p_mat = p_mat + _dot(p_mat, a_pow)
            rhs = jnp.concatenate([vh * bcol, kh * (bcol * egcol)], axis=1)
            uw = _dot(p_mat, rhs)
            a_hi, a_lo = _split(a_mat)
            u_hi, u_lo = _split(uw)
            resid = rhs - uw - (_dot(a_hi, u_hi) + _dot(a_hi, u_lo) + _dot(a_lo, u_hi))
            uw = uw + _dot(p_mat, resid)
            s_old = s_scr[hd]
            wq = _dot(jnp.concatenate([uw[:, HEAD_DIM:], qh * egcol], axis=0), s_old)
            v_new = uw[:, 0:HEAD_DIM] - wq[0:CHUNK]
            o = wq[CHUNK:2 * CHUNK] + _dot(attn, v_new)
            glast = gc[CHUNK - 1:CHUNK, hd:hd + 1]
            kd = kh * jnp.exp(glast - gcol)
            s_scr[hd] = s_old * jnp.exp(glast) + _dot_tn(kd, v_new)
            zb = qkvz[r0:r1, 3 * kdim + lo:3 * kdim + hi]
            yb_scr[r0:r1, lo:hi] = (_rms(o, gdn_g_ref[...]) * _silu(zb)).astype(BF16)

    @pl.when(t == nt - 1)
    def _():
        s_out_ref[0] = s_scr[...]

    pb = jnp.dot(yb_scr[...], w_pb_ref[...], preferred_element_type=F32)

    gates = jnp.dot(h, w_main_ref[:, 3 * d + 4 * kdim:5 * d + 4 * kdim], preferred_element_type=F32)
    merged = jax.nn.sigmoid(gates[:, 0:d]) * pa + jax.nn.sigmoid(gates[:, d:2 * d]) * pb
    out = jnp.dot(merged.astype(BF16), w_out_ref[...], preferred_element_type=F32)
    y_ref[0] = x + _rms(out, post_g_ref[...])


def _const_spec(shape):
    nd = len(shape)
    return pl.BlockSpec(shape, lambda b, t: (0,) * nd, pipeline_mode=pl.Buffered(1))


def _prompt_layer(x, lw):
    bsz, seq, d = x.shape
    tm = min(TILE_M, seq)
    nt = seq // tm
    kdim = N_HEADS * HEAD_DIM
    conv_dim = 3 * kdim
    in_specs = [
        pl.BlockSpec((1, tm, d), lambda b, t: (b, t, 0)),
        _const_spec((1, d)),
        _const_spec(lw["w_main"].shape),
        _const_spec(lw["w_ab"].shape),
        _const_spec((1, d)),
        _const_spec(lw["w_s"].shape),
        _const_spec(lw["b_s"].shape),
        _const_spec((CONV_W, conv_dim)),
        _const_spec((1, LANES)),
        _const_spec((1, LANES)),
        _const_spec((1, HEAD_DIM)),
        _const_spec((d, d)),
        _const_spec((kdim, d)),
        _const_spec((d, d)),
        _const_spec((1, d)),
    ]
    out_specs = [
        pl.BlockSpec((1, tm, d), lambda b, t: (b, t, 0)),
        pl.BlockSpec((1, N_HEADS, HEAD_DIM, HEAD_DIM), lambda b, t: (b, 0, 0, 0)),
        pl.BlockSpec((1, CONV_W - 1, conv_dim), lambda b, t: (b, 0, 0)),
    ]
    out_shape = [
        jax.ShapeDtypeStruct((bsz, seq, d), F32),
        jax.ShapeDtypeStruct((bsz, N_HEADS, HEAD_DIM, HEAD_DIM), F32),
        jax.ShapeDtypeStruct((bsz, CONV_W - 1, conv_dim), F32),
    ]
    scratch = [
        pltpu.VMEM((N_HEADS, HEAD_DIM, HEAD_DIM), F32),
        pltpu.VMEM((CARRY_ROW + tm, conv_dim), F32),
        pltpu.VMEM((tm, d), BF16),
        pltpu.VMEM((tm, kdim), BF16),
        pltpu.VMEM((tm, conv_dim), F32),
    ]
    return pl.pallas_call(
        _prompt_kernel,
        grid=(bsz, nt),
        in_specs=in_specs,
        out_specs=out_specs,
        out_shape=out_shape,
        scratch_shapes=scratch,
        compiler_params=pltpu.CompilerParams(
            dimension_semantics=("arbitrary", "arbitrary"),
            vmem_limit_bytes=VMEM_LIMIT),
        name="prompt_layer",
    )(x, lw["pre_g"], lw["w_main"], lw["w_ab"], lw["gmlp_g"], lw["w_s"], lw["b_s"],
      lw["conv_w"], lw["a_log"], lw["dt_bias"], lw["gdn_g"], lw["w_pa"], lw["w_pb"],
      lw["w_out"], lw["post_g"])


def _sample_pre_kernel(x_ref, cstate_ref, pre_g_ref, w_main_ref, w_ab_ref, gmlp_g_ref, ws00_ref,
                       bs0_ref, convw_ref, alog_ref, dtb_ref, w_pa_ref,
                       q_ref, k_ref, v_ref, beta_ref, eg_ref, zb_ref, sga_pa_ref, sgb_ref,
                       conv_new_ref, vrows_ref):
    d = x_ref.shape[1]
    kdim = N_HEADS * HEAD_DIM
    x = x_ref[...]
    h = _rms(x, pre_g_ref[...]).astype(BF16)
    uvz = jnp.dot(h, w_main_ref[:, 0:3 * d], preferred_element_type=F32)
    vn = _rms(uvz[:, d:2 * d], gmlp_g_ref[...])
    vrows_ref[...] = vn
    mixed = ws00_ref[...] * vn + bs0_ref[...]
    ya = uvz[:, 0:d] * mixed * _silu(uvz[:, 2 * d:3 * d])
    pa = jnp.dot(ya.astype(BF16), w_pa_ref[...], preferred_element_type=F32)
    qkvz = jnp.dot(h, w_main_ref[:, 3 * d:3 * d + 4 * kdim], preferred_element_type=F32)
    qkv_new = qkvz[:, 0:3 * kdim]
    cs = cstate_ref[...]
    conv = (cs[:, 0, :] * convw_ref[0:1, :] + cs[:, 1, :] * convw_ref[1:2, :]
            + cs[:, 2, :] * convw_ref[2:3, :] + qkv_new * convw_ref[3:4, :])
    conv_new_ref[:, 0, :] = cs[:, 1, :]
    conv_new_ref[:, 1, :] = cs[:, 2, :]
    conv_new_ref[:, 2, :] = qkv_new
    qkv = _silu(conv)
    scale = HEAD_DIM ** -0.5
    for hd in range(N_HEADS):
        lo, hi = hd * HEAD_DIM, (hd + 1) * HEAD_DIM
        q_ref[:, lo:hi] = _l2norm(qkv[:, lo:hi]) * scale
        k_ref[:, lo:hi] = _l2norm(qkv[:, kdim + lo:kdim + hi])
    v_ref[...] = qkv[:, 2 * kdim:3 * kdim]
    zb_ref[...] = qkvz[:, 3 * kdim:4 * kdim]
    ab = jnp.dot(h, w_ab_ref[...], preferred_element_type=F32)
    g_all = -jnp.exp(alog_ref[...]) * _softplus(ab + dtb_ref[...])
    eg_ref[...] = jnp.exp(g_all)
    beta_ref[...] = jax.nn.sigmoid(ab)
    gates = jnp.dot(h, w_main_ref[:, 3 * d + 4 * kdim:5 * d + 4 * kdim], preferred_element_type=F32)
    sga_pa_ref[...] = jax.nn.sigmoid(gates[:, 0:d]) * pa
    sgb_ref[...] = jax.nn.sigmoid(gates[:, d:2 * d])


def _sample_pre(x, cstate, lw):
    n, d = x.shape
    kdim = N_HEADS * HEAD_DIM
    conv_dim = 3 * kdim
    f = lambda shape: jax.ShapeDtypeStruct(shape, F32)
    out_shape = [f((n, kdim)), f((n, kdim)), f((n, kdim)), f((n, LANES)), f((n, LANES)),
                 f((n, kdim)), f((n, d)), f((n, d)), f((n, CONV_W - 1, conv_dim)), f((n, d))]
    return pl.pallas_call(
        _sample_pre_kernel,
        out_shape=out_shape,
        compiler_params=pltpu.CompilerParams(vmem_limit_bytes=VMEM_LIMIT),
        name="sample_pre",
    )(x, cstate, lw["pre_g"], lw["w_main"], lw["w_ab"], lw["gmlp_g"], lw["ws00"], lw["bs0"],
      lw["conv_w"], lw["a_log"], lw["dt_bias"], lw["w_pa"])


def _sample_state_kernel(s_ref, kqv_ref, s_out_ref, o_ref):
    n = s_ref.shape[0]
    rowid = _iota2((8, HEAD_DIM), 0)

    def body(i, carry):
        s = s_ref[i]
        kqv = kqv_ref[i]
        k = kqv[0:1, :]
        q = kqv[1:2, :]
        v = kqv[2:3, :]
        beta = kqv[3:4, :]
        eg = kqv[4:5, :]
        ks_qs = _dot(kqv, s)
        v_new = beta * (v - eg * ks_qs[0:1, :])
        qk = jnp.sum(q * k, axis=-1, keepdims=True)
        o = eg * ks_qs[1:2, :] + qk * v_new
        o_ref[i] = jnp.broadcast_to(o, (8, HEAD_DIM))
        k8 = jnp.where(rowid == 0, jnp.broadcast_to(k, (8, HEAD_DIM)), 0.0)
        v8 = jnp.where(rowid == 0, jnp.broadcast_to(v_new, (8, HEAD_DIM)), 0.0)
        s_out_ref[i] = s * eg[:, 0:1] + _dot_tn(k8, v8)
        return carry

    lax.fori_loop(0, n, body, 0)


def _sample_state(s, kqv, tile):
    n = s.shape[0]
    grid = (n // tile,)
    spec3 = lambda a, b: pl.BlockSpec((tile, a, b), lambda i: (i, 0, 0))
    return pl.pallas_call(
        _sample_state_kernel,
        grid=grid,
        in_specs=[spec3(HEAD_DIM, HEAD_DIM), spec3(8, HEAD_DIM)],
        out_specs=[spec3(HEAD_DIM, HEAD_DIM), spec3(8, HEAD_DIM)],
        out_shape=[jax.ShapeDtypeStruct(s.shape, F32), jax.ShapeDtypeStruct((n, 8, HEAD_DIM), F32)],
        compiler_params=pltpu.CompilerParams(
            dimension_semantics=("arbitrary",), vmem_limit_bytes=VMEM_LIMIT),
        name="sample_state",
    )(s, kqv)


def _sample_post_kernel(x_ref, o_ref, zb_ref, sga_pa_ref, sgb_ref, gdn_g_ref, w_pb_ref, w_out_ref,
                        post_g_ref, y_ref, yb_scr):
    for hd in range(N_HEADS):
        lo, hi = hd * HEAD_DIM, (hd + 1) * HEAD_DIM
        yb_scr[:, lo:hi] = (_rms(o_ref[:, lo:hi], gdn_g_ref[...]) * _silu(zb_ref[:, lo:hi])).astype(BF16)
    pb = jnp.dot(yb_scr[...], w_pb_ref[...], preferred_element_type=F32)
    merged = sga_pa_ref[...] + sgb_ref[...] * pb
    out = jnp.dot(merged.astype(BF16), w_out_ref[...], preferred_element_type=F32)
    y_ref[...] = x_ref[...] + _rms(out, post_g_ref[...])


def _sample_post(x, o, zb, sga_pa, sgb, lw):
    n, d = x.shape
    return pl.pallas_call(
        _sample_post_kernel,
        out_shape=jax.ShapeDtypeStruct((n, d), F32),
        scratch_shapes=[pltpu.VMEM((n, N_HEADS * HEAD_DIM), BF16)],
        compiler_params=pltpu.CompilerParams(vmem_limit_bytes=VMEM_LIMIT),
        name="sample_post",
    )(x, o, zb, sga_pa, sgb, lw["gdn_g"], lw["w_pb"], lw["w_out"], lw["post_g"])


def _sample_layer(x, s, cstate, lw):
    n, d = x.shape
    kdim = N_HEADS * HEAD_DIM
    q, k, v, beta, eg, zb, sga_pa, sgb, conv_new, vrows = _sample_pre(x, cstate, lw)
    nh = n * N_HEADS
    row = lambda a: a.reshape(nh, 1, HEAD_DIM)
    lanes = lambda a: jnp.broadcast_to(a.reshape(nh, 1, 1), (nh, 1, HEAD_DIM))
    kqv = jnp.concatenate(
        [row(k), row(q), row(v), lanes(beta[:, N_HEADS:2 * N_HEADS]), lanes(eg[:, 0:N_HEADS]),
         jnp.zeros((nh, 3, HEAD_DIM), F32)], axis=1)
    s_new, o = _sample_state(s.reshape(nh, HEAD_DIM, HEAD_DIM), kqv, tile=64)
    y = _sample_post(x, o[:, 0, :].reshape(n, kdim), zb, sga_pa, sgb, lw)
    return y, s_new.reshape(s.shape), conv_new, vrows


def _layer_weights(l, pre_norm, w_in, gmlp_norm, w_spatial, b_spatial, conv_w, a_log, dt_bias,
                   gdn_norm, w_proj_a, w_proj_b, w_out, post_norm):
    d = w_in.shape[1]
    kdim = N_HEADS * HEAD_DIM
    ab0 = 3 * d + 4 * kdim
    w = w_in[l]
    pad_lanes = lambda vec: jnp.zeros((1, LANES), F32).at[0, 0:N_HEADS].set(vec)
    return {
        "pre_g": pre_norm[l][None, :],
        "w_main": jnp.concatenate([w[:, 0:ab0], w[:, ab0 + 2 * N_HEADS:]], axis=1).astype(BF16),
        "w_ab": jnp.pad(w[:, ab0:ab0 + 2 * N_HEADS], ((0, 0), (0, LANES - 2 * N_HEADS))).astype(BF16),
        "gmlp_g": gmlp_norm[l][None, :],
        "w_s": w_spatial[l],
        "b_s": jnp.broadcast_to(b_spatial[l][:, :, None], (N_HEADS, CHUNK, HEAD_DIM)),
        "ws00": jnp.repeat(w_spatial[l][:, 0, 0], HEAD_DIM)[None, :],
        "bs0": jnp.repeat(b_spatial[l][:, 0], HEAD_DIM)[None, :],
        "conv_w": conv_w[l],
        "a_log": pad_lanes(a_log[l]),
        "dt_bias": pad_lanes(dt_bias[l]),
        "gdn_g": gdn_norm[l][None, :],
        "w_pa": w_proj_a[l].astype(BF16),
        "w_pb": w_proj_b[l].astype(BF16),
        "w_out": w_out[l].astype(BF16),
        "post_g": post_norm[l][None, :],
    }


def kernel(x_prompt, x_sample, state_gdn, state_conv, pre_norm, w_in, gmlp_norm, w_spatial, b_spatial, conv_w, a_log, dt_bias, gdn_norm, w_proj_a, w_proj_b, w_out, post_norm):
    depth = w_in.shape[0]
    xp = x_prompt
    xs = x_sample[:, 0, :]
    gdn_p, conv_p, gdn_s, conv_s, vrows_s = [], [], [], [], []
    for l in range(depth):
        lw = _layer_weights(l, pre_norm, w_in, gmlp_norm, w_spatial, b_spatial, conv_w, a_log,
                            dt_bias, gdn_norm, w_proj_a, w_proj_b, w_out, post_norm)
        xp, sg_p, cb_p = _prompt_layer(xp, lw)
        xs, sg_s, cb_s, vr_s = _sample_layer(xs, state_gdn[l], state_conv[l], lw)
        gdn_p.append(sg_p)
        conv_p.append(cb_p)
        gdn_s.append(sg_s)
        conv_s.append(cb_s)
        vrows_s.append(vr_s[:, None, :])
    return (xp, xs[:, None, :], jnp.stack(gdn_p), jnp.stack(conv_p), jnp.stack(gdn_s),
            jnp.stack(conv_s), jnp.stack(vrows_s))
```

```python
import functools

import jax
import jax.numpy as jnp
from jax import lax
from jax.experimental import pallas as pl
from jax.experimental.pallas import tpu as pltpu

F32 = jnp.float32
BF16 = jnp.bfloat16
EPS = 1e-6

LANES = 128
HEAD_DIM = 128
N_HEADS = 8
CONV_W = 4
CHUNK = 128
TILE_M = 256
CARRY_ROW = 8
VMEM_LIMIT = 56 * 1024 * 1024


def _dot(a, b):
    return jnp.dot(a.astype(BF16), b.astype(BF16), preferred_element_type=F32)


def _dot_nt(a, b):
    return lax.dot_general(a.astype(BF16), b.astype(BF16), (((1,), (1,)), ((), ())),
                           preferred_element_type=F32)


def _dot_tn(a, b):
    return lax.dot_general(a.astype(BF16), b.astype(BF16), (((0,), (0,)), ((), ())),
                           preferred_element_type=F32)


def _split(a):
    hi = a.astype(BF16)
    lo = (a - hi.astype(F32)).astype(BF16)
    return hi, lo


def _dot_exact(a, b):
    return jnp.dot(a, b, preferred_element_type=F32, precision=lax.Precision.HIGHEST)


def _rms(x, g):
    return x * lax.rsqrt(jnp.mean(x * x, axis=-1, keepdims=True) + EPS) * g


def _silu(x):
    return x * jax.nn.sigmoid(x)


def _softplus(x):
    return jnp.maximum(x, 0.0) + jnp.log(1.0 + jnp.exp(-jnp.abs(x)))


def _l2norm(x):
    return x * lax.rsqrt(jnp.sum(x * x, axis=-1, keepdims=True) + EPS)


def _iota2(shape, dim):
    return lax.broadcasted_iota(jnp.int32, shape, dim)


def _prompt_kernel(x_ref, pre_g_ref, w_main_ref, w_ab_ref, gmlp_g_ref, ws_ref, bs_ref,
                   convw_ref, alog_ref, dtb_ref, gdn_g_ref, w_pa_ref, w_pb_ref, w_out_ref,
                   post_g_ref,
                   y_ref, s_out_ref, conv_out_ref,
                   s_scr, cbuf_scr, ya_scr, yb_scr, qkv_scr):
    t = pl.program_id(1)
    nt = pl.num_programs(1)
    tm = x_ref.shape[1]
    d = x_ref.shape[2]
    n_chunks = tm // CHUNK
    kdim = N_HEADS * HEAD_DIM

    @pl.when(t == 0)
    def _():
        s_scr[...] = jnp.zeros_like(s_scr)
        cbuf_scr[0:CARRY_ROW, :] = jnp.zeros((CARRY_ROW, cbuf_scr.shape[1]), F32)

    x = x_ref[0]
    h = _rms(x, pre_g_ref[...]).astype(BF16)

    row = _iota2((CHUNK, CHUNK), 0)
    col = _iota2((CHUNK, CHUNK), 1)
    incl = row >= col
    strict = row > col

    uvz = jnp.dot(h, w_main_ref[:, 0:3 * d], preferred_element_type=F32)
    vn = _rms(uvz[:, d:2 * d], gmlp_g_ref[...])
    vnb = vn.astype(BF16)
    for g in range(N_HEADS):
        wsg = jnp.where(incl, ws_ref[g], 0.0).astype(BF16)
        lo, hi = g * HEAD_DIM, (g + 1) * HEAD_DIM
        for c in range(n_chunks):
            r0, r1 = c * CHUNK, (c + 1) * CHUNK
            mixed = jnp.dot(wsg, vnb[r0:r1, lo:hi], preferred_element_type=F32) + bs_ref[g]
            u = uvz[r0:r1, lo:hi]
            z = uvz[r0:r1, 2 * d + lo:2 * d + hi]
            ya_scr[r0:r1, lo:hi] = (u * mixed * _silu(z)).astype(BF16)
    pa = jnp.dot(ya_scr[...], w_pa_ref[...], preferred_element_type=F32)

    qkvz = jnp.dot(h, w_main_ref[:, 3 * d:3 * d + 4 * kdim], preferred_element_type=F32)
    cbuf_scr[CARRY_ROW:CARRY_ROW + tm, :] = qkvz[:, 0:3 * kdim]
    conv = cbuf_scr[CARRY_ROW - 3:CARRY_ROW - 3 + tm, :] * convw_ref[0:1, :]
    for i in range(1, CONV_W):
        conv = conv + cbuf_scr[CARRY_ROW - 3 + i:CARRY_ROW - 3 + i + tm, :] * convw_ref[i:i + 1, :]
    tail = cbuf_scr[CARRY_ROW + tm - 3:CARRY_ROW + tm, :]
    cbuf_scr[CARRY_ROW - 3:CARRY_ROW, :] = tail

    @pl.when(t == nt - 1)
    def _():
        conv_out_ref[0] = tail

    qkv_scr[...] = _silu(conv)

    ab = jnp.dot(h, w_ab_ref[...], preferred_element_type=F32)
    g_all = -jnp.exp(alog_ref[...]) * _softplus(ab + dtb_ref[...])
    beta_all = jax.nn.sigmoid(ab)
    g_all_t = g_all.T
    ltri = incl.astype(F32)
    utri = (row <= col).astype(F32)
    eye = (row == col).astype(F32)
    scale = HEAD_DIM ** -0.5

    items = [(c, hd) for c in range(n_chunks) for hd in range(N_HEADS)]
    gcs, egcs = [], []
    for c in range(n_chunks):
        r0, r1 = c * CHUNK, (c + 1) * CHUNK
        gcs.append(_dot_exact(ltri, g_all[r0:r1, :]))
        egcs.append(jnp.exp(gcs[c]))
    gcts = [_dot_exact(g_all_t[0:N_HEADS, c * CHUNK:(c + 1) * CHUNK], utri)
            for c in range(n_chunks)]

    qs, ks, kkqks = [], [], []
    for c, hd in items:
        r0, r1 = c * CHUNK, (c + 1) * CHUNK
        lo, hi = hd * HEAD_DIM, (hd + 1) * HEAD_DIM
        qh = _l2norm(qkv_scr[r0:r1, lo:hi]) * scale
        kh = _l2norm(qkv_scr[r0:r1, kdim + lo:kdim + hi])
        qs.append(qh)
        ks.append(kh)
        kkqks.append(_dot_nt(jnp.concatenate([kh, qh], axis=0), kh))

    a_mats, attns, rhss = [], [], []
    for i, (c, hd) in enumerate(items):
        r0, r1 = c * CHUNK, (c + 1) * CHUNK
        lo, hi = hd * HEAD_DIM, (hd + 1) * HEAD_DIM
        gcol = gcs[c][:, hd:hd + 1]
        grow = gcts[c][hd:hd + 1, :]
        egcol = egcs[c][:, hd:hd + 1]
        bcol = beta_all[r0:r1, N_HEADS + hd:N_HEADS + hd + 1]
        dec = jnp.where(incl, jnp.exp(jnp.where(incl, gcol - grow, 0.0)), 0.0)
        a_mats.append(jnp.where(strict, bcol * kkqks[i][0:CHUNK] * dec, 0.0))
        attns.append(kkqks[i][CHUNK:2 * CHUNK] * dec)
        vh = qkv_scr[r0:r1, 2 * kdim + lo:2 * kdim + hi]
        rhss.append(jnp.concatenate([vh * bcol, ks[i] * (bcol * egcol)], axis=1))

    p_mats = [eye - a for a in a_mats]
    a_pows = list(a_mats)
    n_sq = (CHUNK - 1).bit_length() - 1
    for _ in range(n_sq):
        a_pows = [_dot(a, a) for a in a_pows]
        p_mats = [p + _dot(p, a) for p, a in zip(p_mats, a_pows)]
    uws = [_dot(p, r) for p, r in zip(p_mats, rhss)]
    resids = []
    for a, uw, r in zip(a_mats, uws, rhss):
        a_hi, a_lo = _split(a)
        u_hi, u_lo = _split(uw)
        resids.append(r - uw - (_dot(a_hi, u_hi) + _dot(a_hi, u_lo) + _dot(a_lo, u_hi)))
    uws = [uw + _dot(p, rs) for uw, p, rs in zip(uws, p_mats, resids)]

    for c in range(n_chunks):
        r0, r1 = c * CHUNK, (c + 1) * CHUNK
        idx = [c * N_HEADS + hd for hd in range(N_HEADS)]
        s_olds = [s_scr[hd] for hd in range(N_HEADS)]
        wqs = [_dot(jnp.concatenate([uws[i][:, HEAD_DIM:], qs[i] * egcs[c][:, hd:hd + 1]], axis=0),
                    s_olds[hd]) for hd, i in enumerate(idx)]
        v_news = [uws[i][:, 0:HEAD_DIM] - wqs[hd][0:CHUNK] for hd, i in enumerate(idx)]
        os_ = [wqs[hd][CHUNK:2 * CHUNK] + _dot(attns[i], v_news[hd]) for hd, i in enumerate(idx)]
        for hd, i in enumerate(idx):
            glast = gcs[c][CHUNK - 1:CHUNK, hd:hd + 1]
            kd = ks[i] * jnp.exp(glast - gcs[c][:, hd:hd + 1])
            s_scr[hd] = s_olds[hd] * jnp.exp(glast) + _dot_tn(kd, v_news[hd])
        for hd in range(N_HEADS):
            lo, hi = hd * HEAD_DIM, (hd + 1) * HEAD_DIM
            zb = qkvz[r0:r1, 3 * kdim + lo:3 * kdim + hi]
            yb_scr[r0:r1, lo:hi] = (_rms(os_[hd], gdn_g_ref[...]) * _silu(zb)).astype(BF16)

    @pl.when(t == nt - 1)
    def _():
        s_out_ref[0] = s_scr[...]

    pb = jnp.dot(yb_scr[...], w_pb_ref[...], preferred_element_type=F32)

    gates = jnp.dot(h, w_main_ref[:, 3 * d + 4 * kdim:5 * d + 4 * kdim], preferred_element_type=F32)
    merged = jax.nn.sigmoid(gates[:, 0:d]) * pa + jax.nn.sigmoid(gates[:, d:2 * d]) * pb
    out = jnp.dot(merged.astype(BF16), w_out_ref[...], preferred_element_type=F32)
    y_ref[0] = x + _rms(out, post_g_ref[...])


def _const_spec(shape):
    nd = len(shape)
    return pl.BlockSpec(shape, lambda b, t: (0,) * nd, pipeline_mode=pl.Buffered(1))


def _prompt_layer(x, lw):
    bsz, seq, d = x.shape
    tm = min(TILE_M, seq)
    nt = seq // tm
    kdim = N_HEADS * HEAD_DIM
    conv_dim = 3 * kdim
    in_specs = [
        pl.BlockSpec((1, tm, d), lambda b, t: (b, t, 0)),
        _const_spec((1, d)),
        _const_spec(lw["w_main"].shape),
        _const_spec(lw["w_ab"].shape),
        _const_spec((1, d)),
        _const_spec(lw["w_s"].shape),
        _const_spec(lw["b_s"].shape),
        _const_spec((CONV_W, conv_dim)),
        _const_spec((1, LANES)),
        _const_spec((1, LANES)),
        _const_spec((1, HEAD_DIM)),
        _const_spec((d, d)),
        _const_spec((kdim, d)),
        _const_spec((d, d)),
        _const_spec((1, d)),
    ]
    out_specs = [
        pl.BlockSpec((1, tm, d), lambda b, t: (b, t, 0)),
        pl.BlockSpec((1, N_HEADS, HEAD_DIM, HEAD_DIM), lambda b, t: (b, 0, 0, 0)),
        pl.BlockSpec((1, CONV_W - 1, conv_dim), lambda b, t: (b, 0, 0)),
    ]
    out_shape = [
        jax.ShapeDtypeStruct((bsz, seq, d), F32),
        jax.ShapeDtypeStruct((bsz, N_HEADS, HEAD_DIM, HEAD_DIM), F32),
        jax.ShapeDtypeStruct((bsz, CONV_W - 1, conv_dim), F32),
    ]
    scratch = [
        pltpu.VMEM((N_HEADS, HEAD_DIM, HEAD_DIM), F32),
        pltpu.VMEM((CARRY_ROW + tm, conv_dim), F32),
        pltpu.VMEM((tm, d), BF16),
        pltpu.VMEM((tm, kdim), BF16),
        pltpu.VMEM((tm, conv_dim), F32),
    ]
    return pl.pallas_call(
        _prompt_kernel,
        grid=(bsz, nt),
        in_specs=in_specs,
        out_specs=out_specs,
        out_shape=out_shape,
        scratch_shapes=scratch,
        compiler_params=pltpu.CompilerParams(
            dimension_semantics=("arbitrary", "arbitrary"),
            vmem_limit_bytes=VMEM_LIMIT),
        name="prompt_layer",
    )(x, lw["pre_g"], lw["w_main"], lw["w_ab"], lw["gmlp_g"], lw["w_s"], lw["b_s"],
      lw["conv_w"], lw["a_log"], lw["dt_bias"], lw["gdn_g"], lw["w_pa"], lw["w_pb"],
      lw["w_out"], lw["post_g"])


def _sample_pre_kernel(x_ref, cstate_ref, pre_g_ref, w_main_ref, w_ab_ref, gmlp_g_ref, ws00_ref,
                       bs0_ref, convw_ref, alog_ref, dtb_ref, w_pa_ref,
                       q_ref, k_ref, v_ref, beta_ref, eg_ref, zb_ref, sga_pa_ref, sgb_ref,
                       conv_new_ref, vrows_ref):
    d = x_ref.shape[1]
    kdim = N_HEADS * HEAD_DIM
    x = x_ref[...]
    h = _rms(x, pre_g_ref[...]).astype(BF16)
    uvz = jnp.dot(h, w_main_ref[:, 0:3 * d], preferred_element_type=F32)
    vn = _rms(uvz[:, d:2 * d], gmlp_g_ref[...])
    vrows_ref[...] = vn
    mixed = ws00_ref[...] * vn + bs0_ref[...]
    ya = uvz[:, 0:d] * mixed * _silu(uvz[:, 2 * d:3 * d])
    pa = jnp.dot(ya.astype(BF16), w_pa_ref[...], preferred_element_type=F32)
    qkvz = jnp.dot(h, w_main_ref[:, 3 * d:3 * d + 4 * kdim], preferred_element_type=F32)
    qkv_new = qkvz[:, 0:3 * kdim]
    cs = cstate_ref[...]
    conv = (cs[:, 0, :] * convw_ref[0:1, :] + cs[:, 1, :] * convw_ref[1:2, :]
            + cs[:, 2, :] * convw_ref[2:3, :] + qkv_new * convw_ref[3:4, :])
    conv_new_ref[:, 0, :] = cs[:, 1, :]
    conv_new_ref[:, 1, :] = cs[:, 2, :]
    conv_new_ref[:, 2, :] = qkv_new
    qkv = _silu(conv)
    scale = HEAD_DIM ** -0.5
    for hd in range(N_HEADS):
        lo, hi = hd * HEAD_DIM, (hd + 1) * HEAD_DIM
        q_ref[:, lo:hi] = _l2norm(qkv[:, lo:hi]) * scale
        k_ref[:, lo:hi] = _l2norm(qkv[:, kdim + lo:kdim + hi])
    v_ref[...] = qkv[:, 2 * kdim:3 * kdim]
    zb_ref[...] = qkvz[:, 3 * kdim:4 * kdim]
    ab = jnp.dot(h, w_ab_ref[...], preferred_element_type=F32)
    g_all = -jnp.exp(alog_ref[...]) * _softplus(ab + dtb_ref[...])
    eg_ref[...] = jnp.exp(g_all)
    beta_ref[...] = jax.nn.sigmoid(ab)
    gates = jnp.dot(h, w_main_ref[:, 3 * d + 4 * kdim:5 * d + 4 * kdim], preferred_element_type=F32)
    sga_pa_ref[...] = jax.nn.sigmoid(gates[:, 0:d]) * pa
    sgb_ref[...] = jax.nn.sigmoid(gates[:, d:2 * d])


def _sample_pre(x, cstate, lw):
    n, d = x.shape
    kdim = N_HEADS * HEAD_DIM
    conv_dim = 3 * kdim
    f = lambda shape: jax.ShapeDtypeStruct(shape, F32)
    out_shape = [f((n, kdim)), f((n, kdim)), f((n, kdim)), f((n, LANES)), f((n, LANES)),
                 f((n, kdim)), f((n, d)), f((n, d)), f((n, CONV_W - 1, conv_dim)), f((n, d))]
    return pl.pallas_call(
        _sample_pre_kernel,
        out_shape=out_shape,
        compiler_params=pltpu.CompilerParams(vmem_limit_bytes=VMEM_LIMIT),
        name="sample_pre",
    )(x, cstate, lw["pre_g"], lw["w_main"], lw["w_ab"], lw["gmlp_g"], lw["ws00"], lw["bs0"],
      lw["conv_w"], lw["a_log"], lw["dt_bias"], lw["w_pa"])


def _sample_state_kernel(s_ref, kqv_ref, s_out_ref, o_ref):
    n = s_ref.shape[0]
    rowid = _iota2((8, HEAD_DIM), 0)

    def body(i, carry):
        s = s_ref[i]
        kqv = kqv_ref[i]
        k = kqv[0:1, :]
        q = kqv[1:2, :]
        v = kqv[2:3, :]
        beta = kqv[3:4, :]
        eg = kqv[4:5, :]
        ks_qs = _dot(kqv, s)
        v_new = beta * (v - eg * ks_qs[0:1, :])
        qk = jnp.sum(q * k, axis=-1, keepdims=True)
        o = eg * ks_qs[1:2, :] + qk * v_new
        o_ref[i] = jnp.broadcast_to(o, (8, HEAD_DIM))
        k8 = jnp.where(rowid == 0, jnp.broadcast_to(k, (8, HEAD_DIM)), 0.0)
        v8 = jnp.where(rowid == 0, jnp.broadcast_to(v_new, (8, HEAD_DIM)), 0.0)
        s_out_ref[i] = s * eg[:, 0:1] + _dot_tn(k8, v8)
        return carry

    lax.fori_loop(0, n, body, 0)


def _sample_state(s, kqv, tile):
    n = s.shape[0]
    grid = (n // tile,)
    spec3 = lambda a, b: pl.BlockSpec((tile, a, b), lambda i: (i, 0, 0))
    return pl.pallas_call(
        _sample_state_kernel,
        grid=grid,
        in_specs=[spec3(HEAD_DIM, HEAD_DIM), spec3(8, HEAD_DIM)],
        out_specs=[spec3(HEAD_DIM, HEAD_DIM), spec3(8, HEAD_DIM)],
        out_shape=[jax.ShapeDtypeStruct(s.shape, F32), jax.ShapeDtypeStruct((n, 8, HEAD_DIM), F32)],
        compiler_params=pltpu.CompilerParams(
            dimension_semantics=("arbitrary",), vmem_limit_bytes=VMEM_LIMIT),
        name="sample_state",
    )(s, kqv)


def _sample_post_kernel(x_ref, o_ref, zb_ref, sga_pa_ref, sgb_ref, gdn_g_ref, w_pb_ref, w_out_ref,
                        post_g_ref, y_ref, yb_scr):
    for hd in range(N_HEADS):
        lo, hi = hd * HEAD_DIM, (hd + 1) * HEAD_DIM
        yb_scr[:, lo:hi] = (_rms(o_ref[:, lo:hi], gdn_g_ref[...]) * _silu(zb_ref[:, lo:hi])).astype(BF16)
    pb = jnp.dot(yb_scr[...], w_pb_ref[...], preferred_element_type=F32)
    merged = sga_pa_ref[...] + sgb_ref[...] * pb
    out = jnp.dot(merged.astype(BF16), w_out_ref[...], preferred_element_type=F32)
    y_ref[...] = x_ref[...] + _rms(out, post_g_ref[...])


def _sample_post(x, o, zb, sga_pa, sgb, lw):
    n, d = x.shape
    return pl.pallas_call(
        _sample_post_kernel,
        out_shape=jax.ShapeDtypeStruct((n, d), F32),
        scratch_shapes=[pltpu.VMEM((n, N_HEADS * HEAD_DIM), BF16)],
        compiler_params=pltpu.CompilerParams(vmem_limit_bytes=VMEM_LIMIT),
        name="sample_post",
    )(x, o, zb, sga_pa, sgb, lw["gdn_g"], lw["w_pb"], lw["w_out"], lw["post_g"])


def _sample_layer(x, s, cstate, lw):
    n, d = x.shape
    kdim = N_HEADS * HEAD_DIM
    q, k, v, beta, eg, zb, sga_pa, sgb, conv_new, vrows = _sample_pre(x, cstate, lw)
    nh = n * N_HEADS
    row = lambda a: a.reshape(nh, 1, HEAD_DIM)
    lanes = lambda a: jnp.broadcast_to(a.reshape(nh, 1, 1), (nh, 1, HEAD_DIM))
    kqv = jnp.concatenate(
        [row(k), row(q), row(v), lanes(beta[:, N_HEADS:2 * N_HEADS]), lanes(eg[:, 0:N_HEADS]),
         jnp.zeros((nh, 3, HEAD_DIM), F32)], axis=1)
    s_new, o = _sample_state(s.reshape(nh, HEAD_DIM, HEAD_DIM), kqv, tile=64)
    y = _sample_post(x, o[:, 0, :].reshape(n, kdim), zb, sga_pa, sgb, lw)
    return y, s_new.reshape(s.shape), conv_new, vrows


def _layer_weights(l, pre_norm, w_in, gmlp_norm, w_spatial, b_spatial, conv_w, a_log, dt_bias,
                   gdn_norm, w_proj_a, w_proj_b, w_out, post_norm):
    d = w_in.shape[1]
    kdim = N_HEADS * HEAD_DIM
    ab0 = 3 * d + 4 * kdim
    w = w_in[l]
    pad_lanes = lambda vec: jnp.zeros((1, LANES), F32).at[0, 0:N_HEADS].set(vec)
    return {
        "pre_g": pre_norm[l][None, :],
        "w_main": jnp.concatenate([w[:, 0:ab0], w[:, ab0 + 2 * N_HEADS:]], axis=1).astype(BF16),
        "w_ab": jnp.pad(w[:, ab0:ab0 + 2 * N_HEADS], ((0, 0), (0, LANES - 2 * N_HEADS))).astype(BF16),
        "gmlp_g": gmlp_norm[l][None, :],
        "w_s": w_spatial[l],
        "b_s": jnp.broadcast_to(b_spatial[l][:, :, None], (N_HEADS, CHUNK, HEAD_DIM)),
        "ws00": jnp.repeat(w_spatial[l][:, 0, 0], HEAD_DIM)[None, :],
        "bs0": jnp.repeat(b_spatial[l][:, 0], HEAD_DIM)[None, :],
        "conv_w": conv_w[l],
        "a_log": pad_lanes(a_log[l]),
        "dt_bias": pad_lanes(dt_bias[l]),
        "gdn_g": gdn_norm[l][None, :],
        "w_pa": w_proj_a[l].astype(BF16),
        "w_pb": w_proj_b[l].astype(BF16),
        "w_out": w_out[l].astype(BF16),
        "post_g": post_norm[l][None, :],
    }


def kernel(x_prompt, x_sample, state_gdn, state_conv, pre_norm, w_in, gmlp_norm, w_spatial, b_spatial, conv_w, a_log, dt_bias, gdn_norm, w_proj_a, w_proj_b, w_out, post_norm):
    depth = w_in.shape[0]
    xp = x_prompt
    xs = x_sample[:, 0, :]
    gdn_p, conv_p, gdn_s, conv_s, vrows_s = [], [], [], [], []
    for l in range(depth):
        lw = _layer_weights(l, pre_norm, w_in, gmlp_norm, w_spatial, b_spatial, conv_w, a_log,
                            dt_bias, gdn_norm, w_proj_a, w_proj_b, w_out, post_norm)
        xp, sg_p, cb_p = _prompt_layer(xp, lw)
        xs, sg_s, cb_s, vr_s = _sample_layer(xs, state_gdn[l], state_conv[l], lw)
        gdn_p.append(sg_p)
        conv_p.append(cb_p)
        gdn_s.append(sg_s)
        conv_s.append(cb_s)
        vrows_s.append(vr_s[:, None, :])
    return (xp, xs[:, None, :], jnp.stack(gdn_p), jnp.stack(conv_p), jnp.stack(gdn_s),
            jnp.stack(conv_s), jnp.stack(vrows_s))
```

```python
import jax
import jax.numpy as jnp
from jax import lax
from jax.experimental import pallas as pl
from jax.experimental.pallas import tpu as pltpu

F32 = jnp.float32
BF16 = jnp.bfloat16
EPS = 1e-6

LANES = 128
SUBLANES = 8
HEAD_DIM = 128
N_HEADS = 8
CONV_W = 4
CHUNK = 128
TILE_M = 256
CARRY_ROW = 8
SAMPLE_ROWS = 8
HEAD_SPLIT = 2
VMEM_LIMIT = 56 * 1024 * 1024


def _dot(a, b):
    return jnp.dot(a.astype(BF16), b.astype(BF16), preferred_element_type=F32)


def _dot_nt(a, b):
    return lax.dot_general(a.astype(BF16), b.astype(BF16), (((1,), (1,)), ((), ())),
                           preferred_element_type=F32)


def _dot_tn(a, b):
    return lax.dot_general(a.astype(BF16), b.astype(BF16), (((0,), (0,)), ((), ())),
                           preferred_element_type=F32)


def _split(a):
    hi = a.astype(BF16)
    lo = (a - hi.astype(F32)).astype(BF16)
    return hi, lo


def _dot_exact(a, b):
    return jnp.dot(a, b, preferred_element_type=F32, precision=lax.Precision.HIGHEST)


def _rms(x, g):
    return x * lax.rsqrt(jnp.mean(x * x, axis=-1, keepdims=True) + EPS) * g


def _silu(x):
    return x * jax.nn.sigmoid(x)


def _softplus(x):
    return jnp.maximum(x, 0.0) + jnp.log(1.0 + jnp.exp(-jnp.abs(x)))


def _l2norm(x):
    return x * lax.rsqrt(jnp.sum(x * x, axis=-1, keepdims=True) + EPS)


def _iota2(shape, dim):
    return lax.broadcasted_iota(jnp.int32, shape, dim)


def _prompt_kernel(x_ref, pre_g_ref, w_main_ref, w_ab_ref, gmlp_g_ref, ws_ref, bs_ref,
                   convw_ref, alog_ref, dtb_ref, gdn_g_ref, w_pa_ref, w_pb_ref, w_out_ref,
                   post_g_ref,
                   y_ref, s_out_ref, conv_out_ref,
                   s_scr, cbuf_scr, ya_scr, yb_scr, qkv_scr):
    t = pl.program_id(1)
    nt = pl.num_programs(1)
    tm = x_ref.shape[1]
    d = x_ref.shape[2]
    n_chunks = tm // CHUNK
    kdim = N_HEADS * HEAD_DIM

    @pl.when(t == 0)
    def _():
        s_scr[...] = jnp.zeros_like(s_scr)
        cbuf_scr[0:CARRY_ROW, :] = jnp.zeros((CARRY_ROW, cbuf_scr.shape[1]), F32)

    x = x_ref[0]
    h = _rms(x, pre_g_ref[...]).astype(BF16)

    row = _iota2((CHUNK, CHUNK), 0)
    col = _iota2((CHUNK, CHUNK), 1)
    incl = row >= col
    strict = row > col

    uvz = jnp.dot(h, w_main_ref[:, 0:3 * d], preferred_element_type=F32)
    vn = _rms(uvz[:, d:2 * d], gmlp_g_ref[...])
    vnb = vn.astype(BF16)
    for g in range(N_HEADS):
        wsg = jnp.where(incl, ws_ref[g], 0.0).astype(BF16)
        lo, hi = g * HEAD_DIM, (g + 1) * HEAD_DIM
        for c in range(n_chunks):
            r0, r1 = c * CHUNK, (c + 1) * CHUNK
            mixed = jnp.dot(wsg, vnb[r0:r1, lo:hi], preferred_element_type=F32) + bs_ref[g]
            u = uvz[r0:r1, lo:hi]
            z = uvz[r0:r1, 2 * d + lo:2 * d + hi]
            ya_scr[r0:r1, lo:hi] = (u * mixed * _silu(z)).astype(BF16)
    pa = jnp.dot(ya_scr[...], w_pa_ref[...], preferred_element_type=F32)

    qkvz = jnp.dot(h, w_main_ref[:, 3 * d:3 * d + 4 * kdim], preferred_element_type=F32)
    cbuf_scr[CARRY_ROW:CARRY_ROW + tm, :] = qkvz[:, 0:3 * kdim]
    conv = cbuf_scr[CARRY_ROW - 3:CARRY_ROW - 3 + tm, :] * convw_ref[0:1, :]
    for i in range(1, CONV_W):
        conv = conv + cbuf_scr[CARRY_ROW - 3 + i:CARRY_ROW - 3 + i + tm, :] * convw_ref[i:i + 1, :]
    tail = cbuf_scr[CARRY_ROW + tm - 3:CARRY_ROW + tm, :]
    cbuf_scr[CARRY_ROW - 3:CARRY_ROW, :] = tail

    @pl.when(t == nt - 1)
    def _():
        conv_out_ref[0] = tail

    qkv_scr[...] = _silu(conv)

    ab = jnp.dot(h, w_ab_ref[...], preferred_element_type=F32)
    g_all = -jnp.exp(alog_ref[...]) * _softplus(ab + dtb_ref[...])
    beta_all = jax.nn.sigmoid(ab)
    g_all_t = g_all.T
    ltri = incl.astype(F32)
    utri = (row <= col).astype(F32)
    eye = (row == col).astype(F32)
    scale = HEAD_DIM ** -0.5

    items = [(c, hd) for c in range(n_chunks) for hd in range(N_HEADS)]
    gcs, egcs = [], []
    for c in range(n_chunks):
        r0, r1 = c * CHUNK, (c + 1) * CHUNK
        gcs.append(_dot_exact(ltri, g_all[r0:r1, :]))
        egcs.append(jnp.exp(gcs[c]))
    gcts = [_dot_exact(g_all_t[0:N_HEADS, c * CHUNK:(c + 1) * CHUNK], utri)
            for c in range(n_chunks)]

    qs, ks, kkqks = [], [], []
    for c, hd in items:
        r0, r1 = c * CHUNK, (c + 1) * CHUNK
        lo, hi = hd * HEAD_DIM, (hd + 1) * HEAD_DIM
        qh = _l2norm(qkv_scr[r0:r1, lo:hi]) * scale
        kh = _l2norm(qkv_scr[r0:r1, kdim + lo:kdim + hi])
        qs.append(qh)
        ks.append(kh)
        kkqks.append(_dot_nt(jnp.concatenate([kh, qh], axis=0), kh))

    a_mats, attns, rhss = [], [], []
    for i, (c, hd) in enumerate(items):
        r0, r1 = c * CHUNK, (c + 1) * CHUNK
        lo, hi = hd * HEAD_DIM, (hd + 1) * HEAD_DIM
        gcol = gcs[c][:, hd:hd + 1]
        grow = gcts[c][hd:hd + 1, :]
        egcol = egcs[c][:, hd:hd + 1]
        bcol = beta_all[r0:r1, N_HEADS + hd:N_HEADS + hd + 1]
        dec = jnp.where(incl, jnp.exp(jnp.where(incl, gcol - grow, 0.0)), 0.0)
        a_mats.append(jnp.where(strict, bcol * kkqks[i][0:CHUNK] * dec, 0.0))
        attns.append(kkqks[i][CHUNK:2 * CHUNK] * dec)
        vh = qkv_scr[r0:r1, 2 * kdim + lo:2 * kdim + hi]
        rhss.append(jnp.concatenate([vh * bcol, ks[i] * (bcol * egcol)], axis=1))

    p_mats = [eye - a for a in a_mats]
    a_pows = list(a_mats)
    n_sq = (CHUNK - 1).bit_length() - 1
    for _ in range(n_sq):
        a_pows = [_dot(a, a) for a in a_pows]
        p_mats = [p + _dot(p, a) for p, a in zip(p_mats, a_pows)]
    uws = [_dot(p, r) for p, r in zip(p_mats, rhss)]
    resids = []
    for a, uw, r in zip(a_mats, uws, rhss):
        a_hi, a_lo = _split(a)
        u_hi, u_lo = _split(uw)
        resids.append(r - uw - (_dot(a_hi, u_hi) + _dot(a_hi, u_lo) + _dot(a_lo, u_hi)))
    uws = [uw + _dot(p, rs) for uw, p, rs in zip(uws, p_mats, resids)]

    for c in range(n_chunks):
        r0, r1 = c * CHUNK, (c + 1) * CHUNK
        idx = [c * N_HEADS + hd for hd in range(N_HEADS)]
        s_olds = [s_scr[hd] for hd in range(N_HEADS)]
        wqs = [_dot(jnp.concatenate([uws[i][:, HEAD_DIM:], qs[i] * egcs[c][:, hd:hd + 1]], axis=0),
                    s_olds[hd]) for hd, i in enumerate(idx)]
        v_news = [uws[i][:, 0:HEAD_DIM] - wqs[hd][0:CHUNK] for hd, i in enumerate(idx)]
        os_ = [wqs[hd][CHUNK:2 * CHUNK] + _dot(attns[i], v_news[hd]) for hd, i in enumerate(idx)]
        for hd, i in enumerate(idx):
            glast = gcs[c][CHUNK - 1:CHUNK, hd:hd + 1]
            kd = ks[i] * jnp.exp(glast - gcs[c][:, hd:hd + 1])
            s_scr[hd] = s_olds[hd] * jnp.exp(glast) + _dot_tn(kd, v_news[hd])
        for hd in range(N_HEADS):
            lo, hi = hd * HEAD_DIM, (hd + 1) * HEAD_DIM
            zb = qkvz[r0:r1, 3 * kdim + lo:3 * kdim + hi]
            yb_scr[r0:r1, lo:hi] = (_rms(os_[hd], gdn_g_ref[...]) * _silu(zb)).astype(BF16)

    @pl.when(t == nt - 1)
    def _():
        s_out_ref[0] = s_scr[...]

    pb = jnp.dot(yb_scr[...], w_pb_ref[...], preferred_element_type=F32)

    gates = jnp.dot(h, w_main_ref[:, 3 * d + 4 * kdim:5 * d + 4 * kdim], preferred_element_type=F32)
    merged = jax.nn.sigmoid(gates[:, 0:d]) * pa + jax.nn.sigmoid(gates[:, d:2 * d]) * pb
    out = jnp.dot(merged.astype(BF16), w_out_ref[...], preferred_element_type=F32)
    y_ref[0] = x + _rms(out, post_g_ref[...])


def _layer_spec(arr, layer, n_grid):
    shape = (None,) + tuple(arr.shape[1:])
    zeros = (0,) * (arr.ndim - 1)
    if n_grid == 2:
        index_map = lambda b, t: (layer,) + zeros
    else:
        index_map = lambda l, i, j: (l,) + zeros
    return pl.BlockSpec(shape, index_map, pipeline_mode=pl.Buffered(1))


_PROMPT_PARAMS = ("pre_g", "w_main", "w_ab", "gmlp_g", "w_s", "b_s", "conv_w", "a_log", "dt_bias",
                  "gdn_g", "w_pa", "w_pb", "w_out", "post_g")


def _prompt_layer(x, wts, layer):
    bsz, seq, d = x.shape
    tm = min(TILE_M, seq)
    nt = seq // tm
    kdim = N_HEADS * HEAD_DIM
    conv_dim = 3 * kdim
    params = [wts[name] for name in _PROMPT_PARAMS]
    in_specs = [pl.BlockSpec((1, tm, d), lambda b, t: (b, t, 0))]
    in_specs += [_layer_spec(p, layer, 2) for p in params]
    out_specs = [
        pl.BlockSpec((1, tm, d), lambda b, t: (b, t, 0)),
        pl.BlockSpec((1, N_HEADS, HEAD_DIM, HEAD_DIM), lambda b, t: (b, 0, 0, 0)),
        pl.BlockSpec((1, CONV_W - 1, conv_dim), lambda b, t: (b, 0, 0)),
    ]
    out_shape = [
        jax.ShapeDtypeStruct((bsz, seq, d), F32),
        jax.ShapeDtypeStruct((bsz, N_HEADS, HEAD_DIM, HEAD_DIM), F32),
        jax.ShapeDtypeStruct((bsz, CONV_W - 1, conv_dim), F32),
    ]
    scratch = [
        pltpu.VMEM((N_HEADS, HEAD_DIM, HEAD_DIM), F32),
        pltpu.VMEM((CARRY_ROW + tm, conv_dim), F32),
        pltpu.VMEM((tm, d), BF16),
        pltpu.VMEM((tm, kdim), BF16),
        pltpu.VMEM((tm, conv_dim), F32),
    ]
    return pl.pallas_call(
        _prompt_kernel,
        grid=(bsz, nt),
        in_specs=in_specs,
        out_specs=out_specs,
        out_shape=out_shape,
        scratch_shapes=scratch,
        compiler_params=pltpu.CompilerParams(
            dimension_semantics=("arbitrary", "arbitrary"),
            vmem_limit_bytes=VMEM_LIMIT),
        name="prompt_layer",
    )(x, *params)


def _sample_kernel(x_ref, s_ref, cs_ref, pre_g_ref, w_main_ref, w_ab_ref, gmlp_g_ref, ws00_ref,
                   bs0_ref, convw_ref, alog_ref, dtb_ref, gdn_g_ref, w_pa_ref, w_pb_ref, w_out_ref,
                   post_g_ref,
                   y_ref, s_out_ref, conv_out_ref, vrows_ref,
                   xs_scr, qkvn_scr, zb_scr, sga_pa_scr, sgb_scr, beta_scr, eg_scr, o_scr,
                   q_t, k_t, v_t, beta_t, eg_t, o_t):
    layer = pl.program_id(0)
    i = pl.program_id(1)
    hh = pl.program_id(2)
    n_tiles = pl.num_programs(1)
    n_split = pl.num_programs(2)
    d = x_ref.shape[1]
    kdim = N_HEADS * HEAD_DIM
    heads_per = N_HEADS // HEAD_SPLIT
    wid = heads_per * HEAD_DIM
    rows = SAMPLE_ROWS
    scale = HEAD_DIM ** -0.5

    @pl.when(jnp.logical_and(i == 0, hh == 0))
    def _():
        @pl.when(layer == 0)
        def _():
            xs_scr[...] = x_ref[...]

        x = xs_scr[...]
        h = _rms(x, pre_g_ref[...]).astype(BF16)
        uvz = jnp.dot(h, w_main_ref[:, 0:3 * d], preferred_element_type=F32)
        vn = _rms(uvz[:, d:2 * d], gmlp_g_ref[...])
        vrows_ref[0] = vn
        mixed = ws00_ref[...] * vn + bs0_ref[...]
        ya = uvz[:, 0:d] * mixed * _silu(uvz[:, 2 * d:3 * d])
        pa = jnp.dot(ya.astype(BF16), w_pa_ref[...], preferred_element_type=F32)
        qkvz = jnp.dot(h, w_main_ref[:, 3 * d:3 * d + 4 * kdim], preferred_element_type=F32)
        qkvn_scr[...] = qkvz[:, 0:3 * kdim]
        zb_scr[...] = qkvz[:, 3 * kdim:4 * kdim]
        ab = jnp.dot(h, w_ab_ref[...], preferred_element_type=F32)
        eg_scr[...] = jnp.exp(-jnp.exp(alog_ref[...]) * _softplus(ab + dtb_ref[...]))
        beta_scr[...] = jax.nn.sigmoid(ab)
        gates = jnp.dot(h, w_main_ref[:, 3 * d + 4 * kdim:5 * d + 4 * kdim],
                        preferred_element_type=F32)
        sga_pa_scr[...] = jax.nn.sigmoid(gates[:, 0:d]) * pa
        sgb_scr[...] = jax.nn.sigmoid(gates[:, d:2 * d])

    b0 = pl.multiple_of(i * rows, rows)

    @pl.when(hh == 0)
    def _():
        cs = cs_ref[0]
        qkv_new = qkvn_scr[pl.ds(b0, rows), :]
        conv = (cs[:, 0, :] * convw_ref[0:1, :] + cs[:, 1, :] * convw_ref[1:2, :]
                + cs[:, 2, :] * convw_ref[2:3, :] + qkv_new * convw_ref[3:4, :])
        conv_out_ref[0, :, 0, :] = cs[:, 1, :]
        conv_out_ref[0, :, 1, :] = cs[:, 2, :]
        conv_out_ref[0, :, 2, :] = qkv_new
        qkv = _silu(conv)
        beta_rows = beta_scr[pl.ds(b0, rows), :]
        eg_rows = eg_scr[pl.ds(b0, rows), :]
        for hd in range(N_HEADS):
            lo, hi = hd * HEAD_DIM, (hd + 1) * HEAD_DIM
            half, off = hd // heads_per, (hd % heads_per) * HEAD_DIM
            q_t[half, :, off:off + HEAD_DIM] = _l2norm(qkv[:, lo:hi]) * scale
            k_t[half, :, off:off + HEAD_DIM] = _l2norm(qkv[:, kdim + lo:kdim + hi])
            v_t[half, :, off:off + HEAD_DIM] = qkv[:, 2 * kdim + lo:2 * kdim + hi]
            beta_t[hd] = jnp.broadcast_to(beta_rows[:, N_HEADS + hd:N_HEADS + hd + 1], (rows, HEAD_DIM))
            eg_t[hd] = jnp.broadcast_to(eg_rows[:, hd:hd + 1], (rows, HEAD_DIM))

    rowid = _iota2((SUBLANES, HEAD_DIM), 0)
    states = [(r, j) for r in range(rows) for j in range(heads_per)]
    qv, kv, vv, bv, ev, sv, ksqs = [], [], [], [], [], [], []
    for r, j in states:
        off = j * HEAD_DIM
        q = q_t[hh, r:r + 1, off:off + HEAD_DIM]
        k = k_t[hh, r:r + 1, off:off + HEAD_DIM]
        qv.append(q)
        kv.append(k)
        vv.append(v_t[hh, r:r + 1, off:off + HEAD_DIM])
        bv.append(beta_t[hh * heads_per + j, r:r + 1, :])
        ev.append(eg_t[hh * heads_per + j, r:r + 1, :])
        s = s_ref[0, r, j]
        sv.append(s)
        kq8 = jnp.where(rowid == 0, jnp.broadcast_to(k, (SUBLANES, HEAD_DIM)),
                        jnp.where(rowid == 1, jnp.broadcast_to(q, (SUBLANES, HEAD_DIM)), 0.0))
        ksqs.append(_dot(kq8, s))
    for n, (r, j) in enumerate(states):
        v_new = bv[n] * (vv[n] - ev[n] * ksqs[n][0:1, :])
        qk = jnp.sum(qv[n] * kv[n], axis=-1, keepdims=True)
        o_t[r:r + 1, j * HEAD_DIM:(j + 1) * HEAD_DIM] = ev[n] * ksqs[n][1:2, :] + qk * v_new
        k_hi, k_lo = _split(kv[n])
        v_hi, v_lo = _split(v_new)
        bc = lambda a: jnp.broadcast_to(a.astype(F32), (SUBLANES, HEAD_DIM))
        k8 = jnp.where(rowid <= 1, bc(k_hi), jnp.where(rowid == 2, bc(k_lo), 0.0))
        v8 = jnp.where(rowid == 0, bc(v_hi), jnp.where(rowid == 1, bc(v_lo),
                                                       jnp.where(rowid == 2, bc(v_hi), 0.0)))
        s_out_ref[0, r, j] = sv[n] * ev[n] + _dot_tn(k8, v8)
    o_scr[hh, pl.ds(b0, rows), :] = o_t[...]

    @pl.when(jnp.logical_and(i == n_tiles - 1, hh == n_split - 1))
    def _():
        pb = jnp.zeros((x_ref.shape[0], d), F32)
        for hd in range(N_HEADS):
            lo, hi = hd * HEAD_DIM, (hd + 1) * HEAD_DIM
            half, off = hd // heads_per, (hd % heads_per) * HEAD_DIM
            o = o_scr[half, :, off:off + HEAD_DIM]
            yb = (_rms(o, gdn_g_ref[...]) * _silu(zb_scr[:, lo:hi])).astype(BF16)
            pb = pb + jnp.dot(yb, w_pb_ref[lo:hi, :], preferred_element_type=F32)
        merged = sga_pa_scr[...] + sgb_scr[...] * pb
        out = jnp.dot(merged.astype(BF16), w_out_ref[...], preferred_element_type=F32)
        y = xs_scr[...] + _rms(out, post_g_ref[...])
        xs_scr[...] = y
        y_ref[...] = y


_SAMPLE_PARAMS = ("pre_g", "w_main", "w_ab", "gmlp_g", "ws00", "bs0", "conv_w", "a_log", "dt_bias",
                  "gdn_g", "w_pa", "w_pb", "w_out", "post_g")


def _sample_path(x, state_gdn, state_conv, wts):
    n, d = x.shape
    depth = state_gdn.shape[0]
    kdim = N_HEADS * HEAD_DIM
    conv_dim = 3 * kdim
    heads_per = N_HEADS // HEAD_SPLIT
    rows = SAMPLE_ROWS
    params = [wts[name] for name in _SAMPLE_PARAMS]
    s_spec = pl.BlockSpec((1, rows, heads_per, HEAD_DIM, HEAD_DIM), lambda l, i, j: (l, i, j, 0, 0))
    c_spec = pl.BlockSpec((1, rows, CONV_W - 1, conv_dim), lambda l, i, j: (l, i, 0, 0))
    in_specs = [pl.BlockSpec((n, d), lambda l, i, j: (0, 0)), s_spec, c_spec]
    in_specs += [_layer_spec(p, None, 3) for p in params]
    out_specs = [
        pl.BlockSpec((n, d), lambda l, i, j: (0, 0)),
        s_spec,
        c_spec,
        pl.BlockSpec((1, n, d), lambda l, i, j: (l, 0, 0)),
    ]
    out_shape = [
        jax.ShapeDtypeStruct((n, d), F32),
        jax.ShapeDtypeStruct(state_gdn.shape, F32),
        jax.ShapeDtypeStruct(state_conv.shape, F32),
        jax.ShapeDtypeStruct((depth, n, d), F32),
    ]
    scratch = [
        pltpu.VMEM((n, d), F32),
        pltpu.VMEM((n, conv_dim), F32),
        pltpu.VMEM((n, kdim), F32),
        pltpu.VMEM((n, d), F32),
        pltpu.VMEM((n, d), F32),
        pltpu.VMEM((n, LANES), F32),
        pltpu.VMEM((n, LANES), F32),
        pltpu.VMEM((HEAD_SPLIT, n, heads_per * HEAD_DIM), F32),
        pltpu.VMEM((HEAD_SPLIT, rows, heads_per * HEAD_DIM), F32),
        pltpu.VMEM((HEAD_SPLIT, rows, heads_per * HEAD_DIM), F32),
        pltpu.VMEM((HEAD_SPLIT, rows, heads_per * HEAD_DIM), F32),
        pltpu.VMEM((N_HEADS, rows, HEAD_DIM), F32),
        pltpu.VMEM((N_HEADS, rows, HEAD_DIM), F32),
        pltpu.VMEM((rows, heads_per * HEAD_DIM), F32),
    ]
    return pl.pallas_call(
        _sample_kernel,
        grid=(depth, n // rows, HEAD_SPLIT),
        in_specs=in_specs,
        out_specs=out_specs,
        out_shape=out_shape,
        scratch_shapes=scratch,
        compiler_params=pltpu.CompilerParams(
            dimension_semantics=("arbitrary", "arbitrary", "arbitrary"),
            vmem_limit_bytes=VMEM_LIMIT),
        name="sample_path",
    )(x, state_gdn, state_conv, *params)


def _prepare_weights(pre_norm, w_in, gmlp_norm, w_spatial, b_spatial, conv_w, a_log, dt_bias,
                     gdn_norm, w_proj_a, w_proj_b, w_out, post_norm):
    depth, d, _ = w_in.shape
    kdim = N_HEADS * HEAD_DIM
    ab0 = 3 * d + 4 * kdim
    pad_lanes = lambda a: jnp.pad(a, ((0, 0), (0, LANES - N_HEADS)))[:, None, :]
    return {
        "pre_g": pre_norm[:, None, :],
        "w_main": jnp.concatenate([w_in[:, :, 0:ab0], w_in[:, :, ab0 + 2 * N_HEADS:]],
                                  axis=2).astype(BF16),
        "w_ab": jnp.pad(w_in[:, :, ab0:ab0 + 2 * N_HEADS],
                        ((0, 0), (0, 0), (0, LANES - 2 * N_HEADS))).astype(BF16),
        "gmlp_g": gmlp_norm[:, None, :],
        "w_s": w_spatial,
        "b_s": jnp.broadcast_to(b_spatial[:, :, :, None], (depth, N_HEADS, CHUNK, HEAD_DIM)),
        "ws00": jnp.repeat(w_spatial[:, :, 0, 0], HEAD_DIM, axis=1)[:, None, :],
        "bs0": jnp.repeat(b_spatial[:, :, 0], HEAD_DIM, axis=1)[:, None, :],
        "conv_w": conv_w,
        "a_log": pad_lanes(a_log),
        "dt_bias": pad_lanes(dt_bias),
        "gdn_g": gdn_norm[:, None, :],
        "w_pa": w_proj_a.astype(BF16),
        "w_pb": w_proj_b.astype(BF16),
        "w_out": w_out.astype(BF16),
        "post_g": post_norm[:, None, :],
    }


def kernel(x_prompt, x_sample, state_gdn, state_conv, pre_norm, w_in, gmlp_norm, w_spatial, b_spatial, conv_w, a_log, dt_bias, gdn_norm, w_proj_a, w_proj_b, w_out, post_norm):
    depth = w_in.shape[0]
    wts = _prepare_weights(pre_norm, w_in, gmlp_norm, w_spatial, b_spatial, conv_w, a_log, dt_bias,
                           gdn_norm, w_proj_a, w_proj_b, w_out, post_norm)
    xp = x_prompt
    gdn_p, conv_p = [], []
    for l in range(depth):
        xp, sg_p, cb_p = _prompt_layer(xp, wts, l)
        gdn_p.append(sg_p)
        conv_p.append(cb_p)
    ys, gdn_s, conv_s, vrows_s = _sample_path(x_sample[:, 0, :], state_gdn, state_conv, wts)
    return (xp, ys[:, None, :], jnp.stack(gdn_p), jnp.stack(conv_p), gdn_s, conv_s,
            vrows_s[:, :, None, :])
```

```python
import jax
import jax.numpy as jnp
from jax import lax
from jax.experimental import pallas as pl
from jax.experimental.pallas import tpu as pltpu

F32 = jnp.float32
BF16 = jnp.bfloat16
EPS = 1e-6

LANES = 128
SUBLANES = 8
HEAD_DIM = 128
N_HEADS = 8
CONV_W = 4
CHUNK = 128
TILE_M = 256
VROWS = CHUNK // SUBLANES
SAMPLE_ROWS = 8
HEAD_SPLIT = 2
VMEM_LIMIT = 56 * 1024 * 1024


def _dot(a, b):
    return jnp.dot(a.astype(BF16), b.astype(BF16), preferred_element_type=F32)


def _dot_nt(a, b):
    return lax.dot_general(a.astype(BF16), b.astype(BF16), (((1,), (1,)), ((), ())),
                           preferred_element_type=F32)


def _dot_tn(a, b):
    return lax.dot_general(a.astype(BF16), b.astype(BF16), (((0,), (0,)), ((), ())),
                           preferred_element_type=F32)


def _bdiag(a, b):
    z = jnp.zeros_like(a)
    return jnp.concatenate([jnp.concatenate([a, z], axis=1), jnp.concatenate([z, b], axis=1)], axis=0)


def _split(a):
    hi = a.astype(BF16)
    lo = (a - hi.astype(F32)).astype(BF16)
    return hi, lo


def _dot_exact(a, b):
    return jnp.dot(a, b, preferred_element_type=F32, precision=lax.Precision.HIGHEST)


def _rms(x, g):
    return x * lax.rsqrt(jnp.mean(x * x, axis=-1, keepdims=True) + EPS) * g


def _silu(x):
    return x * jax.nn.sigmoid(x)


def _softplus(x):
    return jnp.maximum(x, 0.0) + jnp.log(1.0 + jnp.exp(-jnp.abs(x)))


def _l2norm(x):
    return x * lax.rsqrt(jnp.sum(x * x, axis=-1, keepdims=True) + EPS)


def _iota2(shape, dim):
    return lax.broadcasted_iota(jnp.int32, shape, dim)


def _prompt_kernel(x_ref, pre_g_ref, w_uvz_ref, w_qkvz_ref, w_gates_ref, w_ab_ref, gmlp_g_ref, ws_ref, bs_ref,
                   convw_ref, alog_ref, dtb_ref, gdn_g_ref, w_pa_ref, w_pb_ref, w_out_ref,
                   post_g_ref,
                   y_ref, s_out_ref, conv_out_ref,
                   s_scr, carry_scr, ya_scr, yb_scr, qkv_scr):
    t = pl.program_id(1)
    nt = pl.num_programs(1)
    tm = x_ref.shape[1]
    d = x_ref.shape[2]
    n_chunks = tm // CHUNK
    kdim = N_HEADS * HEAD_DIM

    @pl.when(t == 0)
    def _():
        s_scr[...] = jnp.zeros_like(s_scr)
        carry_scr[...] = jnp.zeros_like(carry_scr)

    x = x_ref[0]
    h = _rms(x, pre_g_ref[...]).astype(BF16)

    row = _iota2((CHUNK, CHUNK), 0)
    col = _iota2((CHUNK, CHUNK), 1)
    token_of = lambda r: (r % SUBLANES) * VROWS + r // SUBLANES
    row_tok, col_tok = token_of(row), token_of(col)
    incl = row_tok >= col_tok
    strict = row_tok > col_tok

    qkvz = jnp.dot(h, w_qkvz_ref[...], preferred_element_type=F32)
    uvz = jnp.dot(h, w_uvz_ref[...], preferred_element_type=F32)
    rowid = _iota2((SUBLANES, 3 * kdim), 0)
    n_tail = CONV_W - 1
    prev = [carry_scr[j:j + 1, :] for j in range(n_tail)]
    for c in range(n_chunks):
        r0 = c * CHUNK
        pre = qkvz[r0:r0 + CHUNK, 0:3 * kdim]
        wrapped = []
        for j in range(n_tail):
            blk = pre[(VROWS - n_tail + j) * SUBLANES:(VROWS - n_tail + j + 1) * SUBLANES, :]
            shifted = pltpu.roll(blk, 1, axis=0)
            wrapped.append(jnp.where(rowid == 0, jnp.broadcast_to(prev[j], blk.shape), shifted))
            prev[j] = blk[SUBLANES - 1:SUBLANES, :]
        conv = pre * convw_ref[CONV_W - 1:CONV_W, :]
        for k in range(1, CONV_W):
            back_k = jnp.concatenate(wrapped[n_tail - k:] + [pre[0:(VROWS - k) * SUBLANES, :]], axis=0)
            conv = conv + back_k * convw_ref[CONV_W - 1 - k:CONV_W - k, :]
        qkv_scr[r0:r0 + CHUNK, :] = _silu(conv)
    tail = jnp.concatenate(prev, axis=0)
    carry_scr[0:n_tail, :] = tail

    @pl.when(t == nt - 1)
    def _():
        conv_out_ref[0] = tail

    gates = jnp.dot(h, w_gates_ref[...], preferred_element_type=F32)
    ab = jnp.dot(h, w_ab_ref[...], preferred_element_type=F32)
    g_all = -jnp.exp(alog_ref[...]) * _softplus(ab + dtb_ref[...])
    beta_all = jax.nn.sigmoid(ab)
    g_all_t = g_all.T
    ltri = incl.astype(F32)
    utri = (row_tok <= col_tok).astype(F32)
    eye = (row == col).astype(F32)
    scale = HEAD_DIM ** -0.5

    items = [(c, hd) for c in range(n_chunks) for hd in range(N_HEADS)]
    gcs, egcs = [], []
    for c in range(n_chunks):
        r0, r1 = c * CHUNK, (c + 1) * CHUNK
        gcs.append(_dot_exact(ltri, g_all[r0:r1, :]))
        egcs.append(jnp.exp(gcs[c]))
    gcts = [_dot_exact(g_all_t[0:N_HEADS, c * CHUNK:(c + 1) * CHUNK], utri)
            for c in range(n_chunks)]

    n_pairs = len(items) // 2
    left = lambda a: a[:, 0:HEAD_DIM]
    right = lambda a: a[:, HEAD_DIM:2 * HEAD_DIM]
    both = lambda a, b: jnp.concatenate([a, b], axis=1)
    stack = lambda a, b: jnp.concatenate([a, b], axis=0)

    qs, ks = [], []
    for c, hd in items:
        r0, r1 = c * CHUNK, (c + 1) * CHUNK
        lo, hi = hd * HEAD_DIM, (hd + 1) * HEAD_DIM
        qs.append(_l2norm(qkv_scr[r0:r1, lo:hi]) * scale)
        ks.append(_l2norm(qkv_scr[r0:r1, kdim + lo:kdim + hi]))
    kkqks = []
    for j in range(n_pairs):
        i1, i2 = 2 * j, 2 * j + 1
        k1, k2 = ks[i1].astype(BF16), ks[i2].astype(BF16)
        lhs = both(stack(k1, qs[i1].astype(BF16)), stack(k2, qs[i2].astype(BF16)))
        kkqks.append(_dot_nt(lhs, _bdiag(k1, k2)))

    vn = _rms(uvz[:, d:2 * d], gmlp_g_ref[...])
    vnb = vn.astype(BF16)
    for g in range(N_HEADS):
        wsg = jnp.where(incl, ws_ref[g], 0.0).astype(BF16)
        lo, hi = g * HEAD_DIM, (g + 1) * HEAD_DIM
        mixed_all = jnp.dot(
            wsg, jnp.concatenate([vnb[c * CHUNK:(c + 1) * CHUNK, lo:hi] for c in range(n_chunks)], axis=1),
            preferred_element_type=F32)
        for c in range(n_chunks):
            r0, r1 = c * CHUNK, (c + 1) * CHUNK
            mixed = mixed_all[:, c * HEAD_DIM:(c + 1) * HEAD_DIM] + bs_ref[g]
            u = uvz[r0:r1, lo:hi]
            z = uvz[r0:r1, 2 * d + lo:2 * d + hi]
            ya_scr[r0:r1, lo:hi] = (u * mixed * _silu(z)).astype(BF16)

    a_mats, attns, rhss = [], [], []
    for i, (c, hd) in enumerate(items):
        r0, r1 = c * CHUNK, (c + 1) * CHUNK
        lo, hi = hd * HEAD_DIM, (hd + 1) * HEAD_DIM
        kkqk = kkqks[i // 2][:, (i % 2) * HEAD_DIM:(i % 2 + 1) * HEAD_DIM]
        gcol = gcs[c][:, hd:hd + 1]
        grow = gcts[c][hd:hd + 1, :]
        egcol = egcs[c][:, hd:hd + 1]
        bcol = beta_all[r0:r1, N_HEADS + hd:N_HEADS + hd + 1]
        dec = jnp.where(incl, jnp.exp(jnp.where(incl, gcol - grow, 0.0)), 0.0)
        a_mats.append(jnp.where(strict, bcol * kkqk[0:CHUNK] * dec, 0.0))
        attns.append(kkqk[CHUNK:2 * CHUNK] * dec)
        vh = qkv_scr[r0:r1, 2 * kdim + lo:2 * kdim + hi]
        rhss.append(jnp.concatenate([vh * bcol, ks[i] * (bcol * egcol)], axis=1))

    pa = jnp.dot(ya_scr[...], w_pa_ref[...], preferred_element_type=F32)

    p_pairs = [both(eye - a_mats[2 * j], eye - a_mats[2 * j + 1]) for j in range(n_pairs)]
    b_pairs = [both(a_mats[2 * j], a_mats[2 * j + 1]).astype(BF16) for j in range(n_pairs)]
    b_pairs = [_dot(b, _bdiag(left(b), right(b))).astype(BF16) for b in b_pairs]
    n_sq = (CHUNK - 1).bit_length() - 1
    for _ in range(n_sq - 1):
        prods = [_dot(stack(p.astype(BF16), b), _bdiag(left(b), right(b)))
                 for p, b in zip(p_pairs, b_pairs)]
        p_pairs = [p + pr[0:CHUNK] for p, pr in zip(p_pairs, prods)]
        b_pairs = [pr[CHUNK:2 * CHUNK].astype(BF16) for pr in prods]
    p_pairs = [p + _dot(p, _bdiag(left(b), right(b))) for p, b in zip(p_pairs, b_pairs)]
    p_mats = [half(p_pairs[j]).astype(BF16) for j in range(n_pairs) for half in (left, right)]
    uws = [_dot(p, r) for p, r in zip(p_mats, rhss)]
    resids = []
    for a, uw, r in zip(a_mats, uws, rhss):
        a_hi, a_lo = _split(a)
        u_hi, u_lo = _split(uw)
        au = _dot(both(a_hi, a_lo), stack(u_hi, u_hi)) + _dot(a_hi, u_lo)
        resids.append(r - uw - au)
    uws = [uw + _dot(p, rs) for uw, p, rs in zip(uws, p_mats, resids)]

    gated_pa = jax.nn.sigmoid(gates[:, 0:d]) * pa
    gate_b = jax.nn.sigmoid(gates[:, d:2 * d])

    for c in range(n_chunks):
        r0, r1 = c * CHUNK, (c + 1) * CHUNK
        idx = [c * N_HEADS + hd for hd in range(N_HEADS)]
        s_olds = [s_scr[hd] for hd in range(N_HEADS)]
        wq_l = lambda hd, i: stack(uws[i][:, HEAD_DIM:], qs[i] * egcs[c][:, hd:hd + 1]).astype(BF16)
        wq_pairs = [_dot(both(wq_l(hd, idx[hd]), wq_l(hd + 1, idx[hd + 1])),
                         _bdiag(s_olds[hd].astype(BF16), s_olds[hd + 1].astype(BF16)))
                    for hd in range(0, N_HEADS, 2)]
        wqs = [half(wq_pairs[hd // 2]) for hd in range(0, N_HEADS, 2) for half in (left, right)]
        v_news = [uws[i][:, 0:HEAD_DIM] - wqs[hd][0:CHUNK] for hd, i in enumerate(idx)]
        av_pairs = [_dot(both(attns[idx[hd]], attns[idx[hd + 1]]),
                         _bdiag(v_news[hd].astype(BF16), v_news[hd + 1].astype(BF16)))
                    for hd in range(0, N_HEADS, 2)]
        avs = [half(av_pairs[hd // 2]) for hd in range(0, N_HEADS, 2) for half in (left, right)]
        os_ = [wqs[hd][CHUNK:2 * CHUNK] + avs[hd] for hd in range(N_HEADS)]
        for hd, i in enumerate(idx):
            glast = gcs[c][CHUNK - 1:CHUNK, hd:hd + 1]
            kd = ks[i] * jnp.exp(glast - gcs[c][:, hd:hd + 1])
            s_scr[hd] = s_olds[hd] * jnp.exp(glast) + _dot_tn(kd, v_news[hd])
        for hd in range(N_HEADS):
            lo, hi = hd * HEAD_DIM, (hd + 1) * HEAD_DIM
            zb = qkvz[r0:r1, 3 * kdim + lo:3 * kdim + hi]
            yb_scr[r0:r1, lo:hi] = (_rms(os_[hd], gdn_g_ref[...]) * _silu(zb)).astype(BF16)

    @pl.when(t == nt - 1)
    def _():
        s_out_ref[0] = s_scr[...]

    pb = jnp.dot(yb_scr[...], w_pb_ref[...], preferred_element_type=F32)

    merged = gated_pa + gate_b * pb
    out = jnp.dot(merged.astype(BF16), w_out_ref[...], preferred_element_type=F32)
    y_ref[0] = x + _rms(out, post_g_ref[...])


def _layer_spec(arr, layer, n_grid):
    shape = (None,) + tuple(arr.shape[1:])
    zeros = (0,) * (arr.ndim - 1)
    if n_grid == 2:
        index_map = lambda b, t: (layer,) + zeros
    else:
        index_map = lambda l, i, j: (l,) + zeros
    return pl.BlockSpec(shape, index_map, pipeline_mode=pl.Buffered(1))


_PROMPT_PARAMS = ("pre_g", "w_uvz", "w_qkvz", "w_gates", "w_ab", "gmlp_g", "w_s", "b_s", "conv_w", "a_log", "dt_bias",
                  "gdn_g", "w_pa", "w_pb", "w_out", "post_g")


def _prompt_layer(x, wts, layer):
    bsz, seq, d = x.shape
    tm = min(TILE_M, seq)
    nt = seq // tm
    kdim = N_HEADS * HEAD_DIM
    conv_dim = 3 * kdim
    params = [wts[name] for name in _PROMPT_PARAMS]
    in_specs = [pl.BlockSpec((1, tm, d), lambda b, t: (b, t, 0))]
    in_specs += [_layer_spec(p, layer, 2) for p in params]
    out_specs = [
        pl.BlockSpec((1, tm, d), lambda b, t: (b, t, 0)),
        pl.BlockSpec((1, N_HEADS, HEAD_DIM, HEAD_DIM), lambda b, t: (b, 0, 0, 0)),
        pl.BlockSpec((1, CONV_W - 1, conv_dim), lambda b, t: (b, 0, 0)),
    ]
    out_shape = [
        jax.ShapeDtypeStruct((bsz, seq, d), F32),
        jax.ShapeDtypeStruct((bsz, N_HEADS, HEAD_DIM, HEAD_DIM), F32),
        jax.ShapeDtypeStruct((bsz, CONV_W - 1, conv_dim), F32),
    ]
    scratch = [
        pltpu.VMEM((N_HEADS, HEAD_DIM, HEAD_DIM), F32),
        pltpu.VMEM((SUBLANES, conv_dim), F32),
        pltpu.VMEM((tm, d), BF16),
        pltpu.VMEM((tm, kdim), BF16),
        pltpu.VMEM((tm, conv_dim), F32),
    ]
    return pl.pallas_call(
        _prompt_kernel,
        grid=(bsz, nt),
        in_specs=in_specs,
        out_specs=out_specs,
        out_shape=out_shape,
        scratch_shapes=scratch,
        compiler_params=pltpu.CompilerParams(
            dimension_semantics=("arbitrary", "arbitrary"),
            vmem_limit_bytes=VMEM_LIMIT),
        name="prompt_layer",
    )(x, *params)


def _sample_kernel(x_ref, s_ref, cs_ref, pre_g_ref, w_uvz_ref, w_qkvz_ref, w_gates_ref, w_ab_ref, gmlp_g_ref, ws00_ref,
                   bs0_ref, convw_ref, alog_ref, dtb_ref, gdn_g_ref, w_pa_ref, w_pb_ref, w_out_ref,
                   post_g_ref,
                   y_ref, s_out_ref, conv_out_ref, vrows_ref,
                   xs_scr, qkvn_scr, zb_scr, sga_pa_scr, sgb_scr, beta_scr, eg_scr, o_scr,
                   q_t, k_t, v_t, beta_t, eg_t, o_t):
    layer = pl.program_id(0)
    i = pl.program_id(1)
    hh = pl.program_id(2)
    n_tiles = pl.num_programs(1)
    n_split = pl.num_programs(2)
    d = x_ref.shape[1]
    kdim = N_HEADS * HEAD_DIM
    heads_per = N_HEADS // HEAD_SPLIT
    wid = heads_per * HEAD_DIM
    rows = SAMPLE_ROWS
    scale = HEAD_DIM ** -0.5

    @pl.when(jnp.logical_and(i == 0, hh == 0))
    def _():
        @pl.when(layer == 0)
        def _():
            xs_scr[...] = x_ref[...]

        x = xs_scr[...]
        h = _rms(x, pre_g_ref[...]).astype(BF16)
        uvz = jnp.dot(h, w_uvz_ref[...], preferred_element_type=F32)
        vn = _rms(uvz[:, d:2 * d], gmlp_g_ref[...])
        vrows_ref[0] = vn
        mixed = ws00_ref[...] * vn + bs0_ref[...]
        ya = uvz[:, 0:d] * mixed * _silu(uvz[:, 2 * d:3 * d])
        pa = jnp.dot(ya.astype(BF16), w_pa_ref[...], preferred_element_type=F32)
        qkvz = jnp.dot(h, w_qkvz_ref[...], preferred_element_type=F32)
        qkvn_scr[...] = qkvz[:, 0:3 * kdim]
        zb_scr[...] = qkvz[:, 3 * kdim:4 * kdim]
        ab = jnp.dot(h, w_ab_ref[...], preferred_element_type=F32)
        eg_scr[...] = jnp.exp(-jnp.exp(alog_ref[...]) * _softplus(ab + dtb_ref[...]))
        beta_scr[...] = jax.nn.sigmoid(ab)
        gates = jnp.dot(h, w_gates_ref[...],
                        preferred_element_type=F32)
        sga_pa_scr[...] = jax.nn.sigmoid(gates[:, 0:d]) * pa
        sgb_scr[...] = jax.nn.sigmoid(gates[:, d:2 * d])

    b0 = pl.multiple_of(i * rows, rows)

    @pl.when(hh == 0)
    def _():
        cs = cs_ref[0]
        qkv_new = qkvn_scr[pl.ds(b0, rows), :]
        conv = (cs[:, 0, :] * convw_ref[0:1, :] + cs[:, 1, :] * convw_ref[1:2, :]
                + cs[:, 2, :] * convw_ref[2:3, :] + qkv_new * convw_ref[3:4, :])
        conv_out_ref[0, :, 0, :] = cs[:, 1, :]
        conv_out_ref[0, :, 1, :] = cs[:, 2, :]
        conv_out_ref[0, :, 2, :] = qkv_new
        qkv = _silu(conv)
        beta_rows = beta_scr[pl.ds(b0, rows), :]
        eg_rows = eg_scr[pl.ds(b0, rows), :]
        for hd in range(N_HEADS):
            lo, hi = hd * HEAD_DIM, (hd + 1) * HEAD_DIM
            half, off = hd // heads_per, (hd % heads_per) * HEAD_DIM
            q_t[half, :, off:off + HEAD_DIM] = _l2norm(qkv[:, lo:hi]) * scale
            k_t[half, :, off:off + HEAD_DIM] = _l2norm(qkv[:, kdim + lo:kdim + hi])
            v_t[half, :, off:off + HEAD_DIM] = qkv[:, 2 * kdim + lo:2 * kdim + hi]
            beta_t[hd] = jnp.broadcast_to(beta_rows[:, N_HEADS + hd:N_HEADS + hd + 1], (rows, HEAD_DIM))
            eg_t[hd] = jnp.broadcast_to(eg_rows[:, hd:hd + 1], (rows, HEAD_DIM))

    rowid = _iota2((SUBLANES, HEAD_DIM), 0)
    states = [(r, j) for r in range(rows) for j in range(heads_per)]
    qv, kv, vv, bv, ev, sv, ksqs = [], [], [], [], [], [], []
    for r, j in states:
        off = j * HEAD_DIM
        q = q_t[hh, r:r + 1, off:off + HEAD_DIM]
        k = k_t[hh, r:r + 1, off:off + HEAD_DIM]
        qv.append(q)
        kv.append(k)
        vv.append(v_t[hh, r:r + 1, off:off + HEAD_DIM])
        bv.append(beta_t[hh * heads_per + j, r:r + 1, :])
        ev.append(eg_t[hh * heads_per + j, r:r + 1, :])
        s = s_ref[0, r, j]
        sv.append(s)
        kq8 = jnp.where(rowid == 0, jnp.broadcast_to(k, (SUBLANES, HEAD_DIM)),
                        jnp.where(rowid == 1, jnp.broadcast_to(q, (SUBLANES, HEAD_DIM)), 0.0))
        ksqs.append(_dot(kq8, s))
    for n, (r, j) in enumerate(states):
        v_new = bv[n] * (vv[n] - ev[n] * ksqs[n][0:1, :])
        qk = jnp.sum(qv[n] * kv[n], axis=-1, keepdims=True)
        o_t[r:r + 1, j * HEAD_DIM:(j + 1) * HEAD_DIM] = ev[n] * ksqs[n][1:2, :] + qk * v_new
        k_hi, k_lo = _split(kv[n])
        v_hi, v_lo = _split(v_new)
        bc = lambda a: jnp.broadcast_to(a.astype(F32), (SUBLANES, HEAD_DIM))
        k8 = jnp.where(rowid <= 1, bc(k_hi), jnp.where(rowid == 2, bc(k_lo), 0.0))
        v8 = jnp.where(rowid == 0, bc(v_hi), jnp.where(rowid == 1, bc(v_lo),
                                                       jnp.where(rowid == 2, bc(v_hi), 0.0)))
        s_out_ref[0, r, j] = sv[n] * ev[n] + _dot_tn(k8, v8)
    o_scr[hh, pl.ds(b0, rows), :] = o_t[...]

    @pl.when(jnp.logical_and(i == n_tiles - 1, hh == n_split - 1))
    def _():
        pb = jnp.zeros((x_ref.shape[0], d), F32)
        for hd in range(N_HEADS):
            lo, hi = hd * HEAD_DIM, (hd + 1) * HEAD_DIM
            half, off = hd // heads_per, (hd % heads_per) * HEAD_DIM
            o = o_scr[half, :, off:off + HEAD_DIM]
            yb = (_rms(o, gdn_g_ref[...]) * _silu(zb_scr[:, lo:hi])).astype(BF16)
            pb = pb + jnp.dot(yb, w_pb_ref[lo:hi, :], preferred_element_type=F32)
        merged = sga_pa_scr[...] + sgb_scr[...] * pb
        out = jnp.dot(merged.astype(BF16), w_out_ref[...], preferred_element_type=F32)
        y = xs_scr[...] + _rms(out, post_g_ref[...])
        xs_scr[...] = y
        y_ref[...] = y


_SAMPLE_PARAMS = ("pre_g", "w_uvz", "w_qkvz", "w_gates", "w_ab", "gmlp_g", "ws00", "bs0", "conv_w", "a_log", "dt_bias",
                  "gdn_g", "w_pa", "w_pb", "w_out", "post_g")


def _sample_path(x, state_gdn, state_conv, wts):
    n, d = x.shape
    depth = state_gdn.shape[0]
    kdim = N_HEADS * HEAD_DIM
    conv_dim = 3 * kdim
    heads_per = N_HEADS // HEAD_SPLIT
    rows = SAMPLE_ROWS
    params = [wts[name] for name in _SAMPLE_PARAMS]
    s_spec = pl.BlockSpec((1, rows, heads_per, HEAD_DIM, HEAD_DIM), lambda l, i, j: (l, i, j, 0, 0))
    c_spec = pl.BlockSpec((1, rows, CONV_W - 1, conv_dim), lambda l, i, j: (l, i, 0, 0))
    in_specs = [pl.BlockSpec((n, d), lambda l, i, j: (0, 0)), s_spec, c_spec]
    in_specs += [_layer_spec(p, None, 3) for p in params]
    out_specs = [
        pl.BlockSpec((n, d), lambda l, i, j: (0, 0)),
        s_spec,
        c_spec,
        pl.BlockSpec((1, n, d), lambda l, i, j: (l, 0, 0)),
    ]
    out_shape = [
        jax.ShapeDtypeStruct((n, d), F32),
        jax.ShapeDtypeStruct(state_gdn.shape, F32),
        jax.ShapeDtypeStruct(state_conv.shape, F32),
        jax.ShapeDtypeStruct((depth, n, d), F32),
    ]
    scratch = [
        pltpu.VMEM((n, d), F32),
        pltpu.VMEM((n, conv_dim), F32),
        pltpu.VMEM((n, kdim), F32),
        pltpu.VMEM((n, d), F32),
        pltpu.VMEM((n, d), F32),
        pltpu.VMEM((n, LANES), F32),
        pltpu.VMEM((n, LANES), F32),
        pltpu.VMEM((HEAD_SPLIT, n, heads_per * HEAD_DIM), F32),
        pltpu.VMEM((HEAD_SPLIT, rows, heads_per * HEAD_DIM), F32),
        pltpu.VMEM((HEAD_SPLIT, rows, heads_per * HEAD_DIM), F32),
        pltpu.VMEM((HEAD_SPLIT, rows, heads_per * HEAD_DIM), F32),
        pltpu.VMEM((N_HEADS, rows, HEAD_DIM), F32),
        pltpu.VMEM((N_HEADS, rows, HEAD_DIM), F32),
        pltpu.VMEM((rows, heads_per * HEAD_DIM), F32),
    ]
    return pl.pallas_call(
        _sample_kernel,
        grid=(depth, n // rows, HEAD_SPLIT),
        in_specs=in_specs,
        out_specs=out_specs,
        out_shape=out_shape,
        scratch_shapes=scratch,
        compiler_params=pltpu.CompilerParams(
            dimension_semantics=("arbitrary", "arbitrary", "arbitrary"),
            vmem_limit_bytes=VMEM_LIMIT),
        name="sample_path",
    )(x, state_gdn, state_conv, *params)


def _to_kernel_order(x):
    b, l, d = x.shape
    return x.reshape(b, l // CHUNK, SUBLANES, VROWS, d).swapaxes(2, 3).reshape(b, l, d)


def _from_kernel_order(x):
    b, l, d = x.shape
    return x.reshape(b, l // CHUNK, VROWS, SUBLANES, d).swapaxes(2, 3).reshape(b, l, d)


def _prepare_weights(pre_norm, w_in, gmlp_norm, w_spatial, b_spatial, conv_w, a_log, dt_bias,
                     gdn_norm, w_proj_a, w_proj_b, w_out, post_norm):
    depth, d, _ = w_in.shape
    kdim = N_HEADS * HEAD_DIM
    ab0 = 3 * d + 4 * kdim
    pad_lanes = lambda a: jnp.pad(a, ((0, 0), (0, LANES - N_HEADS)))[:, None, :]
    rows = jnp.arange(CHUNK)
    tok = (rows % SUBLANES) * VROWS + rows // SUBLANES
    return {
        "pre_g": pre_norm[:, None, :],
        "w_uvz": w_in[:, :, 0:3 * d].astype(BF16),
        "w_qkvz": w_in[:, :, 3 * d:ab0].astype(BF16),
        "w_gates": w_in[:, :, ab0 + 2 * N_HEADS:].astype(BF16),
        "w_ab": jnp.pad(w_in[:, :, ab0:ab0 + 2 * N_HEADS],
                        ((0, 0), (0, 0), (0, LANES - 2 * N_HEADS))).astype(BF16),
        "gmlp_g": gmlp_norm[:, None, :],
        "w_s": w_spatial[:, :, tok, :][:, :, :, tok],
        "b_s": jnp.broadcast_to(b_spatial[:, :, tok, None], (depth, N_HEADS, CHUNK, HEAD_DIM)),
        "ws00": jnp.repeat(w_spatial[:, :, 0, 0], HEAD_DIM, axis=1)[:, None, :],
        "bs0": jnp.repeat(b_spatial[:, :, 0], HEAD_DIM, axis=1)[:, None, :],
        "conv_w": conv_w,
        "a_log": pad_lanes(a_log),
        "dt_bias": pad_lanes(dt_bias),
        "gdn_g": gdn_norm[:, None, :],
        "w_pa": w_proj_a.astype(BF16),
        "w_pb": w_proj_b.astype(BF16),
        "w_out": w_out.astype(BF16),
        "post_g": post_norm[:, None, :],
    }


def kernel(x_prompt, x_sample, state_gdn, state_conv, pre_norm, w_in, gmlp_norm, w_spatial, b_spatial, conv_w, a_log, dt_bias, gdn_norm, w_proj_a, w_proj_b, w_out, post_norm):
    depth = w_in.shape[0]
    wts = _prepare_weights(pre_norm, w_in, gmlp_norm, w_spatial, b_spatial, conv_w, a_log, dt_bias,
                           gdn_norm, w_proj_a, w_proj_b, w_out, post_norm)
    xp = _to_kernel_order(x_prompt)
    gdn_p, conv_p = [], []
    for l in range(depth):
        xp, sg_p, cb_p = _prompt_layer(xp, wts, l)
        gdn_p.append(sg_p)
        conv_p.append(cb_p)
    xp = _from_kernel_order(xp)
    ys, gdn_s, conv_s, vrows_s = _sample_path(x_sample[:, 0, :], state_gdn, state_conv, wts)
    return (xp, ys[:, None, :], jnp.stack(gdn_p), jnp.stack(conv_p), gdn_s, conv_s,
            vrows_s[:, :, None, :])
```

```python
import jax
import jax.numpy as jnp
from jax import lax
from jax.experimental import pallas as pl
from jax.experimental.pallas import tpu as pltpu

F32 = jnp.float32
BF16 = jnp.bfloat16
EPS = 1e-6

LANES = 128
SUBLANES = 8
HEAD_DIM = 128
N_HEADS = 8
CONV_W = 4
CHUNK = 128
TILE_M = 256
VROWS = CHUNK // SUBLANES
SAMPLE_ROWS = 8
HEAD_SPLIT = 2
VMEM_LIMIT = 56 * 1024 * 1024


def _dot(a, b):
    return jnp.dot(a.astype(BF16), b.astype(BF16), preferred_element_type=F32)


def _dot_nt(a, b):
    return lax.dot_general(a.astype(BF16), b.astype(BF16), (((1,), (1,)), ((), ())),
                           preferred_element_type=F32)


def _dot_tn(a, b):
    return lax.dot_general(a.astype(BF16), b.astype(BF16), (((0,), (0,)), ((), ())),
                           preferred_element_type=F32)


def _bdiag(a, b):
    z = jnp.zeros_like(a)
    return jnp.concatenate([jnp.concatenate([a, z], axis=1), jnp.concatenate([z, b], axis=1)], axis=0)


def _w(w_ref):
    return pltpu.bitcast(w_ref[...], BF16)


def _split(a):
    hi = a.astype(BF16)
    lo = (a - hi.astype(F32)).astype(BF16)
    return hi, lo


def _dot_exact(a, b):
    return jnp.dot(a, b, preferred_element_type=F32, precision=lax.Precision.HIGHEST)


def _rms(x, g):
    return x * lax.rsqrt(jnp.mean(x * x, axis=-1, keepdims=True) + EPS) * g


def _silu(x):
    return x * jax.nn.sigmoid(x)


def _softplus(x):
    return jnp.maximum(x, 0.0) + jnp.log(1.0 + jnp.exp(-jnp.abs(x)))


def _l2norm(x):
    return x * lax.rsqrt(jnp.sum(x * x, axis=-1, keepdims=True) + EPS)


def _iota2(shape, dim):
    return lax.broadcasted_iota(jnp.int32, shape, dim)


def _prompt_kernel(x_ref, pre_g_ref, w_uvz_ref, w_qkvz_ref, w_gates_ref, w_ab_ref, gmlp_g_ref, ws_ref, bs_ref,
                   convw_ref, alog_ref, dtb_ref, gdn_g_ref, w_pa_ref, w_pb_ref, w_out_ref,
                   post_g_ref,
                   y_ref, s_out_ref, conv_out_ref,
                   s_scr, carry_scr, ya_scr, yb_scr, qkv_scr):
    t = pl.program_id(1)
    nt = pl.num_programs(1)
    tm = x_ref.shape[1]
    d = x_ref.shape[2]
    n_chunks = tm // CHUNK
    kdim = N_HEADS * HEAD_DIM

    @pl.when(t == 0)
    def _():
        s_scr[...] = jnp.zeros_like(s_scr)
        carry_scr[...] = jnp.zeros_like(carry_scr)

    x = x_ref[0]
    h = _rms(x, pre_g_ref[...]).astype(BF16)

    row = _iota2((CHUNK, CHUNK), 0)
    col = _iota2((CHUNK, CHUNK), 1)
    token_of = lambda r: (r % SUBLANES) * VROWS + r // SUBLANES
    row_tok, col_tok = token_of(row), token_of(col)
    incl = row_tok >= col_tok
    strict = row_tok > col_tok

    qkvz = jnp.dot(h, _w(w_qkvz_ref), preferred_element_type=F32)
    uvz = jnp.dot(h, _w(w_uvz_ref), preferred_element_type=F32)
    rowid = _iota2((SUBLANES, 3 * kdim), 0)
    n_tail = CONV_W - 1
    prev = [carry_scr[j:j + 1, :] for j in range(n_tail)]
    for c in range(n_chunks):
        r0 = c * CHUNK
        pre = qkvz[r0:r0 + CHUNK, 0:3 * kdim]
        wrapped = []
        for j in range(n_tail):
            blk = pre[(VROWS - n_tail + j) * SUBLANES:(VROWS - n_tail + j + 1) * SUBLANES, :]
            shifted = pltpu.roll(blk, 1, axis=0)
            wrapped.append(jnp.where(rowid == 0, jnp.broadcast_to(prev[j], blk.shape), shifted))
            prev[j] = blk[SUBLANES - 1:SUBLANES, :]
        conv = pre * convw_ref[CONV_W - 1:CONV_W, :]
        for k in range(1, CONV_W):
            back_k = jnp.concatenate(wrapped[n_tail - k:] + [pre[0:(VROWS - k) * SUBLANES, :]], axis=0)
            conv = conv + back_k * convw_ref[CONV_W - 1 - k:CONV_W - k, :]
        qkv_scr[r0:r0 + CHUNK, :] = _silu(conv)
    tail = jnp.concatenate(prev, axis=0)
    carry_scr[0:n_tail, :] = tail

    @pl.when(t == nt - 1)
    def _():
        conv_out_ref[0] = tail

    gates = jnp.dot(h, _w(w_gates_ref), preferred_element_type=F32)
    ab = jnp.dot(h, w_ab_ref[...], preferred_element_type=F32)
    g_all = -jnp.exp(alog_ref[...]) * _softplus(ab + dtb_ref[...])
    beta_all = jax.nn.sigmoid(ab)
    g_all_t = g_all.T
    ltri = incl.astype(F32)
    utri = (row_tok <= col_tok).astype(F32)
    eye = (row == col).astype(F32)
    scale = HEAD_DIM ** -0.5

    items = [(c, hd) for c in range(n_chunks) for hd in range(N_HEADS)]
    gcs, egcs = [], []
    for c in range(n_chunks):
        r0, r1 = c * CHUNK, (c + 1) * CHUNK
        gcs.append(_dot_exact(ltri, g_all[r0:r1, :]))
        egcs.append(jnp.exp(gcs[c]))
    gcts = [_dot_exact(g_all_t[0:N_HEADS, c * CHUNK:(c + 1) * CHUNK], utri)
            for c in range(n_chunks)]

    n_pairs = len(items) // 2
    left = lambda a: a[:, 0:HEAD_DIM]
    right = lambda a: a[:, HEAD_DIM:2 * HEAD_DIM]
    both = lambda a, b: jnp.concatenate([a, b], axis=1)
    stack = lambda a, b: jnp.concatenate([a, b], axis=0)

    qs, ks = [], []
    for c, hd in items:
        r0, r1 = c * CHUNK, (c + 1) * CHUNK
        lo, hi = hd * HEAD_DIM, (hd + 1) * HEAD_DIM
        qs.append(_l2norm(qkv_scr[r0:r1, lo:hi]) * scale)
        ks.append(_l2norm(qkv_scr[r0:r1, kdim + lo:kdim + hi]))
    kkqks = []
    for j in range(n_pairs):
        i1, i2 = 2 * j, 2 * j + 1
        k1, k2 = ks[i1].astype(BF16), ks[i2].astype(BF16)
        lhs = both(stack(k1, qs[i1].astype(BF16)), stack(k2, qs[i2].astype(BF16)))
        kkqks.append(_dot_nt(lhs, _bdiag(k1, k2)))

    vn = _rms(uvz[:, d:2 * d], gmlp_g_ref[...])
    vnb = vn.astype(BF16)
    for g in range(N_HEADS):
        wsg = jnp.where(incl, ws_ref[g], 0.0).astype(BF16)
        lo, hi = g * HEAD_DIM, (g + 1) * HEAD_DIM
        mixed_all = jnp.dot(
            wsg, jnp.concatenate([vnb[c * CHUNK:(c + 1) * CHUNK, lo:hi] for c in range(n_chunks)], axis=1),
            preferred_element_type=F32)
        for c in range(n_chunks):
            r0, r1 = c * CHUNK, (c + 1) * CHUNK
            mixed = mixed_all[:, c * HEAD_DIM:(c + 1) * HEAD_DIM] + bs_ref[g]
            u = uvz[r0:r1, lo:hi]
            z = uvz[r0:r1, 2 * d + lo:2 * d + hi]
            ya_scr[r0:r1, lo:hi] = (u * mixed * _silu(z)).astype(BF16)

    a_mats, attns, rhss = [], [], []
    for i, (c, hd) in enumerate(items):
        r0, r1 = c * CHUNK, (c + 1) * CHUNK
        lo, hi = hd * HEAD_DIM, (hd + 1) * HEAD_DIM
        kkqk = kkqks[i // 2][:, (i % 2) * HEAD_DIM:(i % 2 + 1) * HEAD_DIM]
        gcol = gcs[c][:, hd:hd + 1]
        grow = gcts[c][hd:hd + 1, :]
        egcol = egcs[c][:, hd:hd + 1]
        bcol = beta_all[r0:r1, N_HEADS + hd:N_HEADS + hd + 1]
        dec = jnp.where(incl, jnp.exp(jnp.where(incl, gcol - grow, 0.0)), 0.0)
        a_mats.append(jnp.where(strict, bcol * kkqk[0:CHUNK] * dec, 0.0))
        attns.append(kkqk[CHUNK:2 * CHUNK] * dec)
        vh = qkv_scr[r0:r1, 2 * kdim + lo:2 * kdim + hi]
        rhss.append(jnp.concatenate([vh * bcol, ks[i] * (bcol * egcol)], axis=1))

    pa = jnp.dot(ya_scr[...], _w(w_pa_ref), preferred_element_type=F32)

    p_pairs = [both(eye - a_mats[2 * j], eye - a_mats[2 * j + 1]) for j in range(n_pairs)]
    b_pairs = [both(a_mats[2 * j], a_mats[2 * j + 1]).astype(BF16) for j in range(n_pairs)]
    b_pairs = [_dot(b, _bdiag(left(b), right(b))).astype(BF16) for b in b_pairs]
    n_sq = (CHUNK - 1).bit_length() - 1
    for _ in range(n_sq - 1):
        prods = [_dot(stack(p.astype(BF16), b), _bdiag(left(b), right(b)))
                 for p, b in zip(p_pairs, b_pairs)]
        p_pairs = [p + pr[0:CHUNK] for p, pr in zip(p_pairs, prods)]
        b_pairs = [pr[CHUNK:2 * CHUNK].astype(BF16) for pr in prods]
    p_pairs = [p + _dot(p, _bdiag(left(b), right(b))) for p, b in zip(p_pairs, b_pairs)]
    p_mats = [half(p_pairs[j]).astype(BF16) for j in range(n_pairs) for half in (left, right)]
    uws = [_dot(p, r) for p, r in zip(p_mats, rhss)]
    resids = []
    for a, uw, r in zip(a_mats, uws, rhss):
        a_hi, a_lo = _split(a)
        u_hi, u_lo = _split(uw)
        au = _dot(both(a_hi, a_lo), stack(u_hi, u_hi)) + _dot(a_hi, u_lo)
        resids.append(r - uw - au)
    uws = [uw + _dot(p, rs) for uw, p, rs in zip(uws, p_mats, resids)]

    gated_pa = jax.nn.sigmoid(gates[:, 0:d]) * pa
    gate_b = jax.nn.sigmoid(gates[:, d:2 * d])

    for c in range(n_chunks):
        r0, r1 = c * CHUNK, (c + 1) * CHUNK
        idx = [c * N_HEADS + hd for hd in range(N_HEADS)]
        s_olds = [s_scr[hd] for hd in range(N_HEADS)]
        wq_l = lambda hd, i: stack(uws[i][:, HEAD_DIM:], qs[i] * egcs[c][:, hd:hd + 1]).astype(BF16)
        wq_pairs = [_dot(both(wq_l(hd, idx[hd]), wq_l(hd + 1, idx[hd + 1])),
                         _bdiag(s_olds[hd].astype(BF16), s_olds[hd + 1].astype(BF16)))
                    for hd in range(0, N_HEADS, 2)]
        wqs = [half(wq_pairs[hd // 2]) for hd in range(0, N_HEADS, 2) for half in (left, right)]
        v_news = [uws[i][:, 0:HEAD_DIM] - wqs[hd][0:CHUNK] for hd, i in enumerate(idx)]
        av_pairs = [_dot(both(attns[idx[hd]], attns[idx[hd + 1]]),
                         _bdiag(v_news[hd].astype(BF16), v_news[hd + 1].astype(BF16)))
                    for hd in range(0, N_HEADS, 2)]
        avs = [half(av_pairs[hd // 2]) for hd in range(0, N_HEADS, 2) for half in (left, right)]
        os_ = [wqs[hd][CHUNK:2 * CHUNK] + avs[hd] for hd in range(N_HEADS)]
        for hd, i in enumerate(idx):
            glast = gcs[c][CHUNK - 1:CHUNK, hd:hd + 1]
            kd = ks[i] * jnp.exp(glast - gcs[c][:, hd:hd + 1])
            s_scr[hd] = s_olds[hd] * jnp.exp(glast) + _dot_tn(kd, v_news[hd])
        for hd in range(N_HEADS):
            lo, hi = hd * HEAD_DIM, (hd + 1) * HEAD_DIM
            zb = qkvz[r0:r1, 3 * kdim + lo:3 * kdim + hi]
            yb_scr[r0:r1, lo:hi] = (_rms(os_[hd], gdn_g_ref[...]) * _silu(zb)).astype(BF16)

    @pl.when(t == nt - 1)
    def _():
        s_out_ref[0] = s_scr[...]

    pb = jnp.dot(yb_scr[...], _w(w_pb_ref), preferred_element_type=F32)

    merged = gated_pa + gate_b * pb
    out = jnp.dot(merged.astype(BF16), _w(w_out_ref), preferred_element_type=F32)
    y_ref[0] = x + _rms(out, post_g_ref[...])


def _layer_spec(arr, layer, n_grid):
    shape = (None,) + tuple(arr.shape[1:])
    zeros = (0,) * (arr.ndim - 1)
    if n_grid == 2:
        index_map = lambda b, t: (layer,) + zeros
    else:
        index_map = lambda l, i, j: (l,) + zeros
    return pl.BlockSpec(shape, index_map, pipeline_mode=pl.Buffered(1))


_PROMPT_PARAMS = ("pre_g", "w_uvz", "w_qkvz", "w_gates", "w_ab", "gmlp_g", "w_s", "b_s", "conv_w", "a_log", "dt_bias",
                  "gdn_g", "w_pa", "w_pb", "w_out", "post_g")


def _prompt_layer(x, wts, layer):
    bsz, seq, d = x.shape
    tm = min(TILE_M, seq)
    nt = seq // tm
    kdim = N_HEADS * HEAD_DIM
    conv_dim = 3 * kdim
    params = [wts[name] for name in _PROMPT_PARAMS]
    in_specs = [pl.BlockSpec((1, tm, d), lambda b, t: (b, t, 0))]
    in_specs += [_layer_spec(p, layer, 2) for p in params]
    out_specs = [
        pl.BlockSpec((1, tm, d), lambda b, t: (b, t, 0)),
        pl.BlockSpec((1, N_HEADS, HEAD_DIM, HEAD_DIM), lambda b, t: (b, 0, 0, 0)),
        pl.BlockSpec((1, CONV_W - 1, conv_dim), lambda b, t: (b, 0, 0)),
    ]
    out_shape = [
        jax.ShapeDtypeStruct((bsz, seq, d), F32),
        jax.ShapeDtypeStruct((bsz, N_HEADS, HEAD_DIM, HEAD_DIM), F32),
        jax.ShapeDtypeStruct((bsz, CONV_W - 1, conv_dim), F32),
    ]
    scratch = [
        pltpu.VMEM((N_HEADS, HEAD_DIM, HEAD_DIM), F32),
        pltpu.VMEM((SUBLANES, conv_dim), F32),
        pltpu.VMEM((tm, d), BF16),
        pltpu.VMEM((tm, kdim), BF16),
        pltpu.VMEM((tm, conv_dim), F32),
    ]
    return pl.pallas_call(
        _prompt_kernel,
        grid=(bsz, nt),
        in_specs=in_specs,
        out_specs=out_specs,
        out_shape=out_shape,
        scratch_shapes=scratch,
        compiler_params=pltpu.CompilerParams(
            dimension_semantics=("arbitrary", "arbitrary"),
            vmem_limit_bytes=VMEM_LIMIT),
        name="prompt_layer",
    )(x, *params)


def _sample_kernel(x_ref, s_ref, cs_ref, pre_g_ref, w_uvz_ref, w_qkvz_ref, w_gates_ref, w_ab_ref, gmlp_g_ref, ws00_ref,
                   bs0_ref, convw_ref, alog_ref, dtb_ref, gdn_g_ref, w_pa_ref, w_pb_ref, w_out_ref,
                   post_g_ref,
                   y_ref, s_out_ref, conv_out_ref, vrows_ref,
                   xs_scr, qkvn_scr, zb_scr, sga_pa_scr, sgb_scr, beta_scr, eg_scr, o_scr,
                   q_t, k_t, v_t, beta_t, eg_t, o_t):
    layer = pl.program_id(0)
    i = pl.program_id(1)
    hh = pl.program_id(2)
    n_tiles = pl.num_programs(1)
    n_split = pl.num_programs(2)
    d = x_ref.shape[1]
    kdim = N_HEADS * HEAD_DIM
    heads_per = N_HEADS // HEAD_SPLIT
    wid = heads_per * HEAD_DIM
    rows = SAMPLE_ROWS
    scale = HEAD_DIM ** -0.5

    @pl.when(jnp.logical_and(i == 0, hh == 0))
    def _():
        @pl.when(layer == 0)
        def _():
            xs_scr[...] = x_ref[...]

        x = xs_scr[...]
        h = _rms(x, pre_g_ref[...]).astype(BF16)
        uvz = jnp.dot(h, _w(w_uvz_ref), preferred_element_type=F32)
        vn = _rms(uvz[:, d:2 * d], gmlp_g_ref[...])
        vrows_ref[0] = vn
        mixed = ws00_ref[...] * vn + bs0_ref[...]
        ya = uvz[:, 0:d] * mixed * _silu(uvz[:, 2 * d:3 * d])
        pa = jnp.dot(ya.astype(BF16), _w(w_pa_ref), preferred_element_type=F32)
        qkvz = jnp.dot(h, _w(w_qkvz_ref), preferred_element_type=F32)
        qkvn_scr[...] = qkvz[:, 0:3 * kdim]
        zb_scr[...] = qkvz[:, 3 * kdim:4 * kdim]
        ab = jnp.dot(h, w_ab_ref[...], preferred_element_type=F32)
        eg_scr[...] = jnp.exp(-jnp.exp(alog_ref[...]) * _softplus(ab + dtb_ref[...]))
        beta_scr[...] = jax.nn.sigmoid(ab)
        gates = jnp.dot(h, _w(w_gates_ref),
                        preferred_element_type=F32)
        sga_pa_scr[...] = jax.nn.sigmoid(gates[:, 0:d]) * pa
        sgb_scr[...] = jax.nn.sigmoid(gates[:, d:2 * d])

    b0 = pl.multiple_of(i * rows, rows)

    @pl.when(hh == 0)
    def _():
        cs = cs_ref[0]
        qkv_new = qkvn_scr[pl.ds(b0, rows), :]
        conv = (cs[:, 0, :] * convw_ref[0:1, :] + cs[:, 1, :] * convw_ref[1:2, :]
                + cs[:, 2, :] * convw_ref[2:3, :] + qkv_new * convw_ref[3:4, :])
        conv_out_ref[0, :, 0, :] = cs[:, 1, :]
        conv_out_ref[0, :, 1, :] = cs[:, 2, :]
        conv_out_ref[0, :, 2, :] = qkv_new
        qkv = _silu(conv)
        beta_rows = beta_scr[pl.ds(b0, rows), :]
        eg_rows = eg_scr[pl.ds(b0, rows), :]
        for hd in range(N_HEADS):
            lo, hi = hd * HEAD_DIM, (hd + 1) * HEAD_DIM
            half, off = hd // heads_per, (hd % heads_per) * HEAD_DIM
            q_t[half, :, off:off + HEAD_DIM] = _l2norm(qkv[:, lo:hi]) * scale
            k_t[half, :, off:off + HEAD_DIM] = _l2norm(qkv[:, kdim + lo:kdim + hi])
            v_t[half, :, off:off + HEAD_DIM] = qkv[:, 2 * kdim + lo:2 * kdim + hi]
            beta_t[hd] = jnp.broadcast_to(beta_rows[:, N_HEADS + hd:N_HEADS + hd + 1], (rows, HEAD_DIM))
            eg_t[hd] = jnp.broadcast_to(eg_rows[:, hd:hd + 1], (rows, HEAD_DIM))

    rowid = _iota2((SUBLANES, HEAD_DIM), 0)
    states = [(r, j) for r in range(rows) for j in range(heads_per)]
    qv, kv, vv, bv, ev, sv, ksqs = [], [], [], [], [], [], []
    for r, j in states:
        off = j * HEAD_DIM
        q = q_t[hh, r:r + 1, off:off + HEAD_DIM]
        k = k_t[hh, r:r + 1, off:off + HEAD_DIM]
        qv.append(q)
        kv.append(k)
        vv.append(v_t[hh, r:r + 1, off:off + HEAD_DIM])
        bv.append(beta_t[hh * heads_per + j, r:r + 1, :])
        ev.append(eg_t[hh * heads_per + j, r:r + 1, :])
        s = s_ref[0, r, j]
        sv.append(s)
        kq8 = jnp.where(rowid == 0, jnp.broadcast_to(k, (SUBLANES, HEAD_DIM)),
                        jnp.where(rowid == 1, jnp.broadcast_to(q, (SUBLANES, HEAD_DIM)), 0.0))
        ksqs.append(_dot(kq8, s))
    for n, (r, j) in enumerate(states):
        v_new = bv[n] * (vv[n] - ev[n] * ksqs[n][0:1, :])
        qk = jnp.sum(qv[n] * kv[n], axis=-1, keepdims=True)
        o_t[r:r + 1, j * HEAD_DIM:(j + 1) * HEAD_DIM] = ev[n] * ksqs[n][1:2, :] + qk * v_new
        k_hi, k_lo = _split(kv[n])
        v_hi, v_lo = _split(v_new)
        bc = lambda a: jnp.broadcast_to(a.astype(F32), (SUBLANES, HEAD_DIM))
        k8 = jnp.where(rowid <= 1, bc(k_hi), jnp.where(rowid == 2, bc(k_lo), 0.0))
        v8 = jnp.where(rowid == 0, bc(v_hi), jnp.where(rowid == 1, bc(v_lo),
                                                       jnp.where(rowid == 2, bc(v_hi), 0.0)))
        s_out_ref[0, r, j] = sv[n] * ev[n] + _dot_tn(k8, v8)
    o_scr[hh, pl.ds(b0, rows), :] = o_t[...]

    @pl.when(jnp.logical_and(i == n_tiles - 1, hh == n_split - 1))
    def _():
        pb = jnp.zeros((x_ref.shape[0], d), F32)
        for hd in range(N_HEADS):
            lo, hi = hd * HEAD_DIM, (hd + 1) * HEAD_DIM
            half, off = hd // heads_per, (hd % heads_per) * HEAD_DIM
            o = o_scr[half, :, off:off + HEAD_DIM]
            yb = (_rms(o, gdn_g_ref[...]) * _silu(zb_scr[:, lo:hi])).astype(BF16)
            pb = pb + jnp.dot(yb, pltpu.bitcast(w_pb_ref[lo // 2:hi // 2, :], BF16),
                              preferred_element_type=F32)
        merged = sga_pa_scr[...] + sgb_scr[...] * pb
        out = jnp.dot(merged.astype(BF16), _w(w_out_ref), preferred_element_type=F32)
        y = xs_scr[...] + _rms(out, post_g_ref[...])
        xs_scr[...] = y
        y_ref[...] = y


_SAMPLE_PARAMS = ("pre_g", "w_uvz", "w_qkvz", "w_gates", "w_ab", "gmlp_g", "ws00", "bs0", "conv_w", "a_log", "dt_bias",
                  "gdn_g", "w_pa", "w_pb", "w_out", "post_g")


def _sample_path(x, state_gdn, state_conv, wts):
    n, d = x.shape
    depth = state_gdn.shape[0]
    kdim = N_HEADS * HEAD_DIM
    conv_dim = 3 * kdim
    heads_per = N_HEADS // HEAD_SPLIT
    rows = SAMPLE_ROWS
    params = [wts[name] for name in _SAMPLE_PARAMS]
    s_spec = pl.BlockSpec((1, rows, heads_per, HEAD_DIM, HEAD_DIM), lambda l, i, j: (l, i, j, 0, 0))
    c_spec = pl.BlockSpec((1, rows, CONV_W - 1, conv_dim), lambda l, i, j: (l, i, 0, 0))
    in_specs = [pl.BlockSpec((n, d), lambda l, i, j: (0, 0)), s_spec, c_spec]
    in_specs += [_layer_spec(p, None, 3) for p in params]
    out_specs = [
        pl.BlockSpec((n, d), lambda l, i, j: (0, 0)),
        s_spec,
        c_spec,
        pl.BlockSpec((1, n, d), lambda l, i, j: (l, 0, 0)),
    ]
    out_shape = [
        jax.ShapeDtypeStruct((n, d), F32),
        jax.ShapeDtypeStruct(state_gdn.shape, F32),
        jax.ShapeDtypeStruct(state_conv.shape, F32),
        jax.ShapeDtypeStruct((depth, n, d), F32),
    ]
    scratch = [
        pltpu.VMEM((n, d), F32),
        pltpu.VMEM((n, conv_dim), F32),
        pltpu.VMEM((n, kdim), F32),
        pltpu.VMEM((n, d), F32),
        pltpu.VMEM((n, d), F32),
        pltpu.VMEM((n, LANES), F32),
        pltpu.VMEM((n, LANES), F32),
        pltpu.VMEM((HEAD_SPLIT, n, heads_per * HEAD_DIM), F32),
        pltpu.VMEM((HEAD_SPLIT, rows, heads_per * HEAD_DIM), F32),
        pltpu.VMEM((HEAD_SPLIT, rows, heads_per * HEAD_DIM), F32),
        pltpu.VMEM((HEAD_SPLIT, rows, heads_per * HEAD_DIM), F32),
        pltpu.VMEM((N_HEADS, rows, HEAD_DIM), F32),
        pltpu.VMEM((N_HEADS, rows, HEAD_DIM), F32),
        pltpu.VMEM((rows, heads_per * HEAD_DIM), F32),
    ]
    return pl.pallas_call(
        _sample_kernel,
        grid=(depth, n // rows, HEAD_SPLIT),
        in_specs=in_specs,
        out_specs=out_specs,
        out_shape=out_shape,
        scratch_shapes=scratch,
        compiler_params=pltpu.CompilerParams(
            dimension_semantics=("arbitrary", "arbitrary", "arbitrary"),
            vmem_limit_bytes=VMEM_LIMIT),
        name="sample_path",
    )(x, state_gdn, state_conv, *params)


def _to_kernel_order(x):
    b, l, d = x.shape
    return x.reshape(b, l // CHUNK, SUBLANES, VROWS, d).swapaxes(2, 3).reshape(b, l, d)


def _from_kernel_order(x):
    b, l, d = x.shape
    return x.reshape(b, l // CHUNK, VROWS, SUBLANES, d).swapaxes(2, 3).reshape(b, l, d)


def _pack(x):
    return pltpu.bitcast(x.astype(BF16), jnp.uint32)


def _cast_w_in_kernel(w_ref, uvz_ref, qkvz_ref, gates_ref, ab_ref):
    d = uvz_ref.shape[2] // 3
    ab0 = uvz_ref.shape[2] + qkvz_ref.shape[2]
    n_ab = 2 * N_HEADS
    uvz_ref[0] = _pack(w_ref[0, :, 0:3 * d])
    qkvz_ref[0] = _pack(w_ref[0, :, 3 * d:ab0])
    gates_ref[0] = _pack(w_ref[0, :, ab0 + n_ab:ab0 + n_ab + 2 * d])
    ab_blk = w_ref[0, :, ab0:ab0 + LANES]
    lane = _iota2(ab_blk.shape, 1)
    ab_ref[0] = jnp.where(lane < n_ab, ab_blk, 0.0).astype(BF16)


def _cast_square_kernel(a_ref, b_ref, c_ref, oa_ref, ob_ref, oc_ref):
    oa_ref[0] = _pack(a_ref[0])
    ob_ref[0] = _pack(b_ref[0])
    oc_ref[0] = _pack(c_ref[0])


def _cast_weights(w_in, w_proj_a, w_proj_b, w_out):
    depth, d, in_dim = w_in.shape
    kdim = N_HEADS * HEAD_DIM
    rb = 256
    grid = (depth, d // rb)
    u32 = lambda n: jax.ShapeDtypeStruct((depth, d // 2, n), jnp.uint32)
    out_block = lambda n: pl.BlockSpec((1, rb // 2, n), lambda l, i: (l, i, 0))
    params = pltpu.CompilerParams(dimension_semantics=("arbitrary", "arbitrary"),
                                  vmem_limit_bytes=VMEM_LIMIT)
    uvz, qkvz, gates, ab = pl.pallas_call(
        _cast_w_in_kernel,
        grid=grid,
        in_specs=[pl.BlockSpec((1, rb, in_dim), lambda l, i: (l, i, 0))],
        out_specs=[out_block(3 * d), out_block(4 * kdim), out_block(2 * d),
                   pl.BlockSpec((1, rb, LANES), lambda l, i: (l, i, 0))],
        out_shape=[u32(3 * d), u32(4 * kdim), u32(2 * d),
                   jax.ShapeDtypeStruct((depth, d, LANES), BF16)],
        compiler_params=params,
        name="cast_w_in",
    )(w_in)
    sq_in = pl.BlockSpec((1, rb, d), lambda l, i: (l, i, 0))
    pa, pb, out = pl.pallas_call(
        _cast_square_kernel,
        grid=grid,
        in_specs=[sq_in, sq_in, sq_in],
        out_specs=[out_block(d)] * 3,
        out_shape=[u32(d)] * 3,
        compiler_params=params,
        name="cast_w_square",
    )(w_proj_a, w_proj_b, w_out)
    return {"w_uvz": uvz, "w_qkvz": qkvz, "w_gates": gates, "w_ab": ab,
            "w_pa": pa, "w_pb": pb, "w_out": out}


def _prepare_weights(pre_norm, w_in, gmlp_norm, w_spatial, b_spatial, conv_w, a_log, dt_bias,
                     gdn_norm, w_proj_a, w_proj_b, w_out, post_norm):
    depth = w_in.shape[0]
    pad_lanes = lambda a: jnp.pad(a, ((0, 0), (0, LANES - N_HEADS)))[:, None, :]
    rows = jnp.arange(CHUNK)
    tok = (rows % SUBLANES) * VROWS + rows // SUBLANES
    return {
        **_cast_weights(w_in, w_proj_a, w_proj_b, w_out),
        "pre_g": pre_norm[:, None, :],
        "gmlp_g": gmlp_norm[:, None, :],
        "w_s": w_spatial[:, :, tok, :][:, :, :, tok],
        "b_s": jnp.broadcast_to(b_spatial[:, :, tok, None], (depth, N_HEADS, CHUNK, HEAD_DIM)),
        "ws00": jnp.repeat(w_spatial[:, :, 0, 0], HEAD_DIM, axis=1)[:, None, :],
        "bs0": jnp.repeat(b_spatial[:, :, 0], HEAD_DIM, axis=1)[:, None, :],
        "conv_w": conv_w,
        "a_log": pad_lanes(a_log),
        "dt_bias": pad_lanes(dt_bias),
        "gdn_g": gdn_norm[:, None, :],
        "post_g": post_norm[:, None, :],
    }


def kernel(x_prompt, x_sample, state_gdn, state_conv, pre_norm, w_in, gmlp_norm, w_spatial, b_spatial, conv_w, a_log, dt_bias, gdn_norm, w_proj_a, w_proj_b, w_out, post_norm):
    depth = w_in.shape[0]
    wts = _prepare_weights(pre_norm, w_in, gmlp_norm, w_spatial, b_spatial, conv_w, a_log, dt_bias,
                           gdn_norm, w_proj_a, w_proj_b, w_out, post_norm)
    xp = _to_kernel_order(x_prompt)
    gdn_p, conv_p = [], []
    for l in range(depth):
        xp, sg_p, cb_p = _prompt_layer(xp, wts, l)
        gdn_p.append(sg_p)
        conv_p.append(cb_p)
    xp = _from_kernel_order(xp)
    ys, gdn_s, conv_s, vrows_s = _sample_path(x_sample[:, 0, :], state_gdn, state_conv, wts)
    return (xp, ys[:, None, :], jnp.stack(gdn_p), jnp.stack(conv_p), gdn_s, conv_s,
            vrows_s[:, :, None, :])
```

```python
import jax
import jax.numpy as jnp
from jax import lax
from jax.experimental import pallas as pl
from jax.experimental.pallas import tpu as pltpu

F32 = jnp.float32
BF16 = jnp.bfloat16
EPS = 1e-6

LANES = 128
SUBLANES = 8
HEAD_DIM = 128
N_HEADS = 8
CONV_W = 4
CHUNK = 128
TILE_M = 256
VROWS = CHUNK // SUBLANES
SAMPLE_ROWS = 8
HEAD_SPLIT = 2
VMEM_LIMIT = 56 * 1024 * 1024


def _dot(a, b):
    return jnp.dot(a.astype(BF16), b.astype(BF16), preferred_element_type=F32)


def _dot_nt(a, b):
    return lax.dot_general(a.astype(BF16), b.astype(BF16), (((1,), (1,)), ((), ())),
                           preferred_element_type=F32)


def _dot_tn(a, b):
    return lax.dot_general(a.astype(BF16), b.astype(BF16), (((0,), (0,)), ((), ())),
                           preferred_element_type=F32)


def _bdiag(a, b):
    z = jnp.zeros_like(a)
    return jnp.concatenate([jnp.concatenate([a, z], axis=1), jnp.concatenate([z, b], axis=1)], axis=0)


def _w(w_ref):
    return pltpu.bitcast(w_ref[...], BF16)


def _split(a):
    hi = a.astype(BF16)
    lo = (a - hi.astype(F32)).astype(BF16)
    return hi, lo


def _dot_exact(a, b):
    return jnp.dot(a, b, preferred_element_type=F32, precision=lax.Precision.HIGHEST)


def _rms(x, g):
    return x * lax.rsqrt(jnp.mean(x * x, axis=-1, keepdims=True) + EPS) * g


def _silu(x):
    return x * jax.nn.sigmoid(x)


def _softplus(x):
    return jnp.maximum(x, 0.0) + jnp.log(1.0 + jnp.exp(-jnp.abs(x)))


def _l2norm(x):
    return x * lax.rsqrt(jnp.sum(x * x, axis=-1, keepdims=True) + EPS)


def _iota2(shape, dim):
    return lax.broadcasted_iota(jnp.int32, shape, dim)


def _prompt_kernel(x_ref, pre_g_ref, w_uvz_ref, w_qkvz_ref, w_gates_ref, w_ab_ref, gmlp_g_ref, ws_ref, bs_ref,
                   convw_ref, alog_ref, dtb_ref, gdn_g_ref, w_pa_ref, w_pb_ref, w_out_ref,
                   post_g_ref,
                   y_ref, s_out_ref, conv_out_ref,
                   s_scr, carry_scr, ya_scr, yb_scr, qkv_scr):
    t = pl.program_id(1)
    nt = pl.num_programs(1)
    tm = x_ref.shape[1]
    d = x_ref.shape[2]
    n_chunks = tm // CHUNK
    kdim = N_HEADS * HEAD_DIM

    @pl.when(t == 0)
    def _():
        s_scr[...] = jnp.zeros_like(s_scr)
        carry_scr[...] = jnp.zeros_like(carry_scr)

    x = x_ref[0]
    h = _rms(x, pre_g_ref[...]).astype(BF16)

    row = _iota2((CHUNK, CHUNK), 0)
    col = _iota2((CHUNK, CHUNK), 1)
    token_of = lambda r: (r % SUBLANES) * VROWS + r // SUBLANES
    row_tok, col_tok = token_of(row), token_of(col)
    incl = row_tok >= col_tok
    strict = row_tok > col_tok

    qkvz = jnp.dot(h, _w(w_qkvz_ref), preferred_element_type=F32)
    uvz = jnp.dot(h, _w(w_uvz_ref), preferred_element_type=F32)
    rowid = _iota2((SUBLANES, 3 * kdim), 0)
    n_tail = CONV_W - 1
    prev = [carry_scr[j:j + 1, :] for j in range(n_tail)]
    for c in range(n_chunks):
        r0 = c * CHUNK
        pre = qkvz[r0:r0 + CHUNK, 0:3 * kdim]
        wrapped = []
        for j in range(n_tail):
            blk = pre[(VROWS - n_tail + j) * SUBLANES:(VROWS - n_tail + j + 1) * SUBLANES, :]
            shifted = pltpu.roll(blk, 1, axis=0)
            wrapped.append(jnp.where(rowid == 0, jnp.broadcast_to(prev[j], blk.shape), shifted))
            prev[j] = blk[SUBLANES - 1:SUBLANES, :]
        conv = pre * convw_ref[CONV_W - 1:CONV_W, :]
        for k in range(1, CONV_W):
            back_k = jnp.concatenate(wrapped[n_tail - k:] + [pre[0:(VROWS - k) * SUBLANES, :]], axis=0)
            conv = conv + back_k * convw_ref[CONV_W - 1 - k:CONV_W - k, :]
        qkv_scr[r0:r0 + CHUNK, :] = _silu(conv)
    tail = jnp.concatenate(prev, axis=0)
    carry_scr[0:n_tail, :] = tail

    @pl.when(t == nt - 1)
    def _():
        conv_out_ref[0] = tail

    gates = jnp.dot(h, _w(w_gates_ref), preferred_element_type=F32)
    ab = jnp.dot(h, w_ab_ref[...], preferred_element_type=F32)
    g_all = -jnp.exp(alog_ref[...]) * _softplus(ab + dtb_ref[...])
    beta_all = jax.nn.sigmoid(ab)
    g_all_t = g_all.T
    ltri = incl.astype(F32)
    utri = (row_tok <= col_tok).astype(F32)
    eye = (row == col).astype(F32)
    scale = HEAD_DIM ** -0.5

    items = [(c, hd) for c in range(n_chunks) for hd in range(N_HEADS)]
    gcs, egcs = [], []
    for c in range(n_chunks):
        r0, r1 = c * CHUNK, (c + 1) * CHUNK
        gcs.append(_dot_exact(ltri, g_all[r0:r1, :]))
        egcs.append(jnp.exp(gcs[c]))
    gcts = [_dot_exact(g_all_t[0:N_HEADS, c * CHUNK:(c + 1) * CHUNK], utri)
            for c in range(n_chunks)]

    n_pairs = len(items) // 2
    left = lambda a: a[:, 0:HEAD_DIM]
    right = lambda a: a[:, HEAD_DIM:2 * HEAD_DIM]
    both = lambda a, b: jnp.concatenate([a, b], axis=1)
    stack = lambda a, b: jnp.concatenate([a, b], axis=0)

    qs, ks = [], []
    for c, hd in items:
        r0, r1 = c * CHUNK, (c + 1) * CHUNK
        lo, hi = hd * HEAD_DIM, (hd + 1) * HEAD_DIM
        qs.append(_l2norm(qkv_scr[r0:r1, lo:hi]) * scale)
        ks.append(_l2norm(qkv_scr[r0:r1, kdim + lo:kdim + hi]))
    kkqks = []
    for j in range(n_pairs):
        i1, i2 = 2 * j, 2 * j + 1
        k1, k2 = ks[i1].astype(BF16), ks[i2].astype(BF16)
        lhs = both(stack(k1, qs[i1].astype(BF16)), stack(k2, qs[i2].astype(BF16)))
        kkqks.append(_dot_nt(lhs, _bdiag(k1, k2)))

    vn = _rms(uvz[:, d:2 * d], gmlp_g_ref[...])
    vnb = vn.astype(BF16)
    for g in range(N_HEADS):
        wsg = jnp.where(incl, ws_ref[g], 0.0).astype(BF16)
        lo, hi = g * HEAD_DIM, (g + 1) * HEAD_DIM
        mixed_all = jnp.dot(
            wsg, jnp.concatenate([vnb[c * CHUNK:(c + 1) * CHUNK, lo:hi] for c in range(n_chunks)], axis=1),
            preferred_element_type=F32)
        for c in range(n_chunks):
            r0, r1 = c * CHUNK, (c + 1) * CHUNK
            mixed = mixed_all[:, c * HEAD_DIM:(c + 1) * HEAD_DIM] + bs_ref[g]
            u = uvz[r0:r1, lo:hi]
            z = uvz[r0:r1, 2 * d + lo:2 * d + hi]
            ya_scr[r0:r1, lo:hi] = (u * mixed * _silu(z)).astype(BF16)

    a_mats, attns, rhss = [], [], []
    for i, (c, hd) in enumerate(items):
        r0, r1 = c * CHUNK, (c + 1) * CHUNK
        lo, hi = hd * HEAD_DIM, (hd + 1) * HEAD_DIM
        kkqk = kkqks[i // 2][:, (i % 2) * HEAD_DIM:(i % 2 + 1) * HEAD_DIM]
        gcol = gcs[c][:, hd:hd + 1]
        grow = gcts[c][hd:hd + 1, :]
        egcol = egcs[c][:, hd:hd + 1]
        bcol = beta_all[r0:r1, N_HEADS + hd:N_HEADS + hd + 1]
        dec = jnp.where(incl, jnp.exp(jnp.where(incl, gcol - grow, 0.0)), 0.0)
        a_mats.append(jnp.where(strict, bcol * kkqk[0:CHUNK] * dec, 0.0))
        attns.append(kkqk[CHUNK:2 * CHUNK] * dec)
        vh = qkv_scr[r0:r1, 2 * kdim + lo:2 * kdim + hi]
        rhss.append(jnp.concatenate([vh * bcol, ks[i] * (bcol * egcol)], axis=1))

    pa = jnp.dot(ya_scr[...], _w(w_pa_ref), preferred_element_type=F32)

    p_pairs = [both(eye - a_mats[2 * j], eye - a_mats[2 * j + 1]) for j in range(n_pairs)]
    b_pairs = [both(a_mats[2 * j], a_mats[2 * j + 1]).astype(BF16) for j in range(n_pairs)]
    b_pairs = [_dot(b, _bdiag(left(b), right(b))).astype(BF16) for b in b_pairs]
    n_sq = (CHUNK - 1).bit_length() - 1
    for _ in range(n_sq - 1):
        prods = [_dot(stack(p.astype(BF16), b), _bdiag(left(b), right(b)))
                 for p, b in zip(p_pairs, b_pairs)]
        p_pairs = [p + pr[0:CHUNK] for p, pr in zip(p_pairs, prods)]
        b_pairs = [pr[CHUNK:2 * CHUNK].astype(BF16) for pr in prods]
    p_pairs = [p + _dot(p, _bdiag(left(b), right(b))) for p, b in zip(p_pairs, b_pairs)]
    p_mats = [half(p_pairs[j]).astype(BF16) for j in range(n_pairs) for half in (left, right)]
    uws = [_dot(p, r) for p, r in zip(p_mats, rhss)]
    resids = []
    for a, uw, r in zip(a_mats, uws, rhss):
        a_hi, a_lo = _split(a)
        u_hi, u_lo = _split(uw)
        au = _dot(both(a_hi, a_lo), stack(u_hi, u_hi)) + _dot(a_hi, u_lo)
        resids.append(r - uw - au)
    uws = [uw + _dot(p, rs) for uw, p, rs in zip(uws, p_mats, resids)]

    gated_pa = jax.nn.sigmoid(gates[:, 0:d]) * pa
    gate_b = jax.nn.sigmoid(gates[:, d:2 * d])

    for c in range(n_chunks):
        r0, r1 = c * CHUNK, (c + 1) * CHUNK
        idx = [c * N_HEADS + hd for hd in range(N_HEADS)]
        s_olds = [s_scr[hd] for hd in range(N_HEADS)]
        wq_l = lambda hd, i: stack(uws[i][:, HEAD_DIM:], qs[i] * egcs[c][:, hd:hd + 1]).astype(BF16)
        wq_pairs = [_dot(both(wq_l(hd, idx[hd]), wq_l(hd + 1, idx[hd + 1])),
                         _bdiag(s_olds[hd].astype(BF16), s_olds[hd + 1].astype(BF16)))
                    for hd in range(0, N_HEADS, 2)]
        wqs = [half(wq_pairs[hd // 2]) for hd in range(0, N_HEADS, 2) for half in (left, right)]
        v_news = [uws[i][:, 0:HEAD_DIM] - wqs[hd][0:CHUNK] for hd, i in enumerate(idx)]
        av_pairs = [_dot(both(attns[idx[hd]], attns[idx[hd + 1]]),
                         _bdiag(v_news[hd].astype(BF16), v_news[hd + 1].astype(BF16)))
                    for hd in range(0, N_HEADS, 2)]
        avs = [half(av_pairs[hd // 2]) for hd in range(0, N_HEADS, 2) for half in (left, right)]
        os_ = [wqs[hd][CHUNK:2 * CHUNK] + avs[hd] for hd in range(N_HEADS)]
        for hd, i in enumerate(idx):
            glast = gcs[c][CHUNK - 1:CHUNK, hd:hd + 1]
            kd = ks[i] * jnp.exp(glast - gcs[c][:, hd:hd + 1])
            s_scr[hd] = s_olds[hd] * jnp.exp(glast) + _dot_tn(kd, v_news[hd])
        for hd in range(N_HEADS):
            lo, hi = hd * HEAD_DIM, (hd + 1) * HEAD_DIM
            zb = qkvz[r0:r1, 3 * kdim + lo:3 * kdim + hi]
            yb_scr[r0:r1, lo:hi] = (_rms(os_[hd], gdn_g_ref[...]) * _silu(zb)).astype(BF16)

    @pl.when(t == nt - 1)
    def _():
        s_out_ref[0] = s_scr[...]

    pb = jnp.dot(yb_scr[...], _w(w_pb_ref), preferred_element_type=F32)

    merged = gated_pa + gate_b * pb
    out = jnp.dot(merged.astype(BF16), _w(w_out_ref), preferred_element_type=F32)
    y_ref[0] = x + _rms(out, post_g_ref[...])


def _layer_spec(arr, layer, n_grid):
    shape = (None,) + tuple(arr.shape[1:])
    zeros = (0,) * (arr.ndim - 1)
    if n_grid == 2:
        index_map = lambda b, t: (layer,) + zeros
    else:
        index_map = lambda l, i, j: (l,) + zeros
    return pl.BlockSpec(shape, index_map, pipeline_mode=pl.Buffered(1))


_PROMPT_PARAMS = ("pre_g", "w_uvz", "w_qkvz", "w_gates", "w_ab", "gmlp_g", "w_s", "b_s", "conv_w", "a_log", "dt_bias",
                  "gdn_g", "w_pa", "w_pb", "w_out", "post_g")


def _prompt_layer(x, wts, layer):
    bsz, seq, d = x.shape
    tm = min(TILE_M, seq)
    nt = seq // tm
    kdim = N_HEADS * HEAD_DIM
    conv_dim = 3 * kdim
    params = [wts[name] for name in _PROMPT_PARAMS]
    in_specs = [pl.BlockSpec((1, tm, d), lambda b, t: (b, t, 0))]
    in_specs += [_layer_spec(p, layer, 2) for p in params]
    out_specs = [
        pl.BlockSpec((1, tm, d), lambda b, t: (b, t, 0)),
        pl.BlockSpec((1, N_HEADS, HEAD_DIM, HEAD_DIM), lambda b, t: (b, 0, 0, 0)),
        pl.BlockSpec((1, CONV_W - 1, conv_dim), lambda b, t: (b, 0, 0)),
    ]
    out_shape = [
        jax.ShapeDtypeStruct((bsz, seq, d), F32),
        jax.ShapeDtypeStruct((bsz, N_HEADS, HEAD_DIM, HEAD_DIM), F32),
        jax.ShapeDtypeStruct((bsz, CONV_W - 1, conv_dim), F32),
    ]
    scratch = [
        pltpu.VMEM((N_HEADS, HEAD_DIM, HEAD_DIM), F32),
        pltpu.VMEM((SUBLANES, conv_dim), F32),
        pltpu.VMEM((tm, d), BF16),
        pltpu.VMEM((tm, kdim), BF16),
        pltpu.VMEM((tm, conv_dim), F32),
    ]
    return pl.pallas_call(
        _prompt_kernel,
        grid=(bsz, nt),
        in_specs=in_specs,
        out_specs=out_specs,
        out_shape=out_shape,
        scratch_shapes=scratch,
        compiler_params=pltpu.CompilerParams(
            dimension_semantics=("arbitrary", "arbitrary"),
            vmem_limit_bytes=VMEM_LIMIT),
        name="prompt_layer",
    )(x, *params)


def _sample_kernel(x_ref, s_ref, cs_ref, pre_g_ref, w_uvz_ref, w_qkvz_ref, w_gates_ref, w_ab_ref, gmlp_g_ref, ws00_ref,
                   bs0_ref, convw_ref, alog_ref, dtb_ref, gdn_g_ref, w_pa_ref, w_pb_ref, w_out_ref,
                   post_g_ref,
                   y_ref, s_out_ref, conv_out_ref, vrows_ref,
                   xs_scr, qkvn_scr, zb_scr, sga_pa_scr, sgb_scr, beta_scr, eg_scr, o_scr,
                   q_t, k_t, v_t, beta_t, eg_t, o_t):
    layer = pl.program_id(0)
    i = pl.program_id(1)
    hh = pl.program_id(2)
    n_tiles = pl.num_programs(1)
    n_split = pl.num_programs(2)
    d = x_ref.shape[1]
    kdim = N_HEADS * HEAD_DIM
    heads_per = N_HEADS // HEAD_SPLIT
    wid = heads_per * HEAD_DIM
    rows = SAMPLE_ROWS
    scale = HEAD_DIM ** -0.5

    @pl.when(jnp.logical_and(i == 0, hh == 0))
    def _():
        @pl.when(layer == 0)
        def _():
            xs_scr[...] = x_ref[...]

        x = xs_scr[...]
        h = _rms(x, pre_g_ref[...]).astype(BF16)
        uvz = jnp.dot(h, _w(w_uvz_ref), preferred_element_type=F32)
        vn = _rms(uvz[:, d:2 * d], gmlp_g_ref[...])
        vrows_ref[0] = vn
        mixed = ws00_ref[...] * vn + bs0_ref[...]
        ya = uvz[:, 0:d] * mixed * _silu(uvz[:, 2 * d:3 * d])
        pa = jnp.dot(ya.astype(BF16), _w(w_pa_ref), preferred_element_type=F32)
        qkvz = jnp.dot(h, _w(w_qkvz_ref), preferred_element_type=F32)
        qkvn_scr[...] = qkvz[:, 0:3 * kdim]
        zb_scr[...] = qkvz[:, 3 * kdim:4 * kdim]
        ab = jnp.dot(h, w_ab_ref[...], preferred_element_type=F32)
        eg_scr[...] = jnp.exp(-jnp.exp(alog_ref[...]) * _softplus(ab + dtb_ref[...]))
        beta_scr[...] = jax.nn.sigmoid(ab)
        gates = jnp.dot(h, _w(w_gates_ref),
                        preferred_element_type=F32)
        sga_pa_scr[...] = jax.nn.sigmoid(gates[:, 0:d]) * pa
        sgb_scr[...] = jax.nn.sigmoid(gates[:, d:2 * d])

    b0 = pl.multiple_of(i * rows, rows)

    @pl.when(hh == 0)
    def _():
        cs = cs_ref[0]
        qkv_new = qkvn_scr[pl.ds(b0, rows), :]
        conv = (cs[0] * convw_ref[0:1, :] + cs[1] * convw_ref[1:2, :]
                + cs[2] * convw_ref[2:3, :] + qkv_new * convw_ref[3:4, :])
        conv_out_ref[0, 0] = cs[1]
        conv_out_ref[0, 1] = cs[2]
        conv_out_ref[0, 2] = qkv_new
        qkv = _silu(conv)
        beta_rows = beta_scr[pl.ds(b0, rows), :]
        eg_rows = eg_scr[pl.ds(b0, rows), :]
        for hd in range(N_HEADS):
            lo, hi = hd * HEAD_DIM, (hd + 1) * HEAD_DIM
            half, off = hd // heads_per, (hd % heads_per) * HEAD_DIM
            q_t[half, :, off:off + HEAD_DIM] = _l2norm(qkv[:, lo:hi]) * scale
            k_t[half, :, off:off + HEAD_DIM] = _l2norm(qkv[:, kdim + lo:kdim + hi])
            v_t[half, :, off:off + HEAD_DIM] = qkv[:, 2 * kdim + lo:2 * kdim + hi]
            beta_t[hd] = jnp.broadcast_to(beta_rows[:, N_HEADS + hd:N_HEADS + hd + 1], (rows, HEAD_DIM))
            eg_t[hd] = jnp.broadcast_to(eg_rows[:, hd:hd + 1], (rows, HEAD_DIM))

    rowid = _iota2((SUBLANES, HEAD_DIM), 0)
    states = [(r, j) for r in range(rows) for j in range(heads_per)]
    qv, kv, vv, bv, ev, sv, ksqs = [], [], [], [], [], [], []
    for r, j in states:
        off = j * HEAD_DIM
        q = q_t[hh, r:r + 1, off:off + HEAD_DIM]
        k = k_t[hh, r:r + 1, off:off + HEAD_DIM]
        qv.append(q)
        kv.append(k)
        vv.append(v_t[hh, r:r + 1, off:off + HEAD_DIM])
        bv.append(beta_t[hh * heads_per + j, r:r + 1, :])
        ev.append(eg_t[hh * heads_per + j, r:r + 1, :])
        s = s_ref[0, r, j]
        sv.append(s)
        kq8 = jnp.where(rowid == 0, jnp.broadcast_to(k, (SUBLANES, HEAD_DIM)),
                        jnp.where(rowid == 1, jnp.broadcast_to(q, (SUBLANES, HEAD_DIM)), 0.0))
        ksqs.append(_dot(kq8, s))
    for n, (r, j) in enumerate(states):
        v_new = bv[n] * (vv[n] - ev[n] * ksqs[n][0:1, :])
        qk = jnp.sum(qv[n] * kv[n], axis=-1, keepdims=True)
        o_t[r:r + 1, j * HEAD_DIM:(j + 1) * HEAD_DIM] = ev[n] * ksqs[n][1:2, :] + qk * v_new
        k_hi, k_lo = _split(kv[n])
        v_hi, v_lo = _split(v_new)
        bc = lambda a: jnp.broadcast_to(a.astype(F32), (SUBLANES, HEAD_DIM))
        k8 = jnp.where(rowid <= 1, bc(k_hi), jnp.where(rowid == 2, bc(k_lo), 0.0))
        v8 = jnp.where(rowid == 0, bc(v_hi), jnp.where(rowid == 1, bc(v_lo),
                                                       jnp.where(rowid == 2, bc(v_hi), 0.0)))
        s_out_ref[0, r, j] = sv[n] * ev[n] + _dot_tn(k8, v8)
    o_scr[hh, pl.ds(b0, rows), :] = o_t[...]

    @pl.when(jnp.logical_and(i == n_tiles - 1, hh == n_split - 1))
    def _():
        pb = jnp.zeros((x_ref.shape[0], d), F32)
        for hd in range(N_HEADS):
            lo, hi = hd * HEAD_DIM, (hd + 1) * HEAD_DIM
            half, off = hd // heads_per, (hd % heads_per) * HEAD_DIM
            o = o_scr[half, :, off:off + HEAD_DIM]
            yb = (_rms(o, gdn_g_ref[...]) * _silu(zb_scr[:, lo:hi])).astype(BF16)
            pb = pb + jnp.dot(yb, pltpu.bitcast(w_pb_ref[lo // 2:hi // 2, :], BF16),
                              preferred_element_type=F32)
        merged = sga_pa_scr[...] + sgb_scr[...] * pb
        out = jnp.dot(merged.astype(BF16), _w(w_out_ref), preferred_element_type=F32)
        y = xs_scr[...] + _rms(out, post_g_ref[...])
        xs_scr[...] = y
        y_ref[...] = y


_SAMPLE_PARAMS = ("pre_g", "w_uvz", "w_qkvz", "w_gates", "w_ab", "gmlp_g", "ws00", "bs0", "conv_w", "a_log", "dt_bias",
                  "gdn_g", "w_pa", "w_pb", "w_out", "post_g")


def _sample_path(x, state_gdn, state_conv, wts):
    n, d = x.shape
    depth = state_gdn.shape[0]
    kdim = N_HEADS * HEAD_DIM
    conv_dim = 3 * kdim
    heads_per = N_HEADS // HEAD_SPLIT
    rows = SAMPLE_ROWS
    params = [wts[name] for name in _SAMPLE_PARAMS]
    s_spec = pl.BlockSpec((1, rows, heads_per, HEAD_DIM, HEAD_DIM), lambda l, i, j: (l, i, j, 0, 0))
    c_spec = pl.BlockSpec((1, CONV_W - 1, rows, conv_dim), lambda l, i, j: (l, 0, i, 0))
    in_specs = [pl.BlockSpec((n, d), lambda l, i, j: (0, 0)), s_spec, c_spec]
    in_specs += [_layer_spec(p, None, 3) for p in params]
    out_specs = [
        pl.BlockSpec((n, d), lambda l, i, j: (0, 0)),
        s_spec,
        c_spec,
        pl.BlockSpec((1, n, d), lambda l, i, j: (l, 0, 0)),
    ]
    out_shape = [
        jax.ShapeDtypeStruct((n, d), F32),
        jax.ShapeDtypeStruct(state_gdn.shape, F32),
        jax.ShapeDtypeStruct(state_conv.shape, F32),
        jax.ShapeDtypeStruct((depth, n, d), F32),
    ]
    scratch = [
        pltpu.VMEM((n, d), F32),
        pltpu.VMEM((n, conv_dim), F32),
        pltpu.VMEM((n, kdim), F32),
        pltpu.VMEM((n, d), F32),
        pltpu.VMEM((n, d), F32),
        pltpu.VMEM((n, LANES), F32),
        pltpu.VMEM((n, LANES), F32),
        pltpu.VMEM((HEAD_SPLIT, n, heads_per * HEAD_DIM), F32),
        pltpu.VMEM((HEAD_SPLIT, rows, heads_per * HEAD_DIM), F32),
        pltpu.VMEM((HEAD_SPLIT, rows, heads_per * HEAD_DIM), F32),
        pltpu.VMEM((HEAD_SPLIT, rows, heads_per * HEAD_DIM), F32),
        pltpu.VMEM((N_HEADS, rows, HEAD_DIM), F32),
        pltpu.VMEM((N_HEADS, rows, HEAD_DIM), F32),
        pltpu.VMEM((rows, heads_per * HEAD_DIM), F32),
    ]
    return pl.pallas_call(
        _sample_kernel,
        grid=(depth, n // rows, HEAD_SPLIT),
        in_specs=in_specs,
        out_specs=out_specs,
        out_shape=out_shape,
        scratch_shapes=scratch,
        compiler_params=pltpu.CompilerParams(
            dimension_semantics=("arbitrary", "arbitrary", "arbitrary"),
            vmem_limit_bytes=VMEM_LIMIT),
        name="sample_path",
    )(x, state_gdn, state_conv, *params)


def _to_kernel_order(x):
    b, l, d = x.shape
    return x.reshape(b, l // CHUNK, SUBLANES, VROWS, d).swapaxes(2, 3).reshape(b, l, d)


def _from_kernel_order(x):
    b, l, d = x.shape
    return x.reshape(b, l // CHUNK, VROWS, SUBLANES, d).swapaxes(2, 3).reshape(b, l, d)


def _pack(x):
    return pltpu.bitcast(x.astype(BF16), jnp.uint32)


def _cast_w_in_kernel(wt_ref, uvz_ref, qkvz_ref, gates_ref, ab_ref):
    n_uvz, n_qkvz, n_gates = uvz_ref.shape[2], qkvz_ref.shape[2], gates_ref.shape[2]
    ab0 = n_uvz + n_qkvz
    n_ab = 2 * N_HEADS
    uvz_ref[0] = _pack(wt_ref[0, 0:n_uvz, :].T)
    qkvz_ref[0] = _pack(wt_ref[0, n_uvz:ab0, :].T)
    gates_ref[0] = _pack(wt_ref[0, ab0 + n_ab:ab0 + n_ab + n_gates, :].T)
    ab_blk = wt_ref[0, ab0:ab0 + LANES, :].T
    lane = _iota2(ab_blk.shape, 1)
    ab_ref[0] = jnp.where(lane < n_ab, ab_blk, 0.0).astype(BF16)


def _cast_square_kernel(a_ref, b_ref, c_ref, oa_ref, ob_ref, oc_ref):
    oa_ref[0] = _pack(a_ref[0])
    ob_ref[0] = _pack(b_ref[0])
    oc_ref[0] = _pack(c_ref[0])


def _cast_weights(w_in, w_proj_a, w_proj_b, w_out):
    depth, d, in_dim = w_in.shape
    kdim = N_HEADS * HEAD_DIM
    rb = 256
    grid = (depth, d // rb)
    u32 = lambda n: jax.ShapeDtypeStruct((depth, d // 2, n), jnp.uint32)
    out_block = lambda n: pl.BlockSpec((1, rb // 2, n), lambda l, i: (l, i, 0))
    params = pltpu.CompilerParams(dimension_semantics=("arbitrary", "arbitrary"),
                                  vmem_limit_bytes=VMEM_LIMIT)
    uvz, qkvz, gates, ab = pl.pallas_call(
        _cast_w_in_kernel,
        grid=grid,
        in_specs=[pl.BlockSpec((1, in_dim, rb), lambda l, i: (l, 0, i))],
        out_specs=[out_block(3 * d), out_block(4 * kdim), out_block(2 * d),
                   pl.BlockSpec((1, rb, LANES), lambda l, i: (l, i, 0))],
        out_shape=[u32(3 * d), u32(4 * kdim), u32(2 * d),
                   jax.ShapeDtypeStruct((depth, d, LANES), BF16)],
        compiler_params=params,
        name="cast_w_in",
    )(jnp.swapaxes(w_in, 1, 2))
    sq_in = pl.BlockSpec((1, rb, d), lambda l, i: (l, i, 0))
    pa, pb, out = pl.pallas_call(
        _cast_square_kernel,
        grid=grid,
        in_specs=[sq_in, sq_in, sq_in],
        out_specs=[out_block(d)] * 3,
        out_shape=[u32(d)] * 3,
        compiler_params=params,
        name="cast_w_square",
    )(w_proj_a, w_proj_b, w_out)
    return {"w_uvz": uvz, "w_qkvz": qkvz, "w_gates": gates, "w_ab": ab,
            "w_pa": pa, "w_pb": pb, "w_out": out}


def _prepare_weights(pre_norm, w_in, gmlp_norm, w_spatial, b_spatial, conv_w, a_log, dt_bias,
                     gdn_norm, w_proj_a, w_proj_b, w_out, post_norm):
    depth = w_in.shape[0]
    pad_lanes = lambda a: jnp.pad(a, ((0, 0), (0, LANES - N_HEADS)))[:, None, :]
    rows = jnp.arange(CHUNK)
    tok = (rows % SUBLANES) * VROWS + rows // SUBLANES
    return {
        **_cast_weights(w_in, w_proj_a, w_proj_b, w_out),
        "pre_g": pre_norm[:, None, :],
        "gmlp_g": gmlp_norm[:, None, :],
        "w_s": w_spatial[:, :, tok, :][:, :, :, tok],
        "b_s": jnp.broadcast_to(b_spatial[:, :, tok, None], (depth, N_HEADS, CHUNK, HEAD_DIM)),
        "ws00": jnp.repeat(w_spatial[:, :, 0, 0], HEAD_DIM, axis=1)[:, None, :],
        "bs0": jnp.repeat(b_spatial[:, :, 0], HEAD_DIM, axis=1)[:, None, :],
        "conv_w": conv_w,
        "a_log": pad_lanes(a_log),
        "dt_bias": pad_lanes(dt_bias),
        "gdn_g": gdn_norm[:, None, :],
        "post_g": post_norm[:, None, :],
    }


def kernel(x_prompt, x_sample, state_gdn, state_conv, pre_norm, w_in, gmlp_norm, w_spatial, b_spatial, conv_w, a_log, dt_bias, gdn_norm, w_proj_a, w_proj_b, w_out, post_norm):
    depth = w_in.shape[0]
    wts = _prepare_weights(pre_norm, w_in, gmlp_norm, w_spatial, b_spatial, conv_w, a_log, dt_bias,
                           gdn_norm, w_proj_a, w_proj_b, w_out, post_norm)
    xp = _to_kernel_order(x_prompt)
    gdn_p, conv_p = [], []
    for l in range(depth):
        xp, sg_p, cb_p = _prompt_layer(xp, wts, l)
        gdn_p.append(sg_p)
        conv_p.append(cb_p)
    xp = _from_kernel_order(xp)
    ys, gdn_s, conv_s, vrows_s = _sample_path(x_sample[:, 0, :], state_gdn,
                                              jnp.swapaxes(state_conv, 1, 2), wts)
    return (xp, ys[:, None, :], jnp.stack(gdn_p), jnp.stack(conv_p), gdn_s,
            jnp.swapaxes(conv_s, 1, 2), vrows_s[:, :, None, :])
```

```python
import jax
import jax.numpy as jnp
from jax import lax
from jax.experimental import pallas as pl
from jax.experimental.pallas import tpu as pltpu

F32 = jnp.float32
BF16 = jnp.bfloat16
EPS = 1e-6

LANES = 128
SUBLANES = 8
HEAD_DIM = 128
N_HEADS = 8
CONV_W = 4
CHUNK = 128
TILE_M = 256
VROWS = CHUNK // SUBLANES
SAMPLE_ROWS = 8
HEAD_SPLIT = 1
VMEM_LIMIT = 56 * 1024 * 1024


def _dot(a, b):
    return jnp.dot(a.astype(BF16), b.astype(BF16), preferred_element_type=F32)


def _dot_nt(a, b):
    return lax.dot_general(a.astype(BF16), b.astype(BF16), (((1,), (1,)), ((), ())),
                           preferred_element_type=F32)


def _dot_tn(a, b):
    return lax.dot_general(a.astype(BF16), b.astype(BF16), (((0,), (0,)), ((), ())),
                           preferred_element_type=F32)


def _bdiag(a, b):
    z = jnp.zeros_like(a)
    return jnp.concatenate([jnp.concatenate([a, z], axis=1), jnp.concatenate([z, b], axis=1)], axis=0)


def _w(w_ref):
    return pltpu.bitcast(w_ref[...], BF16)


def _split(a):
    hi = a.astype(BF16)
    lo = (a - hi.astype(F32)).astype(BF16)
    return hi, lo


def _dot_exact(a, b):
    return jnp.dot(a, b, preferred_element_type=F32, precision=lax.Precision.HIGHEST)


def _rms(x, g):
    return x * lax.rsqrt(jnp.mean(x * x, axis=-1, keepdims=True) + EPS) * g


def _silu(x):
    return x * jax.nn.sigmoid(x)


def _softplus(x):
    return jnp.maximum(x, 0.0) + jnp.log(1.0 + jnp.exp(-jnp.abs(x)))


def _l2norm(x):
    return x * lax.rsqrt(jnp.sum(x * x, axis=-1, keepdims=True) + EPS)


def _iota2(shape, dim):
    return lax.broadcasted_iota(jnp.int32, shape, dim)


def _prompt_kernel(x_ref, pre_g_ref, w_uvz_ref, w_qkvz_ref, w_gates_ref, w_ab_ref, gmlp_g_ref, ws_ref, bs_ref,
                   convw_ref, alog_ref, dtb_ref, gdn_g_ref, w_pa_ref, w_pb_ref, w_out_ref,
                   post_g_ref,
                   y_ref, s_out_ref, conv_out_ref,
                   s_scr, carry_scr, ya_scr, yb_scr, qkv_scr):
    t = pl.program_id(1)
    nt = pl.num_programs(1)
    tm = x_ref.shape[1]
    d = x_ref.shape[2]
    n_chunks = tm // CHUNK
    kdim = N_HEADS * HEAD_DIM

    @pl.when(t == 0)
    def _():
        s_scr[...] = jnp.zeros_like(s_scr)
        carry_scr[...] = jnp.zeros_like(carry_scr)

    x = x_ref[0]
    h = _rms(x, pre_g_ref[...]).astype(BF16)

    row = _iota2((CHUNK, CHUNK), 0)
    col = _iota2((CHUNK, CHUNK), 1)
    token_of = lambda r: (r % SUBLANES) * VROWS + r // SUBLANES
    row_tok, col_tok = token_of(row), token_of(col)
    incl = row_tok >= col_tok
    strict = row_tok > col_tok

    qkvz = jnp.dot(h, _w(w_qkvz_ref), preferred_element_type=F32)
    uvz = jnp.dot(h, _w(w_uvz_ref), preferred_element_type=F32)
    rowid = _iota2((SUBLANES, 3 * kdim), 0)
    n_tail = CONV_W - 1
    prev = [carry_scr[j:j + 1, :] for j in range(n_tail)]
    for c in range(n_chunks):
        r0 = c * CHUNK
        pre = qkvz[r0:r0 + CHUNK, 0:3 * kdim]
        wrapped = []
        for j in range(n_tail):
            blk = pre[(VROWS - n_tail + j) * SUBLANES:(VROWS - n_tail + j + 1) * SUBLANES, :]
            shifted = pltpu.roll(blk, 1, axis=0)
            wrapped.append(jnp.where(rowid == 0, jnp.broadcast_to(prev[j], blk.shape), shifted))
            prev[j] = blk[SUBLANES - 1:SUBLANES, :]
        conv = pre * convw_ref[CONV_W - 1:CONV_W, :]
        for k in range(1, CONV_W):
            back_k = jnp.concatenate(wrapped[n_tail - k:] + [pre[0:(VROWS - k) * SUBLANES, :]], axis=0)
            conv = conv + back_k * convw_ref[CONV_W - 1 - k:CONV_W - k, :]
        qkv_scr[r0:r0 + CHUNK, :] = _silu(conv)
    tail = jnp.concatenate(prev, axis=0)
    carry_scr[0:n_tail, :] = tail

    @pl.when(t == nt - 1)
    def _():
        conv_out_ref[0] = tail

    gates = jnp.dot(h, _w(w_gates_ref), preferred_element_type=F32)
    ab = jnp.dot(h, w_ab_ref[...], preferred_element_type=F32)
    g_all = -jnp.exp(alog_ref[...]) * _softplus(ab + dtb_ref[...])
    beta_all = jax.nn.sigmoid(ab)
    g_all_t = g_all.T
    ltri = incl.astype(F32)
    utri = (row_tok <= col_tok).astype(F32)
    eye = (row == col).astype(F32)
    scale = HEAD_DIM ** -0.5

    items = [(c, hd) for c in range(n_chunks) for hd in range(N_HEADS)]
    gcs, egcs = [], []
    for c in range(n_chunks):
        r0, r1 = c * CHUNK, (c + 1) * CHUNK
        gcs.append(_dot_exact(ltri, g_all[r0:r1, :]))
        egcs.append(jnp.exp(gcs[c]))
    gcts = [_dot_exact(g_all_t[0:N_HEADS, c * CHUNK:(c + 1) * CHUNK], utri)
            for c in range(n_chunks)]

    n_pairs = len(items) // 2
    left = lambda a: a[:, 0:HEAD_DIM]
    right = lambda a: a[:, HEAD_DIM:2 * HEAD_DIM]
    both = lambda a, b: jnp.concatenate([a, b], axis=1)
    stack = lambda a, b: jnp.concatenate([a, b], axis=0)

    qs, ks = [], []
    for c, hd in items:
        r0, r1 = c * CHUNK, (c + 1) * CHUNK
        lo, hi = hd * HEAD_DIM, (hd + 1) * HEAD_DIM
        qs.append(_l2norm(qkv_scr[r0:r1, lo:hi]) * scale)
        ks.append(_l2norm(qkv_scr[r0:r1, kdim + lo:kdim + hi]))
    kkqks = []
    for j in range(n_pairs):
        i1, i2 = 2 * j, 2 * j + 1
        k1, k2 = ks[i1].astype(BF16), ks[i2].astype(BF16)
        lhs = both(stack(k1, qs[i1].astype(BF16)), stack(k2, qs[i2].astype(BF16)))
        kkqks.append(_dot_nt(lhs, _bdiag(k1, k2)))

    vn = _rms(uvz[:, d:2 * d], gmlp_g_ref[...])
    vnb = vn.astype(BF16)
    for g in range(N_HEADS):
        wsg = jnp.where(incl, ws_ref[g], 0.0).astype(BF16)
        lo, hi = g * HEAD_DIM, (g + 1) * HEAD_DIM
        mixed_all = jnp.dot(
            wsg, jnp.concatenate([vnb[c * CHUNK:(c + 1) * CHUNK, lo:hi] for c in range(n_chunks)], axis=1),
            preferred_element_type=F32)
        for c in range(n_chunks):
            r0, r1 = c * CHUNK, (c + 1) * CHUNK
            mixed = mixed_all[:, c * HEAD_DIM:(c + 1) * HEAD_DIM] + bs_ref[g]
            u = uvz[r0:r1, lo:hi]
            z = uvz[r0:r1, 2 * d + lo:2 * d + hi]
            ya_scr[r0:r1, lo:hi] = (u * mixed * _silu(z)).astype(BF16)

    a_mats, attns, rhss = [], [], []
    for i, (c, hd) in enumerate(items):
        r0, r1 = c * CHUNK, (c + 1) * CHUNK
        lo, hi = hd * HEAD_DIM, (hd + 1) * HEAD_DIM
        kkqk = kkqks[i // 2][:, (i % 2) * HEAD_DIM:(i % 2 + 1) * HEAD_DIM]
        gcol = gcs[c][:, hd:hd + 1]
        grow = gcts[c][hd:hd + 1, :]
        egcol = egcs[c][:, hd:hd + 1]
        bcol = beta_all[r0:r1, N_HEADS + hd:N_HEADS + hd + 1]
        dec = jnp.where(incl, jnp.exp(jnp.where(incl, gcol - grow, 0.0)), 0.0)
        a_mats.append(jnp.where(strict, bcol * kkqk[0:CHUNK] * dec, 0.0))
        attns.append(kkqk[CHUNK:2 * CHUNK] * dec)
        vh = qkv_scr[r0:r1, 2 * kdim + lo:2 * kdim + hi]
        rhss.append(jnp.concatenate([vh * bcol, ks[i] * (bcol * egcol)], axis=1))

    pa = jnp.dot(ya_scr[...], _w(w_pa_ref), preferred_element_type=F32)

    p_pairs = [both(eye - a_mats[2 * j], eye - a_mats[2 * j + 1]) for j in range(n_pairs)]
    b_pairs = [both(a_mats[2 * j], a_mats[2 * j + 1]).astype(BF16) for j in range(n_pairs)]
    b_pairs = [_dot(b, _bdiag(left(b), right(b))).astype(BF16) for b in b_pairs]
    n_sq = (CHUNK - 1).bit_length() - 1
    for _ in range(n_sq - 1):
        prods = [_dot(stack(p.astype(BF16), b), _bdiag(left(b), right(b)))
                 for p, b in zip(p_pairs, b_pairs)]
        p_pairs = [p + pr[0:CHUNK] for p, pr in zip(p_pairs, prods)]
        b_pairs = [pr[CHUNK:2 * CHUNK].astype(BF16) for pr in prods]
    p_pairs = [p + _dot(p, _bdiag(left(b), right(b))) for p, b in zip(p_pairs, b_pairs)]
    p_mats = [half(p_pairs[j]).astype(BF16) for j in range(n_pairs) for half in (left, right)]
    uws = [_dot(p, r) for p, r in zip(p_mats, rhss)]
    resids = []
    for a, uw, r in zip(a_mats, uws, rhss):
        a_hi, a_lo = _split(a)
        u_hi, u_lo = _split(uw)
        au = _dot(both(a_hi, a_lo), stack(u_hi, u_hi)) + _dot(a_hi, u_lo)
        resids.append(r - uw - au)
    uws = [uw + _dot(p, rs) for uw, p, rs in zip(uws, p_mats, resids)]

    gated_pa = jax.nn.sigmoid(gates[:, 0:d]) * pa
    gate_b = jax.nn.sigmoid(gates[:, d:2 * d])

    for c in range(n_chunks):
        r0, r1 = c * CHUNK, (c + 1) * CHUNK
        idx = [c * N_HEADS + hd for hd in range(N_HEADS)]
        s_olds = [s_scr[hd] for hd in range(N_HEADS)]
        wq_l = lambda hd, i: stack(uws[i][:, HEAD_DIM:], qs[i] * egcs[c][:, hd:hd + 1]).astype(BF16)
        wq_pairs = [_dot(both(wq_l(hd, idx[hd]), wq_l(hd + 1, idx[hd + 1])),
                         _bdiag(s_olds[hd].astype(BF16), s_olds[hd + 1].astype(BF16)))
                    for hd in range(0, N_HEADS, 2)]
        wqs = [half(wq_pairs[hd // 2]) for hd in range(0, N_HEADS, 2) for half in (left, right)]
        v_news = [uws[i][:, 0:HEAD_DIM] - wqs[hd][0:CHUNK] for hd, i in enumerate(idx)]
        av_pairs = [_dot(both(attns[idx[hd]], attns[idx[hd + 1]]),
                         _bdiag(v_news[hd].astype(BF16), v_news[hd + 1].astype(BF16)))
                    for hd in range(0, N_HEADS, 2)]
        avs = [half(av_pairs[hd // 2]) for hd in range(0, N_HEADS, 2) for half in (left, right)]
        os_ = [wqs[hd][CHUNK:2 * CHUNK] + avs[hd] for hd in range(N_HEADS)]
        for hd, i in enumerate(idx):
            glast = gcs[c][CHUNK - 1:CHUNK, hd:hd + 1]
            kd = ks[i] * jnp.exp(glast - gcs[c][:, hd:hd + 1])
            s_scr[hd] = s_olds[hd] * jnp.exp(glast) + _dot_tn(kd, v_news[hd])
        for hd in range(N_HEADS):
            lo, hi = hd * HEAD_DIM, (hd + 1) * HEAD_DIM
            zb = qkvz[r0:r1, 3 * kdim + lo:3 * kdim + hi]
            yb_scr[r0:r1, lo:hi] = (_rms(os_[hd], gdn_g_ref[...]) * _silu(zb)).astype(BF16)

    @pl.when(t == nt - 1)
    def _():
        s_out_ref[0] = s_scr[...]

    pb = jnp.dot(yb_scr[...], _w(w_pb_ref), preferred_element_type=F32)

    merged = gated_pa + gate_b * pb
    out = jnp.dot(merged.astype(BF16), _w(w_out_ref), preferred_element_type=F32)
    y_ref[0] = x + _rms(out, post_g_ref[...])


def _layer_spec(arr, layer, n_grid):
    shape = (None,) + tuple(arr.shape[1:])
    zeros = (0,) * (arr.ndim - 1)
    if n_grid == 2:
        index_map = lambda b, t: (layer,) + zeros
    else:
        index_map = lambda l, i, j: (l,) + zeros
    return pl.BlockSpec(shape, index_map, pipeline_mode=pl.Buffered(1))


_PROMPT_PARAMS = ("pre_g", "w_uvz", "w_qkvz", "w_gates", "w_ab", "gmlp_g", "w_s", "b_s", "conv_w", "a_log", "dt_bias",
                  "gdn_g", "w_pa", "w_pb", "w_out", "post_g")


def _prompt_layer(x, wts, layer):
    bsz, seq, d = x.shape
    tm = min(TILE_M, seq)
    nt = seq // tm
    kdim = N_HEADS * HEAD_DIM
    conv_dim = 3 * kdim
    params = [wts[name] for name in _PROMPT_PARAMS]
    in_specs = [pl.BlockSpec((1, tm, d), lambda b, t: (b, t, 0))]
    in_specs += [_layer_spec(p, layer, 2) for p in params]
    out_specs = [
        pl.BlockSpec((1, tm, d), lambda b, t: (b, t, 0)),
        pl.BlockSpec((1, N_HEADS, HEAD_DIM, HEAD_DIM), lambda b, t: (b, 0, 0, 0)),
        pl.BlockSpec((1, CONV_W - 1, conv_dim), lambda b, t: (b, 0, 0)),
    ]
    out_shape = [
        jax.ShapeDtypeStruct((bsz, seq, d), F32),
        jax.ShapeDtypeStruct((bsz, N_HEADS, HEAD_DIM, HEAD_DIM), F32),
        jax.ShapeDtypeStruct((bsz, CONV_W - 1, conv_dim), F32),
    ]
    scratch = [
        pltpu.VMEM((N_HEADS, HEAD_DIM, HEAD_DIM), F32),
        pltpu.VMEM((SUBLANES, conv_dim), F32),
        pltpu.VMEM((tm, d), BF16),
        pltpu.VMEM((tm, kdim), BF16),
        pltpu.VMEM((tm, conv_dim), F32),
    ]
    return pl.pallas_call(
        _prompt_kernel,
        grid=(bsz, nt),
        in_specs=in_specs,
        out_specs=out_specs,
        out_shape=out_shape,
        scratch_shapes=scratch,
        compiler_params=pltpu.CompilerParams(
            dimension_semantics=("arbitrary", "arbitrary"),
            vmem_limit_bytes=VMEM_LIMIT),
        name="prompt_layer",
    )(x, *params)


def _sample_kernel(x_ref, s_ref, cs_ref, pre_g_ref, w_uvz_ref, w_qkvz_ref, w_gates_ref, w_ab_ref, gmlp_g_ref, ws00_ref,
                   bs0_ref, convw_ref, alog_ref, dtb_ref, gdn_g_ref, w_pa_ref, w_pb_ref, w_out_ref,
                   post_g_ref,
                   y_ref, s_out_ref, conv_out_ref, vrows_ref,
                   xs_scr, qkvn_scr, zb_scr, sga_pa_scr, sgb_scr, beta_scr, eg_scr, o_scr,
                   q_t, k_t, v_t, beta_t, eg_t, o_t):
    layer = pl.program_id(0)
    i = pl.program_id(1)
    hh = pl.program_id(2)
    n_tiles = pl.num_programs(1)
    n_split = pl.num_programs(2)
    d = x_ref.shape[1]
    kdim = N_HEADS * HEAD_DIM
    heads_per = N_HEADS // HEAD_SPLIT
    wid = heads_per * HEAD_DIM
    rows = SAMPLE_ROWS
    scale = HEAD_DIM ** -0.5

    @pl.when(jnp.logical_and(i == 0, hh == 0))
    def _():
        @pl.when(layer == 0)
        def _():
            xs_scr[...] = x_ref[...]

        x = xs_scr[...]
        h = _rms(x, pre_g_ref[...]).astype(BF16)
        uvz = jnp.dot(h, _w(w_uvz_ref), preferred_element_type=F32)
        vn = _rms(uvz[:, d:2 * d], gmlp_g_ref[...])
        vrows_ref[0] = vn
        mixed = ws00_ref[...] * vn + bs0_ref[...]
        ya = uvz[:, 0:d] * mixed * _silu(uvz[:, 2 * d:3 * d])
        pa = jnp.dot(ya.astype(BF16), _w(w_pa_ref), preferred_element_type=F32)
        qkvz = jnp.dot(h, _w(w_qkvz_ref), preferred_element_type=F32)
        qkvn_scr[...] = qkvz[:, 0:3 * kdim]
        zb_scr[...] = qkvz[:, 3 * kdim:4 * kdim]
        ab = jnp.dot(h, w_ab_ref[...], preferred_element_type=F32)
        eg_scr[...] = jnp.exp(-jnp.exp(alog_ref[...]) * _softplus(ab + dtb_ref[...]))
        beta_scr[...] = jax.nn.sigmoid(ab)
        gates = jnp.dot(h, _w(w_gates_ref),
                        preferred_element_type=F32)
        sga_pa_scr[...] = jax.nn.sigmoid(gates[:, 0:d]) * pa
        sgb_scr[...] = jax.nn.sigmoid(gates[:, d:2 * d])

    b0 = pl.multiple_of(i * rows, rows)

    @pl.when(hh == 0)
    def _():
        cs = cs_ref[0]
        qkv_new = qkvn_scr[pl.ds(b0, rows), :]
        conv = (cs[0] * convw_ref[0:1, :] + cs[1] * convw_ref[1:2, :]
                + cs[2] * convw_ref[2:3, :] + qkv_new * convw_ref[3:4, :])
        conv_out_ref[0, 0] = cs[1]
        conv_out_ref[0, 1] = cs[2]
        conv_out_ref[0, 2] = qkv_new
        qkv = _silu(conv)
        beta_rows = beta_scr[pl.ds(b0, rows), :]
        eg_rows = eg_scr[pl.ds(b0, rows), :]
        for hd in range(N_HEADS):
            lo, hi = hd * HEAD_DIM, (hd + 1) * HEAD_DIM
            half, off = hd // heads_per, (hd % heads_per) * HEAD_DIM
            q_t[half, :, off:off + HEAD_DIM] = _l2norm(qkv[:, lo:hi]) * scale
            k_t[half, :, off:off + HEAD_DIM] = _l2norm(qkv[:, kdim + lo:kdim + hi])
            v_t[half, :, off:off + HEAD_DIM] = qkv[:, 2 * kdim + lo:2 * kdim + hi]
            beta_t[hd] = jnp.broadcast_to(beta_rows[:, N_HEADS + hd:N_HEADS + hd + 1], (rows, HEAD_DIM))
            eg_t[hd] = jnp.broadcast_to(eg_rows[:, hd:hd + 1], (rows, HEAD_DIM))

    rowid = _iota2((SUBLANES, HEAD_DIM), 0)
    states = [(r, j) for r in range(rows) for j in range(heads_per)]
    qv, kv, vv, bv, ev, sv, ksqs = [], [], [], [], [], [], []
    for r, j in states:
        off = j * HEAD_DIM
        q = q_t[hh, r:r + 1, off:off + HEAD_DIM]
        k = k_t[hh, r:r + 1, off:off + HEAD_DIM]
        qv.append(q)
        kv.append(k)
        vv.append(v_t[hh, r:r + 1, off:off + HEAD_DIM])
        bv.append(beta_t[hh * heads_per + j, r:r + 1, :])
        ev.append(eg_t[hh * heads_per + j, r:r + 1, :])
        s = s_ref[0, r, j]
        sv.append(s)
        kq8 = jnp.where(rowid == 0, jnp.broadcast_to(k, (SUBLANES, HEAD_DIM)),
                        jnp.where(rowid == 1, jnp.broadcast_to(q, (SUBLANES, HEAD_DIM)), 0.0))
        ksqs.append(_dot(kq8, s))
    k8s, v8s = [], []
    for n, (r, j) in enumerate(states):
        v_new = bv[n] * (vv[n] - ev[n] * ksqs[n][0:1, :])
        qk = jnp.sum(qv[n] * kv[n], axis=-1, keepdims=True)
        o_t[r:r + 1, j * HEAD_DIM:(j + 1) * HEAD_DIM] = ev[n] * ksqs[n][1:2, :] + qk * v_new
        k_hi, k_lo = _split(kv[n])
        v_hi, v_lo = _split(v_new)
        bc = lambda a: jnp.broadcast_to(a.astype(F32), (SUBLANES, HEAD_DIM))
        k8s.append(jnp.where(rowid <= 1, bc(k_hi), jnp.where(rowid == 2, bc(k_lo), 0.0)))
        v8s.append(jnp.where(rowid == 0, bc(v_hi), jnp.where(rowid == 1, bc(v_lo),
                                                              jnp.where(rowid == 2, bc(v_hi), 0.0))))
    for n in range(0, len(states), 2):
        (r, j), (r2, j2) = states[n], states[n + 1]
        upd = _dot_tn(jnp.concatenate([k8s[n], k8s[n + 1]], axis=0).astype(BF16),
                      _bdiag(v8s[n].astype(BF16), v8s[n + 1].astype(BF16)))
        s_out_ref[0, r, j] = sv[n] * ev[n] + upd[:, 0:HEAD_DIM]
        s_out_ref[0, r2, j2] = sv[n + 1] * ev[n + 1] + upd[:, HEAD_DIM:2 * HEAD_DIM]
    o_scr[hh, pl.ds(b0, rows), :] = o_t[...]

    @pl.when(jnp.logical_and(i == n_tiles - 1, hh == n_split - 1))
    def _():
        pb = jnp.zeros((x_ref.shape[0], d), F32)
        for hd in range(N_HEADS):
            lo, hi = hd * HEAD_DIM, (hd + 1) * HEAD_DIM
            half, off = hd // heads_per, (hd % heads_per) * HEAD_DIM
            o = o_scr[half, :, off:off + HEAD_DIM]
            yb = (_rms(o, gdn_g_ref[...]) * _silu(zb_scr[:, lo:hi])).astype(BF16)
            pb = pb + jnp.dot(yb, pltpu.bitcast(w_pb_ref[lo // 2:hi // 2, :], BF16),
                              preferred_element_type=F32)
        merged = sga_pa_scr[...] + sgb_scr[...] * pb
        out = jnp.dot(merged.astype(BF16), _w(w_out_ref), preferred_element_type=F32)
        y = xs_scr[...] + _rms(out, post_g_ref[...])
        xs_scr[...] = y
        y_ref[...] = y


_SAMPLE_PARAMS = ("pre_g", "w_uvz", "w_qkvz", "w_gates", "w_ab", "gmlp_g", "ws00", "bs0", "conv_w", "a_log", "dt_bias",
                  "gdn_g", "w_pa", "w_pb", "w_out", "post_g")


def _sample_path(x, state_gdn, state_conv, wts):
    n, d = x.shape
    depth = state_gdn.shape[0]
    kdim = N_HEADS * HEAD_DIM
    conv_dim = 3 * kdim
    heads_per = N_HEADS // HEAD_SPLIT
    rows = SAMPLE_ROWS
    params = [wts[name] for name in _SAMPLE_PARAMS]
    s_spec = pl.BlockSpec((1, rows, heads_per, HEAD_DIM, HEAD_DIM), lambda l, i, j: (l, i, j, 0, 0))
    c_spec = pl.BlockSpec((1, CONV_W - 1, rows, conv_dim), lambda l, i, j: (l, 0, i, 0))
    in_specs = [pl.BlockSpec((n, d), lambda l, i, j: (0, 0)), s_spec, c_spec]
    in_specs += [_layer_spec(p, None, 3) for p in params]
    out_specs = [
        pl.BlockSpec((n, d), lambda l, i, j: (0, 0)),
        s_spec,
        c_spec,
        pl.BlockSpec((1, n, d), lambda l, i, j: (l, 0, 0)),
    ]
    out_shape = [
        jax.ShapeDtypeStruct((n, d), F32),
        jax.ShapeDtypeStruct(state_gdn.shape, F32),
        jax.ShapeDtypeStruct(state_conv.shape, F32),
        jax.ShapeDtypeStruct((depth, n, d), F32),
    ]
    scratch = [
        pltpu.VMEM((n, d), F32),
        pltpu.VMEM((n, conv_dim), F32),
        pltpu.VMEM((n, kdim), F32),
        pltpu.VMEM((n, d), F32),
        pltpu.VMEM((n, d), F32),
        pltpu.VMEM((n, LANES), F32),
        pltpu.VMEM((n, LANES), F32),
        pltpu.VMEM((HEAD_SPLIT, n, heads_per * HEAD_DIM), F32),
        pltpu.VMEM((HEAD_SPLIT, rows, heads_per * HEAD_DIM), F32),
        pltpu.VMEM((HEAD_SPLIT, rows, heads_per * HEAD_DIM), F32),
        pltpu.VMEM((HEAD_SPLIT, rows, heads_per * HEAD_DIM), F32),
        pltpu.VMEM((N_HEADS, rows, HEAD_DIM), F32),
        pltpu.VMEM((N_HEADS, rows, HEAD_DIM), F32),
        pltpu.VMEM((rows, heads_per * HEAD_DIM), F32),
    ]
    return pl.pallas_call(
        _sample_kernel,
        grid=(depth, n // rows, HEAD_SPLIT),
        in_specs=in_specs,
        out_specs=out_specs,
        out_shape=out_shape,
        scratch_shapes=scratch,
        compiler_params=pltpu.CompilerParams(
            dimension_semantics=("arbitrary", "arbitrary", "arbitrary"),
            vmem_limit_bytes=VMEM_LIMIT),
        name="sample_path",
    )(x, state_gdn, state_conv, *params)


def _to_kernel_order(x):
    b, l, d = x.shape
    return x.reshape(b, l // CHUNK, SUBLANES, VROWS, d).swapaxes(2, 3).reshape(b, l, d)


def _from_kernel_order(x):
    b, l, d = x.shape
    return x.reshape(b, l // CHUNK, VROWS, SUBLANES, d).swapaxes(2, 3).reshape(b, l, d)


def _pack(x):
    return pltpu.bitcast(x.astype(BF16), jnp.uint32)


def _cast_w_in_kernel(wt_ref, uvz_ref, qkvz_ref, gates_ref, ab_ref):
    n_uvz, n_qkvz, n_gates = uvz_ref.shape[2], qkvz_ref.shape[2], gates_ref.shape[2]
    ab0 = n_uvz + n_qkvz
    n_ab = 2 * N_HEADS
    uvz_ref[0] = _pack(wt_ref[0, 0:n_uvz, :].T)
    qkvz_ref[0] = _pack(wt_ref[0, n_uvz:ab0, :].T)
    gates_ref[0] = _pack(wt_ref[0, ab0 + n_ab:ab0 + n_ab + n_gates, :].T)
    ab_blk = wt_ref[0, ab0:ab0 + LANES, :].T
    lane = _iota2(ab_blk.shape, 1)
    ab_ref[0] = jnp.where(lane < n_ab, ab_blk, 0.0).astype(BF16)


def _cast_square_kernel(a_ref, b_ref, c_ref, oa_ref, ob_ref, oc_ref):
    oa_ref[0] = _pack(a_ref[0])
    ob_ref[0] = _pack(b_ref[0])
    oc_ref[0] = _pack(c_ref[0])


def _cast_weights(w_in, w_proj_a, w_proj_b, w_out):
    depth, d, in_dim = w_in.shape
    kdim = N_HEADS * HEAD_DIM
    rb = 256
    grid = (depth, d // rb)
    u32 = lambda n: jax.ShapeDtypeStruct((depth, d // 2, n), jnp.uint32)
    out_block = lambda n: pl.BlockSpec((1, rb // 2, n), lambda l, i: (l, i, 0))
    params = pltpu.CompilerParams(dimension_semantics=("arbitrary", "arbitrary"),
                                  vmem_limit_bytes=VMEM_LIMIT)
    uvz, qkvz, gates, ab = pl.pallas_call(
        _cast_w_in_kernel,
        grid=grid,
        in_specs=[pl.BlockSpec((1, in_dim, rb), lambda l, i: (l, 0, i))],
        out_specs=[out_block(3 * d), out_block(4 * kdim), out_block(2 * d),
                   pl.BlockSpec((1, rb, LANES), lambda l, i: (l, i, 0))],
        out_shape=[u32(3 * d), u32(4 * kdim), u32(2 * d),
                   jax.ShapeDtypeStruct((depth, d, LANES), BF16)],
        compiler_params=params,
        name="cast_w_in",
    )(jnp.swapaxes(w_in, 1, 2))
    sq_in = pl.BlockSpec((1, rb, d), lambda l, i: (l, i, 0))
    pa, pb, out = pl.pallas_call(
        _cast_square_kernel,
        grid=grid,
        in_specs=[sq_in, sq_in, sq_in],
        out_specs=[out_block(d)] * 3,
        out_shape=[u32(d)] * 3,
        compiler_params=params,
        name="cast_w_square",
    )(w_proj_a, w_proj_b, w_out)
    return {"w_uvz": uvz, "w_qkvz": qkvz, "w_gates": gates, "w_ab": ab,
            "w_pa": pa, "w_pb": pb, "w_out": out}


def _prepare_weights(pre_norm, w_in, gmlp_norm, w_spatial, b_spatial, conv_w, a_log, dt_bias,
                     gdn_norm, w_proj_a, w_proj_b, w_out, post_norm):
    depth = w_in.shape[0]
    pad_lanes = lambda a: jnp.pad(a, ((0, 0), (0, LANES - N_HEADS)))[:, None, :]
    rows = jnp.arange(CHUNK)
    tok = (rows % SUBLANES) * VROWS + rows // SUBLANES
    return {
        **_cast_weights(w_in, w_proj_a, w_proj_b, w_out),
        "pre_g": pre_norm[:, None, :],
        "gmlp_g": gmlp_norm[:, None, :],
        "w_s": w_spatial[:, :, tok, :][:, :, :, tok],
        "b_s": jnp.broadcast_to(b_spatial[:, :, tok, None], (depth, N_HEADS, CHUNK, HEAD_DIM)),
        "ws00": jnp.repeat(w_spatial[:, :, 0, 0], HEAD_DIM, axis=1)[:, None, :],
        "bs0": jnp.repeat(b_spatial[:, :, 0], HEAD_DIM, axis=1)[:, None, :],
        "conv_w": conv_w,
        "a_log": pad_lanes(a_log),
        "dt_bias": pad_lanes(dt_bias),
        "gdn_g": gdn_norm[:, None, :],
        "post_g": post_norm[:, None, :],
    }


def kernel(x_prompt, x_sample, state_gdn, state_conv, pre_norm, w_in, gmlp_norm, w_spatial, b_spatial, conv_w, a_log, dt_bias, gdn_norm, w_proj_a, w_proj_b, w_out, post_norm):
    depth = w_in.shape[0]
    wts = _prepare_weights(pre_norm, w_in, gmlp_norm, w_spatial, b_spatial, conv_w, a_log, dt_bias,
                           gdn_norm, w_proj_a, w_proj_b, w_out, post_norm)
    xp = _to_kernel_order(x_prompt)
    gdn_p, conv_p = [], []
    for l in range(depth):
        xp, sg_p, cb_p = _prompt_layer(xp, wts, l)
        gdn_p.append(sg_p)
        conv_p.append(cb_p)
    xp = _from_kernel_order(xp)
    ys, gdn_s, conv_s, vrows_s = _sample_path(x_sample[:, 0, :], state_gdn,
                                              jnp.swapaxes(state_conv, 1, 2), wts)
    return (xp, ys[:, None, :], jnp.stack(gdn_p), jnp.stack(conv_p), gdn_s,
            jnp.swapaxes(conv_s, 1, 2), vrows_s[:, :, None, :])
```

```python
import jax
import jax.numpy as jnp
from jax import lax
from jax.experimental import pallas as pl
from jax.experimental.pallas import tpu as pltpu

F32 = jnp.float32
BF16 = jnp.bfloat16
EPS = 1e-6

LANES = 128
SUBLANES = 8
HEAD_DIM = 128
N_HEADS = 8
CONV_W = 4
CHUNK = 128
TILE_M = 256
VROWS = CHUNK // SUBLANES
SAMPLE_ROWS = 8
HEAD_SPLIT = 1
VMEM_LIMIT = 56 * 1024 * 1024


def _dot(a, b):
    return jnp.dot(a.astype(BF16), b.astype(BF16), preferred_element_type=F32)


def _dot_nt(a, b):
    return lax.dot_general(a.astype(BF16), b.astype(BF16), (((1,), (1,)), ((), ())),
                           preferred_element_type=F32)


def _dot_tn(a, b):
    return lax.dot_general(a.astype(BF16), b.astype(BF16), (((0,), (0,)), ((), ())),
                           preferred_element_type=F32)


def _bdiag(a, b):
    z = jnp.zeros_like(a)
    return jnp.concatenate([jnp.concatenate([a, z], axis=1), jnp.concatenate([z, b], axis=1)], axis=0)


def _w(w_ref):
    return pltpu.bitcast(w_ref[...], BF16)


def _split(a):
    hi = a.astype(BF16)
    lo = (a - hi.astype(F32)).astype(BF16)
    return hi, lo


def _dot_exact(a, b):
    return jnp.dot(a, b, preferred_element_type=F32, precision=lax.Precision.HIGHEST)


def _rms(x, g):
    return x * lax.rsqrt(jnp.mean(x * x, axis=-1, keepdims=True) + EPS) * g


def _silu(x):
    return x * jax.nn.sigmoid(x)


def _softplus(x):
    return jnp.maximum(x, 0.0) + jnp.log(1.0 + jnp.exp(-jnp.abs(x)))


def _l2norm(x):
    return x * lax.rsqrt(jnp.sum(x * x, axis=-1, keepdims=True) + EPS)


def _iota2(shape, dim):
    return lax.broadcasted_iota(jnp.int32, shape, dim)


def _alternate(*stage_gens):
    pending = list(stage_gens)
    while pending:
        for gen in list(pending):
            try:
                next(gen)
            except StopIteration:
                pending.remove(gen)


def _prompt_kernel(x_ref, pre_g_ref, w_uvz_ref, w_qkvz_ref, w_gates_ref, w_ab_ref, gmlp_g_ref, ws_ref, bs_ref,
                   convw_ref, alog_ref, dtb_ref, gdn_g_ref, w_pa_ref, w_pb_ref, w_out_ref,
                   post_g_ref,
                   y_ref, s_out_ref, conv_out_ref,
                   s_scr, carry_scr, ya_scr, yb_scr, qkv_scr):
    t = pl.program_id(1)
    nt = pl.num_programs(1)
    tm = x_ref.shape[1]
    d = x_ref.shape[2]
    n_chunks = tm // CHUNK
    kdim = N_HEADS * HEAD_DIM

    @pl.when(t == 0)
    def _():
        s_scr[...] = jnp.zeros_like(s_scr)
        carry_scr[...] = jnp.zeros_like(carry_scr)

    x = x_ref[0]
    h = _rms(x, pre_g_ref[...]).astype(BF16)

    row = _iota2((CHUNK, CHUNK), 0)
    col = _iota2((CHUNK, CHUNK), 1)
    token_of = lambda r: (r % SUBLANES) * VROWS + r // SUBLANES
    row_tok, col_tok = token_of(row), token_of(col)
    incl = row_tok >= col_tok
    strict = row_tok > col_tok

    qkvz = jnp.dot(h, _w(w_qkvz_ref), preferred_element_type=F32)
    uvz = jnp.dot(h, _w(w_uvz_ref), preferred_element_type=F32)
    rowid = _iota2((SUBLANES, kdim), 0)
    n_tail = CONV_W - 1
    prev = [carry_scr[j:j + 1, :] for j in range(n_tail)]

    def conv_gen(c):
        r0 = c * CHUNK
        last_rows = [[] for _ in range(n_tail)]
        for lo in range(0, 3 * kdim, kdim):
            pre = qkvz[r0:r0 + CHUNK, lo:lo + kdim]
            wrapped = []
            for j in range(n_tail):
                blk = pre[(VROWS - n_tail + j) * SUBLANES:(VROWS - n_tail + j + 1) * SUBLANES, :]
                shifted = pltpu.roll(blk, 1, axis=0)
                wrapped.append(jnp.where(rowid == 0, jnp.broadcast_to(prev[j][:, lo:lo + kdim], blk.shape),
                                         shifted))
                last_rows[j].append(blk[SUBLANES - 1:SUBLANES, :])
            conv = pre * convw_ref[CONV_W - 1:CONV_W, lo:lo + kdim]
            for k in range(1, CONV_W):
                back_k = jnp.concatenate(wrapped[n_tail - k:] + [pre[0:(VROWS - k) * SUBLANES, :]], axis=0)
                conv = conv + back_k * convw_ref[CONV_W - 1 - k:CONV_W - k, lo:lo + kdim]
            qkv_scr[r0:r0 + CHUNK, lo:lo + kdim] = _silu(conv)
            yield
        prev[:] = [jnp.concatenate(rows, axis=1) for rows in last_rows]

    gates = jnp.dot(h, _w(w_gates_ref), preferred_element_type=F32)
    ab = jnp.dot(h, w_ab_ref[...], preferred_element_type=F32)
    g_all = -jnp.exp(alog_ref[...]) * _softplus(ab + dtb_ref[...])
    beta_all = jax.nn.sigmoid(ab)
    g_all_t = g_all.T
    ltri = incl.astype(F32)
    utri = (row_tok <= col_tok).astype(F32)
    eye = (row == col).astype(F32)
    scale = HEAD_DIM ** -0.5

    pair_heads = range(0, N_HEADS, 2)
    left = lambda a: a[:, 0:HEAD_DIM]
    right = lambda a: a[:, HEAD_DIM:2 * HEAD_DIM]
    both = lambda a, b: jnp.concatenate([a, b], axis=1)
    stack = lambda a, b: jnp.concatenate([a, b], axis=0)
    unpair = lambda pairs: [half(p) for p in pairs for half in (left, right)]
    st = [dict() for _ in range(n_chunks)]
    branch_a = {}

    def front_gen(c):
        r0, r1 = c * CHUNK, (c + 1) * CHUNK
        gc = _dot_exact(ltri, g_all[r0:r1, :])
        gct = _dot_exact(g_all_t[0:N_HEADS, r0:r1], utri)
        egc = jnp.exp(gc)
        qs, ks = [], []
        for hd in range(N_HEADS):
            lo, hi = hd * HEAD_DIM, (hd + 1) * HEAD_DIM
            qs.append(_l2norm(qkv_scr[r0:r1, lo:hi]) * scale)
            ks.append(_l2norm(qkv_scr[r0:r1, kdim + lo:kdim + hi]))
            if hd % 2:
                yield
        kkqks = []
        for hd in pair_heads:
            k1, k2 = ks[hd].astype(BF16), ks[hd + 1].astype(BF16)
            lhs = both(stack(k1, qs[hd].astype(BF16)), stack(k2, qs[hd + 1].astype(BF16)))
            kkqks.append(_dot_nt(lhs, _bdiag(k1, k2)))
        kkqks = unpair(kkqks)
        yield
        a_mats, attns, rhss = [], [], []
        for hd in range(N_HEADS):
            lo, hi = hd * HEAD_DIM, (hd + 1) * HEAD_DIM
            gcol = gc[:, hd:hd + 1]
            grow = gct[hd:hd + 1, :]
            bcol = beta_all[r0:r1, N_HEADS + hd:N_HEADS + hd + 1]
            dec = jnp.where(incl, jnp.exp(jnp.where(incl, gcol - grow, 0.0)), 0.0)
            a_mats.append(jnp.where(strict, bcol * kkqks[hd][0:CHUNK] * dec, 0.0))
            attns.append(kkqks[hd][CHUNK:2 * CHUNK] * dec)
            vh = qkv_scr[r0:r1, 2 * kdim + lo:2 * kdim + hi]
            rhss.append(jnp.concatenate([vh * bcol, ks[hd] * (bcol * egc[:, hd:hd + 1])], axis=1))
            if hd % 2:
                yield
        st[c].update(gc=gc, egc=egc, qs=qs, ks=ks, a_mats=a_mats, attns=attns, rhss=rhss)

    def solve_gen(c):
        a_mats, rhss = st[c]["a_mats"], st[c]["rhss"]
        p_pairs = [both(eye - a_mats[hd], eye - a_mats[hd + 1]) for hd in pair_heads]
        b_pairs = [both(a_mats[hd], a_mats[hd + 1]).astype(BF16) for hd in pair_heads]
        b_pairs = [_dot(b, _bdiag(left(b), right(b))).astype(BF16) for b in b_pairs]
        yield
        n_sq = (CHUNK - 1).bit_length() - 1
        for _ in range(n_sq - 1):
            prods = [_dot(stack(p.astype(BF16), b), _bdiag(left(b), right(b)))
                     for p, b in zip(p_pairs, b_pairs)]
            p_pairs = [p + pr[0:CHUNK] for p, pr in zip(p_pairs, prods)]
            b_pairs = [pr[CHUNK:2 * CHUNK].astype(BF16) for pr in prods]
            yield
        p_pairs = [p + _dot(p, _bdiag(left(b), right(b))) for p, b in zip(p_pairs, b_pairs)]
        p_mats = [p.astype(BF16) for p in unpair(p_pairs)]
        yield
        uws = [_dot(p, r) for p, r in zip(p_mats, rhss)]
        yield
        resids = [r - uw - _dot(a, uw) for a, uw, r in zip(a_mats, uws, rhss)]
        yield
        st[c]["uws"] = [uw + _dot(p, rs) for uw, p, rs in zip(uws, p_mats, resids)]

    def recur_gen(c):
        r0, r1 = c * CHUNK, (c + 1) * CHUNK
        gc, egc, qs, ks, attns, uws = (st[c][n] for n in ("gc", "egc", "qs", "ks", "attns", "uws"))
        s_olds = [s_scr[hd] for hd in range(N_HEADS)]
        wq_l = lambda hd: stack(uws[hd][:, HEAD_DIM:], qs[hd] * egc[:, hd:hd + 1]).astype(BF16)
        wqs = unpair([_dot(both(wq_l(hd), wq_l(hd + 1)),
                           _bdiag(s_olds[hd].astype(BF16), s_olds[hd + 1].astype(BF16)))
                      for hd in pair_heads])
        yield
        v_news = [uws[hd][:, 0:HEAD_DIM] - wqs[hd][0:CHUNK] for hd in range(N_HEADS)]
        avs = unpair([_dot(both(attns[hd], attns[hd + 1]),
                           _bdiag(v_news[hd].astype(BF16), v_news[hd + 1].astype(BF16)))
                      for hd in pair_heads])
        for hd in range(N_HEADS):
            glast = gc[CHUNK - 1:CHUNK, hd:hd + 1]
            kd = ks[hd] * jnp.exp(glast - gc[:, hd:hd + 1])
            s_scr[hd] = s_olds[hd] * jnp.exp(glast) + _dot_tn(kd, v_news[hd])
        yield
        for hd in range(N_HEADS):
            lo, hi = hd * HEAD_DIM, (hd + 1) * HEAD_DIM
            o = wqs[hd][CHUNK:2 * CHUNK] + avs[hd]
            zb = qkvz[r0:r1, 3 * kdim + lo:3 * kdim + hi]
            yb_scr[r0:r1, lo:hi] = (_rms(o, gdn_g_ref[...]) * _silu(zb)).astype(BF16)
            if hd % 2:
                yield

    def branch_a_gen():
        vn = _rms(uvz[:, d:2 * d], gmlp_g_ref[...])
        vnb = vn.astype(BF16)
        yield
        for g in range(N_HEADS):
            wsg = jnp.where(incl, ws_ref[g], 0.0).astype(BF16)
            lo, hi = g * HEAD_DIM, (g + 1) * HEAD_DIM
            mixed_all = jnp.dot(
                wsg, jnp.concatenate([vnb[c * CHUNK:(c + 1) * CHUNK, lo:hi] for c in range(n_chunks)], axis=1),
                preferred_element_type=F32)
            for c in range(n_chunks):
                r0, r1 = c * CHUNK, (c + 1) * CHUNK
                mixed = mixed_all[:, c * HEAD_DIM:(c + 1) * HEAD_DIM] + bs_ref[g]
                u = uvz[r0:r1, lo:hi]
                z = uvz[r0:r1, 2 * d + lo:2 * d + hi]
                ya_scr[r0:r1, lo:hi] = (u * mixed * _silu(z)).astype(BF16)
            yield
        pa = jnp.dot(ya_scr[...], _w(w_pa_ref), preferred_element_type=F32)
        yield
        branch_a["gated_pa"] = jax.nn.sigmoid(gates[:, 0:d]) * pa
        yield
        branch_a["gate_b"] = jax.nn.sigmoid(gates[:, d:2 * d])

    def chain(*gens):
        for g in gens:
            yield from g

    _alternate(chain(conv_gen(0), front_gen(0)))
    for c in range(n_chunks):
        fillers = []
        if c + 1 < n_chunks:
            fillers.append(chain(conv_gen(c + 1), front_gen(c + 1)))
        if c > 0:
            fillers.append(recur_gen(c - 1))
        if c == n_chunks - 1:
            fillers.append(branch_a_gen())
        _alternate(solve_gen(c), chain(*fillers))
    _alternate(recur_gen(n_chunks - 1))
    gated_pa, gate_b = branch_a["gated_pa"], branch_a["gate_b"]

    tail = jnp.concatenate(prev, axis=0)
    carry_scr[0:n_tail, :] = tail

    @pl.when(t == nt - 1)
    def _():
        conv_out_ref[0] = tail

    @pl.when(t == nt - 1)
    def _():
        s_out_ref[0] = s_scr[...]

    pb = jnp.dot(yb_scr[...], _w(w_pb_ref), preferred_element_type=F32)

    merged = gated_pa + gate_b * pb
    out = jnp.dot(merged.astype(BF16), _w(w_out_ref), preferred_element_type=F32)
    y_ref[0] = x + _rms(out, post_g_ref[...])


def _layer_spec(arr, layer, n_grid):
    shape = (None,) + tuple(arr.shape[1:])
    zeros = (0,) * (arr.ndim - 1)
    if n_grid == 2:
        index_map = lambda b, t: (layer,) + zeros
    else:
        index_map = lambda l, i, j: (l,) + zeros
    return pl.BlockSpec(shape, index_map, pipeline_mode=pl.Buffered(1))


_PROMPT_PARAMS = ("pre_g", "w_uvz", "w_qkvz", "w_gates", "w_ab", "gmlp_g", "w_s", "b_s", "conv_w", "a_log", "dt_bias",
                  "gdn_g", "w_pa", "w_pb", "w_out", "post_g")


def _prompt_layer(x, wts, layer):
    bsz, seq, d = x.shape
    tm = min(TILE_M, seq)
    nt = seq // tm
    kdim = N_HEADS * HEAD_DIM
    conv_dim = 3 * kdim
    params = [wts[name] for name in _PROMPT_PARAMS]
    in_specs = [pl.BlockSpec((1, tm, d), lambda b, t: (b, t, 0))]
    in_specs += [_layer_spec(p, layer, 2) for p in params]
    out_specs = [
        pl.BlockSpec((1, tm, d), lambda b, t: (b, t, 0)),
        pl.BlockSpec((1, N_HEADS, HEAD_DIM, HEAD_DIM), lambda b, t: (b, 0, 0, 0)),
        pl.BlockSpec((1, CONV_W - 1, conv_dim), lambda b, t: (b, 0, 0)),
    ]
    out_shape = [
        jax.ShapeDtypeStruct((bsz, seq, d), F32),
        jax.ShapeDtypeStruct((bsz, N_HEADS, HEAD_DIM, HEAD_DIM), F32),
        jax.ShapeDtypeStruct((bsz, CONV_W - 1, conv_dim), F32),
    ]
    scratch = [
        pltpu.VMEM((N_HEADS, HEAD_DIM, HEAD_DIM), F32),
        pltpu.VMEM((SUBLANES, conv_dim), F32),
        pltpu.VMEM((tm, d), BF16),
        pltpu.VMEM((tm, kdim), BF16),
        pltpu.VMEM((tm, conv_dim), F32),
    ]
    return pl.pallas_call(
        _prompt_kernel,
        grid=(bsz, nt),
        in_specs=in_specs,
        out_specs=out_specs,
        out_shape=out_shape,
        scratch_shapes=scratch,
        compiler_params=pltpu.CompilerParams(
            dimension_semantics=("arbitrary", "arbitrary"),
            vmem_limit_bytes=VMEM_LIMIT),
        name="prompt_layer",
    )(x, *params)


def _sample_kernel(x_ref, s_ref, cs_ref, pre_g_ref, w_uvz_ref, w_qkvz_ref, w_gates_ref, w_ab_ref, gmlp_g_ref, ws00_ref,
                   bs0_ref, convw_ref, alog_ref, dtb_ref, gdn_g_ref, w_pa_ref, w_pb_ref, w_out_ref,
                   post_g_ref,
                   y_ref, s_out_ref, conv_out_ref, vrows_ref,
                   xs_scr, qkvn_scr, zb_scr, sga_pa_scr, sgb_scr, beta_scr, eg_scr, o_scr,
                   q_t, k_t, v_t, beta_t, eg_t, o_t):
    layer = pl.program_id(0)
    i = pl.program_id(1)
    hh = pl.program_id(2)
    n_tiles = pl.num_programs(1)
    n_split = pl.num_programs(2)
    d = x_ref.shape[1]
    kdim = N_HEADS * HEAD_DIM
    heads_per = N_HEADS // HEAD_SPLIT
    wid = heads_per * HEAD_DIM
    rows = SAMPLE_ROWS
    scale = HEAD_DIM ** -0.5

    @pl.when(jnp.logical_and(i == 0, hh == 0))
    def _():
        @pl.when(layer == 0)
        def _():
            xs_scr[...] = x_ref[...]

        x = xs_scr[...]
        h = _rms(x, pre_g_ref[...]).astype(BF16)
        uvz = jnp.dot(h, _w(w_uvz_ref), preferred_element_type=F32)
        vn = _rms(uvz[:, d:2 * d], gmlp_g_ref[...])
        vrows_ref[0] = vn
        mixed = ws00_ref[...] * vn + bs0_ref[...]
        ya = uvz[:, 0:d] * mixed * _silu(uvz[:, 2 * d:3 * d])
        pa = jnp.dot(ya.astype(BF16), _w(w_pa_ref), preferred_element_type=F32)
        qkvz = jnp.dot(h, _w(w_qkvz_ref), preferred_element_type=F32)
        qkvn_scr[...] = qkvz[:, 0:3 * kdim]
        zb_scr[...] = qkvz[:, 3 * kdim:4 * kdim]
        ab = jnp.dot(h, w_ab_ref[...], preferred_element_type=F32)
        eg_scr[...] = jnp.exp(-jnp.exp(alog_ref[...]) * _softplus(ab + dtb_ref[...]))
        beta_scr[...] = jax.nn.sigmoid(ab)
        gates = jnp.dot(h, _w(w_gates_ref),
                        preferred_element_type=F32)
        sga_pa_scr[...] = jax.nn.sigmoid(gates[:, 0:d]) * pa
        sgb_scr[...] = jax.nn.sigmoid(gates[:, d:2 * d])

    b0 = pl.multiple_of(i * rows, rows)

    @pl.when(hh == 0)
    def _():
        cs = cs_ref[0]
        qkv_new = qkvn_scr[pl.ds(b0, rows), :]
        conv = (cs[0] * convw_ref[0:1, :] + cs[1] * convw_ref[1:2, :]
                + cs[2] * convw_ref[2:3, :] + qkv_new * convw_ref[3:4, :])
        conv_out_ref[0, 0] = cs[1]
        conv_out_ref[0, 1] = cs[2]
        conv_out_ref[0, 2] = qkv_new
        qkv = _silu(conv)
        beta_rows = beta_scr[pl.ds(b0, rows), :]
        eg_rows = eg_scr[pl.ds(b0, rows), :]
        for hd in range(N_HEADS):
            lo, hi = hd * HEAD_DIM, (hd + 1) * HEAD_DIM
            half, off = hd // heads_per, (hd % heads_per) * HEAD_DIM
            q_t[half, :, off:off + HEAD_DIM] = _l2norm(qkv[:, lo:hi]) * scale
            k_t[half, :, off:off + HEAD_DIM] = _l2norm(qkv[:, kdim + lo:kdim + hi])
            v_t[half, :, off:off + HEAD_DIM] = qkv[:, 2 * kdim + lo:2 * kdim + hi]
            beta_t[hd] = jnp.broadcast_to(beta_rows[:, N_HEADS + hd:N_HEADS + hd + 1], (rows, HEAD_DIM))
            eg_t[hd] = jnp.broadcast_to(eg_rows[:, hd:hd + 1], (rows, HEAD_DIM))

    rowid = _iota2((SUBLANES, HEAD_DIM), 0)
    states = [(r, j) for r in range(rows) for j in range(heads_per)]
    qv, kv, vv, bv, ev, sv, ksqs = [], [], [], [], [], [], []
    for r, j in states:
        off = j * HEAD_DIM
        q = q_t[hh, r:r + 1, off:off + HEAD_DIM]
        k = k_t[hh, r:r + 1, off:off + HEAD_DIM]
        qv.append(q)
        kv.append(k)
        vv.append(v_t[hh, r:r + 1, off:off + HEAD_DIM])
        bv.append(beta_t[hh * heads_per + j, r:r + 1, :])
        ev.append(eg_t[hh * heads_per + j, r:r + 1, :])
        s = s_ref[0, r, j]
        sv.append(s)
        kq8 = jnp.where(rowid == 0, jnp.broadcast_to(k, (SUBLANES, HEAD_DIM)),
                        jnp.where(rowid == 1, jnp.broadcast_to(q, (SUBLANES, HEAD_DIM)), 0.0))
        ksqs.append(_dot(kq8, s))
    k8s, v8s = [], []
    for n, (r, j) in enumerate(states):
        v_new = bv[n] * (vv[n] - ev[n] * ksqs[n][0:1, :])
        qk = jnp.sum(qv[n] * kv[n], axis=-1, keepdims=True)
        o_t[r:r + 1, j * HEAD_DIM:(j + 1) * HEAD_DIM] = ev[n] * ksqs[n][1:2, :] + qk * v_new
        k_hi, k_lo = _split(kv[n])
        v_hi, v_lo = _split(v_new)
        bc = lambda a: jnp.broadcast_to(a.astype(F32), (SUBLANES, HEAD_DIM))
        k8s.append(jnp.where(rowid <= 1, bc(k_hi), jnp.where(rowid == 2, bc(k_lo), 0.0)))
        v8s.append(jnp.where(rowid == 0, bc(v_hi), jnp.where(rowid == 1, bc(v_lo),
                                                              jnp.where(rowid == 2, bc(v_hi), 0.0))))
    for n in range(0, len(states), 2):
        (r, j), (r2, j2) = states[n], states[n + 1]
        upd = _dot_tn(jnp.concatenate([k8s[n], k8s[n + 1]], axis=0).astype(BF16),
                      _bdiag(v8s[n].astype(BF16), v8s[n + 1].astype(BF16)))
        s_out_ref[0, r, j] = sv[n] * ev[n] + upd[:, 0:HEAD_DIM]
        s_out_ref[0, r2, j2] = sv[n + 1] * ev[n + 1] + upd[:, HEAD_DIM:2 * HEAD_DIM]
    o_scr[hh, pl.ds(b0, rows), :] = o_t[...]

    @pl.when(jnp.logical_and(i == n_tiles - 1, hh == n_split - 1))
    def _():
        pb = jnp.zeros((x_ref.shape[0], d), F32)
        for hd in range(N_HEADS):
            lo, hi = hd * HEAD_DIM, (hd + 1) * HEAD_DIM
            half, off = hd // heads_per, (hd % heads_per) * HEAD_DIM
            o = o_scr[half, :, off:off + HEAD_DIM]
            yb = (_rms(o, gdn_g_ref[...]) * _silu(zb_scr[:, lo:hi])).astype(BF16)
            pb = pb + jnp.dot(yb, pltpu.bitcast(w_pb_ref[lo // 2:hi // 2, :], BF16),
                              preferred_element_type=F32)
        merged = sga_pa_scr[...] + sgb_scr[...] * pb
        out = jnp.dot(merged.astype(BF16), _w(w_out_ref), preferred_element_type=F32)
        y = xs_scr[...] + _rms(out, post_g_ref[...])
        xs_scr[...] = y
        y_ref[...] = y


_SAMPLE_PARAMS = ("pre_g", "w_uvz", "w_qkvz", "w_gates", "w_ab", "gmlp_g", "ws00", "bs0", "conv_w", "a_log", "dt_bias",
                  "gdn_g", "w_pa", "w_pb", "w_out", "post_g")


def _sample_path(x, state_gdn, state_conv, wts):
    n, d = x.shape
    depth = state_gdn.shape[0]
    kdim = N_HEADS * HEAD_DIM
    conv_dim = 3 * kdim
    heads_per = N_HEADS // HEAD_SPLIT
    rows = SAMPLE_ROWS
    params = [wts[name] for name in _SAMPLE_PARAMS]
    s_spec = pl.BlockSpec((1, rows, heads_per, HEAD_DIM, HEAD_DIM), lambda l, i, j: (l, i, j, 0, 0))
    c_spec = pl.BlockSpec((1, CONV_W - 1, rows, conv_dim), lambda l, i, j: (l, 0, i, 0))
    in_specs = [pl.BlockSpec((n, d), lambda l, i, j: (0, 0)), s_spec, c_spec]
    in_specs += [_layer_spec(p, None, 3) for p in params]
    out_specs = [
        pl.BlockSpec((n, d), lambda l, i, j: (0, 0)),
        s_spec,
        c_spec,
        pl.BlockSpec((1, n, d), lambda l, i, j: (l, 0, 0)),
    ]
    out_shape = [
        jax.ShapeDtypeStruct((n, d), F32),
        jax.ShapeDtypeStruct(state_gdn.shape, F32),
        jax.ShapeDtypeStruct(state_conv.shape, F32),
        jax.ShapeDtypeStruct((depth, n, d), F32),
    ]
    scratch = [
        pltpu.VMEM((n, d), F32),
        pltpu.VMEM((n, conv_dim), F32),
        pltpu.VMEM((n, kdim), F32),
        pltpu.VMEM((n, d), F32),
        pltpu.VMEM((n, d), F32),
        pltpu.VMEM((n, LANES), F32),
        pltpu.VMEM((n, LANES), F32),
        pltpu.VMEM((HEAD_SPLIT, n, heads_per * HEAD_DIM), F32),
        pltpu.VMEM((HEAD_SPLIT, rows, heads_per * HEAD_DIM), F32),
        pltpu.VMEM((HEAD_SPLIT, rows, heads_per * HEAD_DIM), F32),
        pltpu.VMEM((HEAD_SPLIT, rows, heads_per * HEAD_DIM), F32),
        pltpu.VMEM((N_HEADS, rows, HEAD_DIM), F32),
        pltpu.VMEM((N_HEADS, rows, HEAD_DIM), F32),
        pltpu.VMEM((rows, heads_per * HEAD_DIM), F32),
    ]
    return pl.pallas_call(
        _sample_kernel,
        grid=(depth, n // rows, HEAD_SPLIT),
        in_specs=in_specs,
        out_specs=out_specs,
        out_shape=out_shape,
        scratch_shapes=scratch,
        compiler_params=pltpu.CompilerParams(
            dimension_semantics=("arbitrary", "arbitrary", "arbitrary"),
            vmem_limit_bytes=VMEM_LIMIT),
        name="sample_path",
    )(x, state_gdn, state_conv, *params)


def _to_kernel_order(x):
    b, l, d = x.shape
    return x.reshape(b, l // CHUNK, SUBLANES, VROWS, d).swapaxes(2, 3).reshape(b, l, d)


def _from_kernel_order(x):
    b, l, d = x.shape
    return x.reshape(b, l // CHUNK, VROWS, SUBLANES, d).swapaxes(2, 3).reshape(b, l, d)


def _pack(x):
    return pltpu.bitcast(x.astype(BF16), jnp.uint32)


def _cast_w_in_kernel(wt_ref, uvz_ref, qkvz_ref, gates_ref, ab_ref):
    n_uvz, n_qkvz, n_gates = uvz_ref.shape[2], qkvz_ref.shape[2], gates_ref.shape[2]
    ab0 = n_uvz + n_qkvz
    n_ab = 2 * N_HEADS
    uvz_ref[0] = _pack(wt_ref[0, 0:n_uvz, :].T)
    qkvz_ref[0] = _pack(wt_ref[0, n_uvz:ab0, :].T)
    gates_ref[0] = _pack(wt_ref[0, ab0 + n_ab:ab0 + n_ab + n_gates, :].T)
    ab_blk = wt_ref[0, ab0:ab0 + LANES, :].T
    lane = _iota2(ab_blk.shape, 1)
    ab_ref[0] = jnp.where(lane < n_ab, ab_blk, 0.0).astype(BF16)


def _cast_square_kernel(a_ref, b_ref, c_ref, oa_ref, ob_ref, oc_ref):
    oa_ref[0] = _pack(a_ref[0])
    ob_ref[0] = _pack(b_ref[0])
    oc_ref[0] = _pack(c_ref[0])


def _cast_weights(w_in, w_proj_a, w_proj_b, w_out):
    depth, d, in_dim = w_in.shape
    kdim = N_HEADS * HEAD_DIM
    rb = 256
    grid = (depth, d // rb)
    u32 = lambda n: jax.ShapeDtypeStruct((depth, d // 2, n), jnp.uint32)
    out_block = lambda n: pl.BlockSpec((1, rb // 2, n), lambda l, i: (l, i, 0))
    params = pltpu.CompilerParams(dimension_semantics=("arbitrary", "arbitrary"),
                                  vmem_limit_bytes=VMEM_LIMIT)
    uvz, qkvz, gates, ab = pl.pallas_call(
        _cast_w_in_kernel,
        grid=grid,
        in_specs=[pl.BlockSpec((1, in_dim, rb), lambda l, i: (l, 0, i))],
        out_specs=[out_block(3 * d), out_block(4 * kdim), out_block(2 * d),
                   pl.BlockSpec((1, rb, LANES), lambda l, i: (l, i, 0))],
        out_shape=[u32(3 * d), u32(4 * kdim), u32(2 * d),
                   jax.ShapeDtypeStruct((depth, d, LANES), BF16)],
        compiler_params=params,
        name="cast_w_in",
    )(jnp.swapaxes(w_in, 1, 2))
    sq_in = pl.BlockSpec((1, rb, d), lambda l, i: (l, i, 0))
    pa, pb, out = pl.pallas_call(
        _cast_square_kernel,
        grid=grid,
        in_specs=[sq_in, sq_in, sq_in],
        out_specs=[out_block(d)] * 3,
        out_shape=[u32(d)] * 3,
        compiler_params=params,
        name="cast_w_square",
    )(w_proj_a, w_proj_b, w_out)
    return {"w_uvz": uvz, "w_qkvz": qkvz, "w_gates": gates, "w_ab": ab,
            "w_pa": pa, "w_pb": pb, "w_out": out}


def _prepare_weights(pre_norm, w_in, gmlp_norm, w_spatial, b_spatial, conv_w, a_log, dt_bias,
                     gdn_norm, w_proj_a, w_proj_b, w_out, post_norm):
    depth = w_in.shape[0]
    pad_lanes = lambda a: jnp.pad(a, ((0, 0), (0, LANES - N_HEADS)))[:, None, :]
    rows = jnp.arange(CHUNK)
    tok = (rows % SUBLANES) * VROWS + rows // SUBLANES
    return {
        **_cast_weights(w_in, w_proj_a, w_proj_b, w_out),
        "pre_g": pre_norm[:, None, :],
        "gmlp_g": gmlp_norm[:, None, :],
        "w_s": w_spatial[:, :, tok, :][:, :, :, tok],
        "b_s": jnp.broadcast_to(b_spatial[:, :, tok, None], (depth, N_HEADS, CHUNK, HEAD_DIM)),
        "ws00": jnp.repeat(w_spatial[:, :, 0, 0], HEAD_DIM, axis=1)[:, None, :],
        "bs0": jnp.repeat(b_spatial[:, :, 0], HEAD_DIM, axis=1)[:, None, :],
        "conv_w": conv_w,
        "a_log": pad_lanes(a_log),
        "dt_bias": pad_lanes(dt_bias),
        "gdn_g": gdn_norm[:, None, :],
        "post_g": post_norm[:, None, :],
    }


def kernel(x_prompt, x_sample, state_gdn, state_conv, pre_norm, w_in, gmlp_norm, w_spatial, b_spatial, conv_w, a_log, dt_bias, gdn_norm, w_proj_a, w_proj_b, w_out, post_norm):
    depth = w_in.shape[0]
    wts = _prepare_weights(pre_norm, w_in, gmlp_norm, w_spatial, b_spatial, conv_w, a_log, dt_bias,
                           gdn_norm, w_proj_a, w_proj_b, w_out, post_norm)
    xp = _to_kernel_order(x_prompt)
    gdn_p, conv_p = [], []
    for l in range(depth):
        xp, sg_p, cb_p = _prompt_layer(xp, wts, l)
        gdn_p.append(sg_p)
        conv_p.append(cb_p)
    xp = _from_kernel_order(xp)
    ys, gdn_s, conv_s, vrows_s = _sample_path(x_sample[:, 0, :], state_gdn,
                                              jnp.swapaxes(state_conv, 1, 2), wts)
    return (xp, ys[:, None, :], jnp.stack(gdn_p), jnp.stack(conv_p), gdn_s,
            jnp.swapaxes(conv_s, 1, 2), vrows_s[:, :, None, :])
```

```python
import jax
import jax.numpy as jnp
from jax import lax
from jax.experimental import pallas as pl
from jax.experimental.pallas import tpu as pltpu

F32 = jnp.float32
BF16 = jnp.bfloat16
EPS = 1e-6

LANES = 128
SUBLANES = 8
HEAD_DIM = 128
N_HEADS = 8
CONV_W = 4
CHUNK = 128
TILE_M = 256
VROWS = CHUNK // SUBLANES
SAMPLE_ROWS = 8
HEAD_SPLIT = 1
VMEM_LIMIT = 56 * 1024 * 1024


def _dot(a, b):
    return jnp.dot(a.astype(BF16), b.astype(BF16), preferred_element_type=F32)


def _dot_nt(a, b):
    return lax.dot_general(a.astype(BF16), b.astype(BF16), (((1,), (1,)), ((), ())),
                           preferred_element_type=F32)


def _dot_tn(a, b):
    return lax.dot_general(a.astype(BF16), b.astype(BF16), (((0,), (0,)), ((), ())),
                           preferred_element_type=F32)


def _bdiag(a, b):
    z = jnp.zeros_like(a)
    return jnp.concatenate([jnp.concatenate([a, z], axis=1), jnp.concatenate([z, b], axis=1)], axis=0)


def _w(w_ref):
    return pltpu.bitcast(w_ref[...], BF16)


def _split(a):
    hi = a.astype(BF16)
    lo = (a - hi.astype(F32)).astype(BF16)
    return hi, lo


def _dot_exact(a, b):
    return jnp.dot(a, b, preferred_element_type=F32, precision=lax.Precision.HIGHEST)


def _rms(x, g):
    return x * lax.rsqrt(jnp.mean(x * x, axis=-1, keepdims=True) + EPS) * g


def _silu(x):
    return x * jax.nn.sigmoid(x)


def _softplus(x):
    return jnp.maximum(x, 0.0) + jnp.log(1.0 + jnp.exp(-jnp.abs(x)))


def _l2norm(x):
    return x * lax.rsqrt(jnp.sum(x * x, axis=-1, keepdims=True) + EPS)


def _iota2(shape, dim):
    return lax.broadcasted_iota(jnp.int32, shape, dim)


def _alternate(*stage_gens):
    pending = list(stage_gens)
    while pending:
        for gen in list(pending):
            try:
                next(gen)
            except StopIteration:
                pending.remove(gen)


def _prompt_kernel(x_ref, pre_g_ref, w_uvz_ref, w_qkvz_ref, w_gates_ref, w_ab_ref, gmlp_g_ref, ws_ref, bs_ref,
                   convw_ref, alog_ref, dtb_ref, gdn_g_ref, w_pa_ref, w_pb_ref, w_out_ref,
                   post_g_ref,
                   y_ref, s_out_ref, conv_out_ref,
                   s_scr, carry_scr, ya_scr, yb_scr, qkv_scr):
    t = pl.program_id(1)
    nt = pl.num_programs(1)
    tm = x_ref.shape[1]
    d = x_ref.shape[2]
    n_chunks = tm // CHUNK
    kdim = N_HEADS * HEAD_DIM

    @pl.when(t == 0)
    def _():
        s_scr[...] = jnp.zeros_like(s_scr)
        carry_scr[...] = jnp.zeros_like(carry_scr)

    x = x_ref[0]
    h = _rms(x, pre_g_ref[...]).astype(BF16)

    row = _iota2((CHUNK, CHUNK), 0)
    col = _iota2((CHUNK, CHUNK), 1)
    token_of = lambda r: (r % SUBLANES) * VROWS + r // SUBLANES
    row_tok, col_tok = token_of(row), token_of(col)
    incl = row_tok >= col_tok
    strict = row_tok > col_tok

    qkvz = jnp.dot(h, jnp.concatenate([_w(w_qkvz_ref), w_ab_ref[...]], axis=1),
                   preferred_element_type=F32)
    uvz = jnp.dot(h, _w(w_uvz_ref), preferred_element_type=F32)
    rowid = _iota2((SUBLANES, kdim), 0)
    n_tail = CONV_W - 1
    prev = [carry_scr[j:j + 1, :] for j in range(n_tail)]

    def conv_gen(c):
        r0 = c * CHUNK
        last_rows = [[] for _ in range(n_tail)]
        for lo in range(0, 3 * kdim, kdim):
            pre = qkvz[r0:r0 + CHUNK, lo:lo + kdim]
            wrapped = []
            for j in range(n_tail):
                blk = pre[(VROWS - n_tail + j) * SUBLANES:(VROWS - n_tail + j + 1) * SUBLANES, :]
                shifted = pltpu.roll(blk, 1, axis=0)
                wrapped.append(jnp.where(rowid == 0, jnp.broadcast_to(prev[j][:, lo:lo + kdim], blk.shape),
                                         shifted))
                last_rows[j].append(blk[SUBLANES - 1:SUBLANES, :])
            conv = pre * convw_ref[CONV_W - 1:CONV_W, lo:lo + kdim]
            for k in range(1, CONV_W):
                back_k = jnp.concatenate(wrapped[n_tail - k:] + [pre[0:(VROWS - k) * SUBLANES, :]], axis=0)
                conv = conv + back_k * convw_ref[CONV_W - 1 - k:CONV_W - k, lo:lo + kdim]
            qkv_scr[r0:r0 + CHUNK, lo:lo + kdim] = _silu(conv)
            yield
        prev[:] = [jnp.concatenate(rows, axis=1) for rows in last_rows]

    gates = jnp.dot(h, _w(w_gates_ref), preferred_element_type=F32)
    ab = qkvz[:, 4 * kdim:4 * kdim + LANES]
    g_all = -jnp.exp(alog_ref[...]) * _softplus(ab + dtb_ref[...])
    beta_all = jax.nn.sigmoid(ab)
    g_all_t = g_all.T
    ltri = incl.astype(F32)
    utri = (row_tok <= col_tok).astype(F32)
    eye = (row == col).astype(F32)
    scale = HEAD_DIM ** -0.5

    pair_heads = range(0, N_HEADS, 2)
    left = lambda a: a[:, 0:HEAD_DIM]
    right = lambda a: a[:, HEAD_DIM:2 * HEAD_DIM]
    both = lambda a, b: jnp.concatenate([a, b], axis=1)
    stack = lambda a, b: jnp.concatenate([a, b], axis=0)
    unpair = lambda pairs: [half(p) for p in pairs for half in (left, right)]
    st = [dict() for _ in range(n_chunks)]
    branch_a = {}

    def front_gen(c):
        r0, r1 = c * CHUNK, (c + 1) * CHUNK
        gc = _dot_exact(ltri, g_all[r0:r1, :])
        gct = _dot_exact(g_all_t[0:N_HEADS, r0:r1], utri)
        egc = jnp.exp(gc)
        qs, ks = [], []
        for hd in range(N_HEADS):
            lo, hi = hd * HEAD_DIM, (hd + 1) * HEAD_DIM
            qs.append(_l2norm(qkv_scr[r0:r1, lo:hi]) * scale)
            ks.append(_l2norm(qkv_scr[r0:r1, kdim + lo:kdim + hi]))
            if hd % 2:
                yield
        kkqks = []
        for hd in pair_heads:
            k1, k2 = ks[hd].astype(BF16), ks[hd + 1].astype(BF16)
            lhs = both(stack(k1, qs[hd].astype(BF16)), stack(k2, qs[hd + 1].astype(BF16)))
            kkqks.append(_dot_nt(lhs, _bdiag(k1, k2)))
        kkqks = unpair(kkqks)
        yield
        a_mats, attns, rhss = [], [], []
        for hd in range(N_HEADS):
            lo, hi = hd * HEAD_DIM, (hd + 1) * HEAD_DIM
            gcol = gc[:, hd:hd + 1]
            grow = gct[hd:hd + 1, :]
            bcol = beta_all[r0:r1, N_HEADS + hd:N_HEADS + hd + 1]
            dec = jnp.where(incl, jnp.exp(jnp.where(incl, gcol - grow, 0.0)), 0.0)
            a_mats.append(jnp.where(strict, bcol * kkqks[hd][0:CHUNK] * dec, 0.0))
            attns.append(kkqks[hd][CHUNK:2 * CHUNK] * dec)
            vh = qkv_scr[r0:r1, 2 * kdim + lo:2 * kdim + hi]
            rhss.append(jnp.concatenate([vh * bcol, ks[hd] * (bcol * egc[:, hd:hd + 1])], axis=1))
            if hd % 2:
                yield
        st[c].update(gc=gc, egc=egc, qs=qs, ks=ks, a_mats=a_mats, attns=attns, rhss=rhss)

    def solve_gen(c):
        a_mats, rhss = st[c]["a_mats"], st[c]["rhss"]
        p_pairs = [both(eye - a_mats[hd], eye - a_mats[hd + 1]) for hd in pair_heads]
        b_pairs = [both(a_mats[hd], a_mats[hd + 1]).astype(BF16) for hd in pair_heads]
        b_pairs = [_dot(b, _bdiag(left(b), right(b))).astype(BF16) for b in b_pairs]
        yield
        n_sq = (CHUNK - 1).bit_length() - 1
        for _ in range(n_sq - 1):
            prods = [_dot(stack(p.astype(BF16), b), _bdiag(left(b), right(b)))
                     for p, b in zip(p_pairs, b_pairs)]
            p_pairs = [p + pr[0:CHUNK] for p, pr in zip(p_pairs, prods)]
            b_pairs = [pr[CHUNK:2 * CHUNK].astype(BF16) for pr in prods]
            yield
        p_pairs = [p + _dot(p, _bdiag(left(b), right(b))) for p, b in zip(p_pairs, b_pairs)]
        p_mats = [p.astype(BF16) for p in unpair(p_pairs)]
        yield
        uws = [_dot(p, r) for p, r in zip(p_mats, rhss)]
        yield
        resids = [r - uw - _dot(a, uw) for a, uw, r in zip(a_mats, uws, rhss)]
        yield
        st[c]["uws"] = [uw + _dot(p, rs) for uw, p, rs in zip(uws, p_mats, resids)]

    def recur_gen(c):
        r0, r1 = c * CHUNK, (c + 1) * CHUNK
        gc, egc, qs, ks, attns, uws = (st[c][n] for n in ("gc", "egc", "qs", "ks", "attns", "uws"))
        s_olds = [s_scr[hd] for hd in range(N_HEADS)]
        wq_l = lambda hd: stack(uws[hd][:, HEAD_DIM:], qs[hd] * egc[:, hd:hd + 1]).astype(BF16)
        wqs = unpair([_dot(both(wq_l(hd), wq_l(hd + 1)),
                           _bdiag(s_olds[hd].astype(BF16), s_olds[hd + 1].astype(BF16)))
                      for hd in pair_heads])
        yield
        v_news = [uws[hd][:, 0:HEAD_DIM] - wqs[hd][0:CHUNK] for hd in range(N_HEADS)]
        avs = unpair([_dot(both(attns[hd], attns[hd + 1]),
                           _bdiag(v_news[hd].astype(BF16), v_news[hd + 1].astype(BF16)))
                      for hd in pair_heads])
        for hd in range(N_HEADS):
            glast = gc[CHUNK - 1:CHUNK, hd:hd + 1]
            kd = ks[hd] * jnp.exp(glast - gc[:, hd:hd + 1])
            s_scr[hd] = s_olds[hd] * jnp.exp(glast) + _dot_tn(kd, v_news[hd])
        yield
        for hd in range(N_HEADS):
            lo, hi = hd * HEAD_DIM, (hd + 1) * HEAD_DIM
            o = wqs[hd][CHUNK:2 * CHUNK] + avs[hd]
            zb = qkvz[r0:r1, 3 * kdim + lo:3 * kdim + hi]
            yb_scr[r0:r1, lo:hi] = (_rms(o, gdn_g_ref[...]) * _silu(zb)).astype(BF16)
            if hd % 2:
                yield

    def branch_a_gen():
        vn = _rms(uvz[:, d:2 * d], gmlp_g_ref[...])
        vnb = vn.astype(BF16)
        yield
        for g in range(N_HEADS):
            wsg = jnp.where(incl, ws_ref[g], 0.0).astype(BF16)
            lo, hi = g * HEAD_DIM, (g + 1) * HEAD_DIM
            mixed_all = jnp.dot(
                wsg, jnp.concatenate([vnb[c * CHUNK:(c + 1) * CHUNK, lo:hi] for c in range(n_chunks)], axis=1),
                preferred_element_type=F32)
            for c in range(n_chunks):
                r0, r1 = c * CHUNK, (c + 1) * CHUNK
                mixed = mixed_all[:, c * HEAD_DIM:(c + 1) * HEAD_DIM] + bs_ref[g]
                u = uvz[r0:r1, lo:hi]
                z = uvz[r0:r1, 2 * d + lo:2 * d + hi]
                ya_scr[r0:r1, lo:hi] = (u * mixed * _silu(z)).astype(BF16)
            yield
        pa = jnp.dot(ya_scr[...], _w(w_pa_ref), preferred_element_type=F32)
        yield
        branch_a["gated_pa"] = jax.nn.sigmoid(gates[:, 0:d]) * pa
        yield
        branch_a["gate_b"] = jax.nn.sigmoid(gates[:, d:2 * d])

    def chain(*gens):
        for g in gens:
            yield from g

    _alternate(chain(conv_gen(0), front_gen(0)))
    for c in range(n_chunks):
        fillers = []
        if c + 1 < n_chunks:
            fillers.append(chain(conv_gen(c + 1), front_gen(c + 1)))
        if c > 0:
            fillers.append(recur_gen(c - 1))
        if c == n_chunks - 1:
            fillers.append(branch_a_gen())
        _alternate(solve_gen(c), chain(*fillers))
    _alternate(recur_gen(n_chunks - 1))
    gated_pa, gate_b = branch_a["gated_pa"], branch_a["gate_b"]

    tail = jnp.concatenate(prev, axis=0)
    carry_scr[0:n_tail, :] = tail

    @pl.when(t == nt - 1)
    def _():
        conv_out_ref[0] = tail

    @pl.when(t == nt - 1)
    def _():
        s_out_ref[0] = s_scr[...]

    pb = jnp.dot(yb_scr[...], _w(w_pb_ref), preferred_element_type=F32)

    merged = gated_pa + gate_b * pb
    out = jnp.dot(merged.astype(BF16), _w(w_out_ref), preferred_element_type=F32)
    y_ref[0] = x + _rms(out, post_g_ref[...])


def _layer_spec(arr, layer, n_grid):
    shape = (None,) + tuple(arr.shape[1:])
    zeros = (0,) * (arr.ndim - 1)
    if n_grid == 2:
        index_map = lambda b, t: (layer,) + zeros
    else:
        index_map = lambda l, i, j: (l,) + zeros
    return pl.BlockSpec(shape, index_map, pipeline_mode=pl.Buffered(1))


_PROMPT_PARAMS = ("pre_g", "w_uvz", "w_qkvz", "w_gates", "w_ab", "gmlp_g", "w_s", "b_s", "conv_w", "a_log", "dt_bias",
                  "gdn_g", "w_pa", "w_pb", "w_out", "post_g")


def _prompt_layer(x, wts, layer):
    bsz, seq, d = x.shape
    tm = min(TILE_M, seq)
    nt = seq // tm
    kdim = N_HEADS * HEAD_DIM
    conv_dim = 3 * kdim
    params = [wts[name] for name in _PROMPT_PARAMS]
    in_specs = [pl.BlockSpec((1, tm, d), lambda b, t: (b, t, 0))]
    in_specs += [_layer_spec(p, layer, 2) for p in params]
    out_specs = [
        pl.BlockSpec((1, tm, d), lambda b, t: (b, t, 0)),
        pl.BlockSpec((1, N_HEADS, HEAD_DIM, HEAD_DIM), lambda b, t: (b, 0, 0, 0)),
        pl.BlockSpec((1, CONV_W - 1, conv_dim), lambda b, t: (b, 0, 0)),
    ]
    out_shape = [
        jax.ShapeDtypeStruct((bsz, seq, d), F32),
        jax.ShapeDtypeStruct((bsz, N_HEADS, HEAD_DIM, HEAD_DIM), F32),
        jax.ShapeDtypeStruct((bsz, CONV_W - 1, conv_dim), F32),
    ]
    scratch = [
        pltpu.VMEM((N_HEADS, HEAD_DIM, HEAD_DIM), F32),
        pltpu.VMEM((SUBLANES, conv_dim), F32),
        pltpu.VMEM((tm, d), BF16),
        pltpu.VMEM((tm, kdim), BF16),
        pltpu.VMEM((tm, conv_dim), F32),
    ]
    return pl.pallas_call(
        _prompt_kernel,
        grid=(bsz, nt),
        in_specs=in_specs,
        out_specs=out_specs,
        out_shape=out_shape,
        scratch_shapes=scratch,
        compiler_params=pltpu.CompilerParams(
            dimension_semantics=("arbitrary", "arbitrary"),
            vmem_limit_bytes=VMEM_LIMIT),
        name="prompt_layer",
    )(x, *params)


def _sample_kernel(x_ref, s_ref, cs_ref, pre_g_ref, w_uvz_ref, w_qkvz_ref, w_gates_ref, w_ab_ref, gmlp_g_ref, ws00_ref,
                   bs0_ref, convw_ref, alog_ref, dtb_ref, gdn_g_ref, w_pa_ref, w_pb_ref, w_out_ref,
                   post_g_ref,
                   y_ref, s_out_ref, conv_out_ref, vrows_ref,
                   xs_scr, qkvn_scr, zb_scr, sga_pa_scr, sgb_scr, beta_scr, eg_scr, o_scr,
                   q_t, k_t, v_t, beta_t, eg_t, o_t):
    layer = pl.program_id(0)
    i = pl.program_id(1)
    hh = pl.program_id(2)
    n_tiles = pl.num_programs(1)
    n_split = pl.num_programs(2)
    d = x_ref.shape[1]
    kdim = N_HEADS * HEAD_DIM
    heads_per = N_HEADS // HEAD_SPLIT
    wid = heads_per * HEAD_DIM
    rows = SAMPLE_ROWS
    scale = HEAD_DIM ** -0.5

    @pl.when(jnp.logical_and(i == 0, hh == 0))
    def _():
        @pl.when(layer == 0)
        def _():
            xs_scr[...] = x_ref[...]

        x = xs_scr[...]
        h = _rms(x, pre_g_ref[...]).astype(BF16)
        uvz = jnp.dot(h, _w(w_uvz_ref), preferred_element_type=F32)
        vn = _rms(uvz[:, d:2 * d], gmlp_g_ref[...])
        vrows_ref[0] = vn
        mixed = ws00_ref[...] * vn + bs0_ref[...]
        ya = uvz[:, 0:d] * mixed * _silu(uvz[:, 2 * d:3 * d])
        pa = jnp.dot(ya.astype(BF16), _w(w_pa_ref), preferred_element_type=F32)
        qkvz = jnp.dot(h, _w(w_qkvz_ref), preferred_element_type=F32)
        qkvn_scr[...] = qkvz[:, 0:3 * kdim]
        zb_scr[...] = qkvz[:, 3 * kdim:4 * kdim]
        ab = jnp.dot(h, w_ab_ref[...], preferred_element_type=F32)
        eg_scr[...] = jnp.exp(-jnp.exp(alog_ref[...]) * _softplus(ab + dtb_ref[...]))
        beta_scr[...] = jax.nn.sigmoid(ab)
        gates = jnp.dot(h, _w(w_gates_ref),
                        preferred_element_type=F32)
        sga_pa_scr[...] = jax.nn.sigmoid(gates[:, 0:d]) * pa
        sgb_scr[...] = jax.nn.sigmoid(gates[:, d:2 * d])

    b0 = pl.multiple_of(i * rows, rows)

    @pl.when(hh == 0)
    def _():
        cs = cs_ref[0]
        qkv_new = qkvn_scr[pl.ds(b0, rows), :]
        conv = (cs[0] * convw_ref[0:1, :] + cs[1] * convw_ref[1:2, :]
                + cs[2] * convw_ref[2:3, :] + qkv_new * convw_ref[3:4, :])
        conv_out_ref[0, 0] = cs[1]
        conv_out_ref[0, 1] = cs[2]
        conv_out_ref[0, 2] = qkv_new
        qkv = _silu(conv)
        beta_rows = beta_scr[pl.ds(b0, rows), :]
        eg_rows = eg_scr[pl.ds(b0, rows), :]
        for hd in range(N_HEADS):
            lo, hi = hd * HEAD_DIM, (hd + 1) * HEAD_DIM
            half, off = hd // heads_per, (hd % heads_per) * HEAD_DIM
            q_t[half, :, off:off + HEAD_DIM] = _l2norm(qkv[:, lo:hi]) * scale
            k_t[half, :, off:off + HEAD_DIM] = _l2norm(qkv[:, kdim + lo:kdim + hi])
            v_t[half, :, off:off + HEAD_DIM] = qkv[:, 2 * kdim + lo:2 * kdim + hi]
            beta_t[hd] = jnp.broadcast_to(beta_rows[:, N_HEADS + hd:N_HEADS + hd + 1], (rows, HEAD_DIM))
            eg_t[hd] = jnp.broadcast_to(eg_rows[:, hd:hd + 1], (rows, HEAD_DIM))

    rowid = _iota2((SUBLANES, HEAD_DIM), 0)
    states = [(r, j) for r in range(rows) for j in range(heads_per)]
    qv, kv, vv, bv, ev, sv, ksqs = [], [], [], [], [], [], []
    for r, j in states:
        off = j * HEAD_DIM
        q = q_t[hh, r:r + 1, off:off + HEAD_DIM]
        k = k_t[hh, r:r + 1, off:off + HEAD_DIM]
        qv.append(q)
        kv.append(k)
        vv.append(v_t[hh, r:r + 1, off:off + HEAD_DIM])
        bv.append(beta_t[hh * heads_per + j, r:r + 1, :])
        ev.append(eg_t[hh * heads_per + j, r:r + 1, :])
        s = s_ref[0, r, j]
        sv.append(s)
        kq8 = jnp.where(rowid == 0, jnp.broadcast_to(k, (SUBLANES, HEAD_DIM)),
                        jnp.where(rowid == 1, jnp.broadcast_to(q, (SUBLANES, HEAD_DIM)), 0.0))
        ksqs.append(_dot(kq8, s))
    k8s, v8s = [], []
    for n, (r, j) in enumerate(states):
        v_new = bv[n] * (vv[n] - ev[n] * ksqs[n][0:1, :])
        qk = jnp.sum(qv[n] * kv[n], axis=-1, keepdims=True)
        o_t[r:r + 1, j * HEAD_DIM:(j + 1) * HEAD_DIM] = ev[n] * ksqs[n][1:2, :] + qk * v_new
        k_hi, k_lo = _split(kv[n])
        v_hi, v_lo = _split(v_new)
        bc = lambda a: jnp.broadcast_to(a.astype(F32), (SUBLANES, HEAD_DIM))
        k8s.append(jnp.where(rowid <= 1, bc(k_hi), jnp.where(rowid == 2, bc(k_lo), 0.0)))
        v8s.append(jnp.where(rowid == 0, bc(v_hi), jnp.where(rowid == 1, bc(v_lo),
                                                              jnp.where(rowid == 2, bc(v_hi), 0.0))))
    for n in range(0, len(states), 2):
        (r, j), (r2, j2) = states[n], states[n + 1]
        upd = _dot_tn(jnp.concatenate([k8s[n], k8s[n + 1]], axis=0).astype(BF16),
                      _bdiag(v8s[n].astype(BF16), v8s[n + 1].astype(BF16)))
        s_out_ref[0, r, j] = sv[n] * ev[n] + upd[:, 0:HEAD_DIM]
        s_out_ref[0, r2, j2] = sv[n + 1] * ev[n + 1] + upd[:, HEAD_DIM:2 * HEAD_DIM]
    o_scr[hh, pl.ds(b0, rows), :] = o_t[...]

    @pl.when(jnp.logical_and(i == n_tiles - 1, hh == n_split - 1))
    def _():
        pb = jnp.zeros((x_ref.shape[0], d), F32)
        for hd in range(N_HEADS):
            lo, hi = hd * HEAD_DIM, (hd + 1) * HEAD_DIM
            half, off = hd // heads_per, (hd % heads_per) * HEAD_DIM
            o = o_scr[half, :, off:off + HEAD_DIM]
            yb = (_rms(o, gdn_g_ref[...]) * _silu(zb_scr[:, lo:hi])).astype(BF16)
            pb = pb + jnp.dot(yb, pltpu.bitcast(w_pb_ref[lo // 2:hi // 2, :], BF16),
                              preferred_element_type=F32)
        merged = sga_pa_scr[...] + sgb_scr[...] * pb
        out = jnp.dot(merged.astype(BF16), _w(w_out_ref), preferred_element_type=F32)
        y = xs_scr[...] + _rms(out, post_g_ref[...])
        xs_scr[...] = y
        y_ref[...] = y


_SAMPLE_PARAMS = ("pre_g", "w_uvz", "w_qkvz", "w_gates", "w_ab", "gmlp_g", "ws00", "bs0", "conv_w", "a_log", "dt_bias",
                  "gdn_g", "w_pa", "w_pb", "w_out", "post_g")


def _sample_path(x, state_gdn, state_conv, wts):
    n, d = x.shape
    depth = state_gdn.shape[0]
    kdim = N_HEADS * HEAD_DIM
    conv_dim = 3 * kdim
    heads_per = N_HEADS // HEAD_SPLIT
    rows = SAMPLE_ROWS
    params = [wts[name] for name in _SAMPLE_PARAMS]
    s_spec = pl.BlockSpec((1, rows, heads_per, HEAD_DIM, HEAD_DIM), lambda l, i, j: (l, i, j, 0, 0))
    c_spec = pl.BlockSpec((1, CONV_W - 1, rows, conv_dim), lambda l, i, j: (l, 0, i, 0))
    in_specs = [pl.BlockSpec((n, d), lambda l, i, j: (0, 0)), s_spec, c_spec]
    in_specs += [_layer_spec(p, None, 3) for p in params]
    out_specs = [
        pl.BlockSpec((n, d), lambda l, i, j: (0, 0)),
        s_spec,
        c_spec,
        pl.BlockSpec((1, n, d), lambda l, i, j: (l, 0, 0)),
    ]
    out_shape = [
        jax.ShapeDtypeStruct((n, d), F32),
        jax.ShapeDtypeStruct(state_gdn.shape, F32),
        jax.ShapeDtypeStruct(state_conv.shape, F32),
        jax.ShapeDtypeStruct((depth, n, d), F32),
    ]
    scratch = [
        pltpu.VMEM((n, d), F32),
        pltpu.VMEM((n, conv_dim), F32),
        pltpu.VMEM((n, kdim), F32),
        pltpu.VMEM((n, d), F32),
        pltpu.VMEM((n, d), F32),
        pltpu.VMEM((n, LANES), F32),
        pltpu.VMEM((n, LANES), F32),
        pltpu.VMEM((HEAD_SPLIT, n, heads_per * HEAD_DIM), F32),
        pltpu.VMEM((HEAD_SPLIT, rows, heads_per * HEAD_DIM), F32),
        pltpu.VMEM((HEAD_SPLIT, rows, heads_per * HEAD_DIM), F32),
        pltpu.VMEM((HEAD_SPLIT, rows, heads_per * HEAD_DIM), F32),
        pltpu.VMEM((N_HEADS, rows, HEAD_DIM), F32),
        pltpu.VMEM((N_HEADS, rows, HEAD_DIM), F32),
        pltpu.VMEM((rows, heads_per * HEAD_DIM), F32),
    ]
    return pl.pallas_call(
        _sample_kernel,
        grid=(depth, n // rows, HEAD_SPLIT),
        in_specs=in_specs,
        out_specs=out_specs,
        out_shape=out_shape,
        scratch_shapes=scratch,
        compiler_params=pltpu.CompilerParams(
            dimension_semantics=("arbitrary", "arbitrary", "arbitrary"),
            vmem_limit_bytes=VMEM_LIMIT),
        name="sample_path",
    )(x, state_gdn, state_conv, *params)


def _to_kernel_order(x):
    b, l, d = x.shape
    return x.reshape(b, l // CHUNK, SUBLANES, VROWS, d).swapaxes(2, 3).reshape(b, l, d)


def _from_kernel_order(x):
    b, l, d = x.shape
    return x.reshape(b, l // CHUNK, VROWS, SUBLANES, d).swapaxes(2, 3).reshape(b, l, d)


def _pack(x):
    return pltpu.bitcast(x.astype(BF16), jnp.uint32)


def _cast_w_in_kernel(wt_ref, uvz_ref, qkvz_ref, gates_ref, ab_ref):
    n_uvz, n_qkvz, n_gates = uvz_ref.shape[2], qkvz_ref.shape[2], gates_ref.shape[2]
    ab0 = n_uvz + n_qkvz
    n_ab = 2 * N_HEADS
    uvz_ref[0] = _pack(wt_ref[0, 0:n_uvz, :].T)
    qkvz_ref[0] = _pack(wt_ref[0, n_uvz:ab0, :].T)
    gates_ref[0] = _pack(wt_ref[0, ab0 + n_ab:ab0 + n_ab + n_gates, :].T)
    ab_blk = wt_ref[0, ab0:ab0 + LANES, :].T
    lane = _iota2(ab_blk.shape, 1)
    ab_ref[0] = jnp.where(lane < n_ab, ab_blk, 0.0).astype(BF16)


def _cast_square_kernel(a_ref, b_ref, c_ref, oa_ref, ob_ref, oc_ref):
    oa_ref[0] = _pack(a_ref[0])
    ob_ref[0] = _pack(b_ref[0])
    oc_ref[0] = _pack(c_ref[0])


def _cast_weights(w_in, w_proj_a, w_proj_b, w_out):
    depth, d, in_dim = w_in.shape
    kdim = N_HEADS * HEAD_DIM
    rb = 256
    grid = (depth, d // rb)
    u32 = lambda n: jax.ShapeDtypeStruct((depth, d // 2, n), jnp.uint32)
    out_block = lambda n: pl.BlockSpec((1, rb // 2, n), lambda l, i: (l, i, 0))
    params = pltpu.CompilerParams(dimension_semantics=("arbitrary", "arbitrary"),
                                  vmem_limit_bytes=VMEM_LIMIT)
    uvz, qkvz, gates, ab = pl.pallas_call(
        _cast_w_in_kernel,
        grid=grid,
        in_specs=[pl.BlockSpec((1, in_dim, rb), lambda l, i: (l, 0, i))],
        out_specs=[out_block(3 * d), out_block(4 * kdim), out_block(2 * d),
                   pl.BlockSpec((1, rb, LANES), lambda l, i: (l, i, 0))],
        out_shape=[u32(3 * d), u32(4 * kdim), u32(2 * d),
                   jax.ShapeDtypeStruct((depth, d, LANES), BF16)],
        compiler_params=params,
        name="cast_w_in",
    )(jnp.swapaxes(w_in, 1, 2))
    sq_in = pl.BlockSpec((1, rb, d), lambda l, i: (l, i, 0))
    pa, pb, out = pl.pallas_call(
        _cast_square_kernel,
        grid=grid,
        in_specs=[sq_in, sq_in, sq_in],
        out_specs=[out_block(d)] * 3,
        out_shape=[u32(d)] * 3,
        compiler_params=params,
        name="cast_w_square",
    )(w_proj_a, w_proj_b, w_out)
    return {"w_uvz": uvz, "w_qkvz": qkvz, "w_gates": gates, "w_ab": ab,
            "w_pa": pa, "w_pb": pb, "w_out": out}


def _prepare_weights(pre_norm, w_in, gmlp_norm, w_spatial, b_spatial, conv_w, a_log, dt_bias,
                     gdn_norm, w_proj_a, w_proj_b, w_out, post_norm):
    depth = w_in.shape[0]
    pad_lanes = lambda a: jnp.pad(a, ((0, 0), (0, LANES - N_HEADS)))[:, None, :]
    rows = jnp.arange(CHUNK)
    tok = (rows % SUBLANES) * VROWS + rows // SUBLANES
    return {
        **_cast_weights(w_in, w_proj_a, w_proj_b, w_out),
        "pre_g": pre_norm[:, None, :],
        "gmlp_g": gmlp_norm[:, None, :],
        "w_s": w_spatial[:, :, tok, :][:, :, :, tok],
        "b_s": jnp.broadcast_to(b_spatial[:, :, tok, None], (depth, N_HEADS, CHUNK, HEAD_DIM)),
        "ws00": jnp.repeat(w_spatial[:, :, 0, 0], HEAD_DIM, axis=1)[:, None, :],
        "bs0": jnp.repeat(b_spatial[:, :, 0], HEAD_DIM, axis=1)[:, None, :],
        "conv_w": conv_w,
        "a_log": pad_lanes(a_log),
        "dt_bias": pad_lanes(dt_bias),
        "gdn_g": gdn_norm[:, None, :],
        "post_g": post_norm[:, None, :],
    }


def kernel(x_prompt, x_sample, state_gdn, state_conv, pre_norm, w_in, gmlp_norm, w_spatial, b_spatial, conv_w, a_log, dt_bias, gdn_norm, w_proj_a, w_proj_b, w_out, post_norm):
    depth = w_in.shape[0]
    wts = _prepare_weights(pre_norm, w_in, gmlp_norm, w_spatial, b_spatial, conv_w, a_log, dt_bias,
                           gdn_norm, w_proj_a, w_proj_b, w_out, post_norm)
    xp = _to_kernel_order(x_prompt)
    gdn_p, conv_p = [], []
    for l in range(depth):
        xp, sg_p, cb_p = _prompt_layer(xp, wts, l)
        gdn_p.append(sg_p)
        conv_p.append(cb_p)
    xp = _from_kernel_order(xp)
    ys, gdn_s, conv_s, vrows_s = _sample_path(x_sample[:, 0, :], state_gdn,
                                              jnp.swapaxes(state_conv, 1, 2), wts)
    return (xp, ys[:, None, :], jnp.stack(gdn_p), jnp.stack(conv_p), gdn_s,
            jnp.swapaxes(conv_s, 1, 2), vrows_s[:, :, None, :])
```

```python
import jax
import jax.numpy as jnp
from jax import lax
from jax.experimental import pallas as pl
from jax.experimental.pallas import tpu as pltpu

F32 = jnp.float32
BF16 = jnp.bfloat16
EPS = 1e-6

LANES = 128
SUBLANES = 8
HEAD_DIM = 128
N_HEADS = 8
CONV_W = 4
CHUNK = 128
TILE_M = 256
VROWS = CHUNK // SUBLANES
SAMPLE_ROWS = 8
HEAD_SPLIT = 1
VMEM_LIMIT = 56 * 1024 * 1024


def _dot(a, b):
    return jnp.dot(a.astype(BF16), b.astype(BF16), preferred_element_type=F32)


def _dot_nt(a, b):
    return lax.dot_general(a.astype(BF16), b.astype(BF16), (((1,), (1,)), ((), ())),
                           preferred_element_type=F32)


def _dot_tn(a, b):
    return lax.dot_general(a.astype(BF16), b.astype(BF16), (((0,), (0,)), ((), ())),
                           preferred_element_type=F32)


def _bdiag(a, b):
    z = jnp.zeros_like(a)
    return jnp.concatenate([jnp.concatenate([a, z], axis=1), jnp.concatenate([z, b], axis=1)], axis=0)


def _w(w_ref):
    return pltpu.bitcast(w_ref[...], BF16)


def _split(a):
    hi = a.astype(BF16)
    lo = (a - hi.astype(F32)).astype(BF16)
    return hi, lo


def _dot_exact(a, b):
    return jnp.dot(a, b, preferred_element_type=F32, precision=lax.Precision.HIGHEST)


def _rms(x, g):
    return x * lax.rsqrt(jnp.mean(x * x, axis=-1, keepdims=True) + EPS) * g


def _silu(x):
    return x * jax.nn.sigmoid(x)


def _softplus(x):
    return jnp.maximum(x, 0.0) + jnp.log(1.0 + jnp.exp(-jnp.abs(x)))


def _l2norm(x):
    return x * lax.rsqrt(jnp.sum(x * x, axis=-1, keepdims=True) + EPS)


def _iota2(shape, dim):
    return lax.broadcasted_iota(jnp.int32, shape, dim)


def _alternate(*stage_gens):
    pending = list(stage_gens)
    while pending:
        for gen in list(pending):
            try:
                next(gen)
            except StopIteration:
                pending.remove(gen)


def _prompt_kernel(x_ref, pre_g_ref, w_uvz_ref, w_qkvz_ref, w_gates_ref, w_ab_ref, gmlp_g_ref, ws_ref, bs_ref,
                   convw_ref, alog_ref, dtb_ref, gdn_g_ref, w_pa_ref, w_pb_ref, w_out_ref,
                   post_g_ref,
                   y_ref, s_out_ref, conv_out_ref,
                   s_scr, carry_scr, ya_scr, yb_scr, qkv_scr):
    t = pl.program_id(1)
    nt = pl.num_programs(1)
    tm = x_ref.shape[1]
    d = x_ref.shape[2]
    n_chunks = tm // CHUNK
    kdim = N_HEADS * HEAD_DIM

    @pl.when(t == 0)
    def _():
        s_scr[...] = jnp.zeros_like(s_scr)
        carry_scr[...] = jnp.zeros_like(carry_scr)

    x = x_ref[0]
    h = _rms(x, pre_g_ref[...]).astype(BF16)

    row = _iota2((CHUNK, CHUNK), 0)
    col = _iota2((CHUNK, CHUNK), 1)
    token_of = lambda r: (r % SUBLANES) * VROWS + r // SUBLANES
    row_tok, col_tok = token_of(row), token_of(col)
    incl = row_tok >= col_tok
    strict = row_tok > col_tok

    qkvz = jnp.dot(h, jnp.concatenate([_w(w_qkvz_ref), w_ab_ref[...]], axis=1),
                   preferred_element_type=F32)
    uvz = jnp.dot(h, _w(w_uvz_ref), preferred_element_type=F32)
    rowid = _iota2((SUBLANES, kdim), 0)
    n_tail = CONV_W - 1
    prev = [carry_scr[j:j + 1, :] for j in range(n_tail)]

    def conv_gen(c):
        r0 = c * CHUNK
        last_rows = [[] for _ in range(n_tail)]
        for lo in range(0, 3 * kdim, kdim):
            pre = qkvz[r0:r0 + CHUNK, lo:lo + kdim]
            wrapped = []
            for j in range(n_tail):
                blk = pre[(VROWS - n_tail + j) * SUBLANES:(VROWS - n_tail + j + 1) * SUBLANES, :]
                shifted = pltpu.roll(blk, 1, axis=0)
                wrapped.append(jnp.where(rowid == 0, jnp.broadcast_to(prev[j][:, lo:lo + kdim], blk.shape),
                                         shifted))
                last_rows[j].append(blk[SUBLANES - 1:SUBLANES, :])
            conv = pre * convw_ref[CONV_W - 1:CONV_W, lo:lo + kdim]
            for k in range(1, CONV_W):
                back_k = jnp.concatenate(wrapped[n_tail - k:] + [pre[0:(VROWS - k) * SUBLANES, :]], axis=0)
                conv = conv + back_k * convw_ref[CONV_W - 1 - k:CONV_W - k, lo:lo + kdim]
            qkv_scr[r0:r0 + CHUNK, lo:lo + kdim] = _silu(conv)
            yield
        prev[:] = [jnp.concatenate(rows, axis=1) for rows in last_rows]

    gates = jnp.dot(h, _w(w_gates_ref), preferred_element_type=F32)
    ab = qkvz[:, 4 * kdim:4 * kdim + LANES]
    g_all = -jnp.exp(alog_ref[...]) * _softplus(ab + dtb_ref[...])
    beta_all = jax.nn.sigmoid(ab)
    g_all_t = g_all.T
    ltri = incl.astype(F32)
    utri = (row_tok <= col_tok).astype(F32)
    eye = (row == col).astype(F32)
    scale = HEAD_DIM ** -0.5

    pair_heads = range(0, N_HEADS, 2)
    left = lambda a: a[:, 0:HEAD_DIM]
    right = lambda a: a[:, HEAD_DIM:2 * HEAD_DIM]
    both = lambda a, b: jnp.concatenate([a, b], axis=1)
    stack = lambda a, b: jnp.concatenate([a, b], axis=0)
    unpair = lambda pairs: [half(p) for p in pairs for half in (left, right)]
    st = [dict() for _ in range(n_chunks)]
    branch_a = {}

    def front_gen(c):
        r0, r1 = c * CHUNK, (c + 1) * CHUNK
        gc = _dot_exact(ltri, g_all[r0:r1, :])
        gct = _dot_exact(g_all_t[0:N_HEADS, r0:r1], utri)
        egc = jnp.exp(gc)
        qs, ks = [], []
        for hd in range(N_HEADS):
            lo, hi = hd * HEAD_DIM, (hd + 1) * HEAD_DIM
            qs.append(_l2norm(qkv_scr[r0:r1, lo:hi]) * scale)
            ks.append(_l2norm(qkv_scr[r0:r1, kdim + lo:kdim + hi]))
            if hd % 2:
                yield
        kkqks = []
        for hd in pair_heads:
            k1, k2 = ks[hd].astype(BF16), ks[hd + 1].astype(BF16)
            lhs = both(stack(k1, qs[hd].astype(BF16)), stack(k2, qs[hd + 1].astype(BF16)))
            kkqks.append(_dot_nt(lhs, _bdiag(k1, k2)))
        kkqks = unpair(kkqks)
        yield
        a_mats, attns, rhss = [], [], []
        for hd in range(N_HEADS):
            lo, hi = hd * HEAD_DIM, (hd + 1) * HEAD_DIM
            gcol = gc[:, hd:hd + 1]
            grow = gct[hd:hd + 1, :]
            bcol = beta_all[r0:r1, N_HEADS + hd:N_HEADS + hd + 1]
            dec = jnp.where(incl, jnp.exp(jnp.where(incl, gcol - grow, 0.0)), 0.0)
            a_mats.append(jnp.where(strict, bcol * kkqks[hd][0:CHUNK] * dec, 0.0))
            attns.append(kkqks[hd][CHUNK:2 * CHUNK] * dec)
            vh = qkv_scr[r0:r1, 2 * kdim + lo:2 * kdim + hi]
            rhss.append(jnp.concatenate([vh * bcol, ks[hd] * (bcol * egc[:, hd:hd + 1])], axis=1))
            if hd % 2:
                yield
        st[c].update(gc=gc, egc=egc, qs=qs, ks=ks, a_mats=a_mats, attns=attns, rhss=rhss)

    def solve_gen(c):
        a_mats, rhss = st[c]["a_mats"], st[c]["rhss"]
        p_pairs = [both(eye - a_mats[hd], eye - a_mats[hd + 1]) for hd in pair_heads]
        b_pairs = [both(a_mats[hd], a_mats[hd + 1]).astype(BF16) for hd in pair_heads]
        b_pairs = [_dot(b, _bdiag(left(b), right(b))).astype(BF16) for b in b_pairs]
        yield
        n_sq = (CHUNK // 2 - 1).bit_length() - 1
        for _ in range(n_sq - 1):
            prods = [_dot(stack(p.astype(BF16), b), _bdiag(left(b), right(b)))
                     for p, b in zip(p_pairs, b_pairs)]
            p_pairs = [p + pr[0:CHUNK] for p, pr in zip(p_pairs, prods)]
            b_pairs = [pr[CHUNK:2 * CHUNK].astype(BF16) for pr in prods]
            yield
        p_pairs = [p + _dot(p, _bdiag(left(b), right(b))) for p, b in zip(p_pairs, b_pairs)]
        p_mats = [p.astype(BF16) for p in unpair(p_pairs)]
        yield
        uws = [_dot(p, r) for p, r in zip(p_mats, rhss)]
        yield
        resids = [r - uw - _dot(a, uw) for a, uw, r in zip(a_mats, uws, rhss)]
        yield
        st[c]["uws"] = [uw + _dot(p, rs) for uw, p, rs in zip(uws, p_mats, resids)]

    def recur_gen(c):
        r0, r1 = c * CHUNK, (c + 1) * CHUNK
        gc, egc, qs, ks, attns, uws = (st[c][n] for n in ("gc", "egc", "qs", "ks", "attns", "uws"))
        s_olds = [s_scr[hd] for hd in range(N_HEADS)]
        wq_l = lambda hd: stack(uws[hd][:, HEAD_DIM:], qs[hd] * egc[:, hd:hd + 1]).astype(BF16)
        wqs = unpair([_dot(both(wq_l(hd), wq_l(hd + 1)),
                           _bdiag(s_olds[hd].astype(BF16), s_olds[hd + 1].astype(BF16)))
                      for hd in pair_heads])
        yield
        v_news = [uws[hd][:, 0:HEAD_DIM] - wqs[hd][0:CHUNK] for hd in range(N_HEADS)]
        avs = unpair([_dot(both(attns[hd], attns[hd + 1]),
                           _bdiag(v_news[hd].astype(BF16), v_news[hd + 1].astype(BF16)))
                      for hd in pair_heads])
        for hd in range(N_HEADS):
            glast = gc[CHUNK - 1:CHUNK, hd:hd + 1]
            kd = ks[hd] * jnp.exp(glast - gc[:, hd:hd + 1])
            s_scr[hd] = s_olds[hd] * jnp.exp(glast) + _dot_tn(kd, v_news[hd])
        yield
        for hd in range(N_HEADS):
            lo, hi = hd * HEAD_DIM, (hd + 1) * HEAD_DIM
            o = wqs[hd][CHUNK:2 * CHUNK] + avs[hd]
            zb = qkvz[r0:r1, 3 * kdim + lo:3 * kdim + hi]
            yb_scr[r0:r1, lo:hi] = (_rms(o, gdn_g_ref[...]) * _silu(zb)).astype(BF16)
            if hd % 2:
                yield

    def branch_a_gen():
        vn = _rms(uvz[:, d:2 * d], gmlp_g_ref[...])
        vnb = vn.astype(BF16)
        yield
        for g in range(N_HEADS):
            wsg = jnp.where(incl, ws_ref[g], 0.0).astype(BF16)
            lo, hi = g * HEAD_DIM, (g + 1) * HEAD_DIM
            mixed_all = jnp.dot(
                wsg, jnp.concatenate([vnb[c * CHUNK:(c + 1) * CHUNK, lo:hi] for c in range(n_chunks)], axis=1),
                preferred_element_type=F32)
            for c in range(n_chunks):
                r0, r1 = c * CHUNK, (c + 1) * CHUNK
                mixed = mixed_all[:, c * HEAD_DIM:(c + 1) * HEAD_DIM] + bs_ref[g]
                u = uvz[r0:r1, lo:hi]
                z = uvz[r0:r1, 2 * d + lo:2 * d + hi]
                ya_scr[r0:r1, lo:hi] = (u * mixed * _silu(z)).astype(BF16)
            yield
        pa = jnp.dot(ya_scr[...], _w(w_pa_ref), preferred_element_type=F32)
        yield
        branch_a["gated_pa"] = jax.nn.sigmoid(gates[:, 0:d]) * pa
        yield
        branch_a["gate_b"] = jax.nn.sigmoid(gates[:, d:2 * d])

    def chain(*gens):
        for g in gens:
            yield from g

    _alternate(chain(conv_gen(0), front_gen(0)))
    for c in range(n_chunks):
        fillers = []
        if c + 1 < n_chunks:
            fillers.append(chain(conv_gen(c + 1), front_gen(c + 1)))
        if c > 0:
            fillers.append(recur_gen(c - 1))
        if c == n_chunks - 1:
            fillers.append(branch_a_gen())
        _alternate(solve_gen(c), chain(*fillers))
    _alternate(recur_gen(n_chunks - 1))
    gated_pa, gate_b = branch_a["gated_pa"], branch_a["gate_b"]

    tail = jnp.concatenate(prev, axis=0)
    carry_scr[0:n_tail, :] = tail

    @pl.when(t == nt - 1)
    def _():
        conv_out_ref[0] = tail

    @pl.when(t == nt - 1)
    def _():
        s_out_ref[0] = s_scr[...]

    pb = jnp.dot(yb_scr[...], _w(w_pb_ref), preferred_element_type=F32)

    merged = gated_pa + gate_b * pb
    out = jnp.dot(merged.astype(BF16), _w(w_out_ref), preferred_element_type=F32)
    y_ref[0] = x + _rms(out, post_g_ref[...])


def _layer_spec(arr, layer, n_grid):
    shape = (None,) + tuple(arr.shape[1:])
    zeros = (0,) * (arr.ndim - 1)
    if n_grid == 2:
        index_map = lambda b, t: (layer,) + zeros
    else:
        index_map = lambda l, i, j: (l,) + zeros
    return pl.BlockSpec(shape, index_map, pipeline_mode=pl.Buffered(1))


_PROMPT_PARAMS = ("pre_g", "w_uvz", "w_qkvz", "w_gates", "w_ab", "gmlp_g", "w_s", "b_s", "conv_w", "a_log", "dt_bias",
                  "gdn_g", "w_pa", "w_pb", "w_out", "post_g")


def _prompt_layer(x, wts, layer):
    bsz, seq, d = x.shape
    tm = min(TILE_M, seq)
    nt = seq // tm
    kdim = N_HEADS * HEAD_DIM
    conv_dim = 3 * kdim
    params = [wts[name] for name in _PROMPT_PARAMS]
    in_specs = [pl.BlockSpec((1, tm, d), lambda b, t: (b, t, 0))]
    in_specs += [_layer_spec(p, layer, 2) for p in params]
    out_specs = [
        pl.BlockSpec((1, tm, d), lambda b, t: (b, t, 0)),
        pl.BlockSpec((1, N_HEADS, HEAD_DIM, HEAD_DIM), lambda b, t: (b, 0, 0, 0)),
        pl.BlockSpec((1, CONV_W - 1, conv_dim), lambda b, t: (b, 0, 0)),
    ]
    out_shape = [
        jax.ShapeDtypeStruct((bsz, seq, d), F32),
        jax.ShapeDtypeStruct((bsz, N_HEADS, HEAD_DIM, HEAD_DIM), F32),
        jax.ShapeDtypeStruct((bsz, CONV_W - 1, conv_dim), F32),
    ]
    scratch = [
        pltpu.VMEM((N_HEADS, HEAD_DIM, HEAD_DIM), F32),
        pltpu.VMEM((SUBLANES, conv_dim), F32),
        pltpu.VMEM((tm, d), BF16),
        pltpu.VMEM((tm, kdim), BF16),
        pltpu.VMEM((tm, conv_dim), F32),
    ]
    return pl.pallas_call(
        _prompt_kernel,
        grid=(bsz, nt),
        in_specs=in_specs,
        out_specs=out_specs,
        out_shape=out_shape,
        scratch_shapes=scratch,
        compiler_params=pltpu.CompilerParams(
            dimension_semantics=("arbitrary", "arbitrary"),
            vmem_limit_bytes=VMEM_LIMIT),
        name="prompt_layer",
    )(x, *params)


def _sample_kernel(x_ref, s_ref, cs_ref, pre_g_ref, w_uvz_ref, w_qkvz_ref, w_gates_ref, w_ab_ref, gmlp_g_ref, ws00_ref,
                   bs0_ref, convw_ref, alog_ref, dtb_ref, gdn_g_ref, w_pa_ref, w_pb_ref, w_out_ref,
                   post_g_ref,
                   y_ref, s_out_ref, conv_out_ref, vrows_ref,
                   xs_scr, qkvn_scr, zb_scr, sga_pa_scr, sgb_scr, beta_scr, eg_scr, o_scr,
                   q_t, k_t, v_t, beta_t, eg_t, o_t):
    layer = pl.program_id(0)
    i = pl.program_id(1)
    hh = pl.program_id(2)
    n_tiles = pl.num_programs(1)
    n_split = pl.num_programs(2)
    d = x_ref.shape[1]
    kdim = N_HEADS * HEAD_DIM
    heads_per = N_HEADS // HEAD_SPLIT
    wid = heads_per * HEAD_DIM
    rows = SAMPLE_ROWS
    scale = HEAD_DIM ** -0.5

    @pl.when(jnp.logical_and(i == 0, hh == 0))
    def _():
        @pl.when(layer == 0)
        def _():
            xs_scr[...] = x_ref[...]

        x = xs_scr[...]
        h = _rms(x, pre_g_ref[...]).astype(BF16)
        uvz = jnp.dot(h, _w(w_uvz_ref), preferred_element_type=F32)
        vn = _rms(uvz[:, d:2 * d], gmlp_g_ref[...])
        vrows_ref[0] = vn
        mixed = ws00_ref[...] * vn + bs0_ref[...]
        ya = uvz[:, 0:d] * mixed * _silu(uvz[:, 2 * d:3 * d])
        pa = jnp.dot(ya.astype(BF16), _w(w_pa_ref), preferred_element_type=F32)
        qkvz = jnp.dot(h, _w(w_qkvz_ref), preferred_element_type=F32)
        qkvn_scr[...] = qkvz[:, 0:3 * kdim]
        zb_scr[...] = qkvz[:, 3 * kdim:4 * kdim]
        ab = jnp.dot(h, w_ab_ref[...], preferred_element_type=F32)
        eg_scr[...] = jnp.exp(-jnp.exp(alog_ref[...]) * _softplus(ab + dtb_ref[...]))
        beta_scr[...] = jax.nn.sigmoid(ab)
        gates = jnp.dot(h, _w(w_gates_ref),
                        preferred_element_type=F32)
        sga_pa_scr[...] = jax.nn.sigmoid(gates[:, 0:d]) * pa
        sgb_scr[...] = jax.nn.sigmoid(gates[:, d:2 * d])

    b0 = pl.multiple_of(i * rows, rows)

    @pl.when(hh == 0)
    def _():
        cs = cs_ref[0]
        qkv_new = qkvn_scr[pl.ds(b0, rows), :]
        conv = (cs[0] * convw_ref[0:1, :] + cs[1] * convw_ref[1:2, :]
                + cs[2] * convw_ref[2:3, :] + qkv_new * convw_ref[3:4, :])
        conv_out_ref[0, 0] = cs[1]
        conv_out_ref[0, 1] = cs[2]
        conv_out_ref[0, 2] = qkv_new
        qkv = _silu(conv)
        beta_rows = beta_scr[pl.ds(b0, rows), :]
        eg_rows = eg_scr[pl.ds(b0, rows), :]
        for hd in range(N_HEADS):
            lo, hi = hd * HEAD_DIM, (hd + 1) * HEAD_DIM
            half, off = hd // heads_per, (hd % heads_per) * HEAD_DIM
            q_t[half, :, off:off + HEAD_DIM] = _l2norm(qkv[:, lo:hi]) * scale
            k_t[half, :, off:off + HEAD_DIM] = _l2norm(qkv[:, kdim + lo:kdim + hi])
            v_t[half, :, off:off + HEAD_DIM] = qkv[:, 2 * kdim + lo:2 * kdim + hi]
            beta_t[hd] = jnp.broadcast_to(beta_rows[:, N_HEADS + hd:N_HEADS + hd + 1], (rows, HEAD_DIM))
            eg_t[hd] = jnp.broadcast_to(eg_rows[:, hd:hd + 1], (rows, HEAD_DIM))

    rowid = _iota2((SUBLANES, HEAD_DIM), 0)
    states = [(r, j) for r in range(rows) for j in range(heads_per)]
    qv, kv, vv, bv, ev, sv, ksqs = [], [], [], [], [], [], []
    for r, j in states:
        off = j * HEAD_DIM
        q = q_t[hh, r:r + 1, off:off + HEAD_DIM]
        k = k_t[hh, r:r + 1, off:off + HEAD_DIM]
        qv.append(q)
        kv.append(k)
        vv.append(v_t[hh, r:r + 1, off:off + HEAD_DIM])
        bv.append(beta_t[hh * heads_per + j, r:r + 1, :])
        ev.append(eg_t[hh * heads_per + j, r:r + 1, :])
        s = s_ref[0, r, j]
        sv.append(s)
        kq8 = jnp.where(rowid == 0, jnp.broadcast_to(k, (SUBLANES, HEAD_DIM)),
                        jnp.where(rowid == 1, jnp.broadcast_to(q, (SUBLANES, HEAD_DIM)), 0.0))
        ksqs.append(_dot(kq8, s))
    k8s, v8s = [], []
    for n, (r, j) in enumerate(states):
        v_new = bv[n] * (vv[n] - ev[n] * ksqs[n][0:1, :])
        qk = jnp.sum(qv[n] * kv[n], axis=-1, keepdims=True)
        o_t[r:r + 1, j * HEAD_DIM:(j + 1) * HEAD_DIM] = ev[n] * ksqs[n][1:2, :] + qk * v_new
        k_hi, k_lo = _split(kv[n])
        v_hi, v_lo = _split(v_new)
        bc = lambda a: jnp.broadcast_to(a.astype(F32), (SUBLANES, HEAD_DIM))
        k8s.append(jnp.where(rowid <= 1, bc(k_hi), jnp.where(rowid == 2, bc(k_lo), 0.0)))
        v8s.append(jnp.where(rowid == 0, bc(v_hi), jnp.where(rowid == 1, bc(v_lo),
                                                              jnp.where(rowid == 2, bc(v_hi), 0.0))))
    for n in range(0, len(states), 2):
        (r, j), (r2, j2) = states[n], states[n + 1]
        upd = _dot_tn(jnp.concatenate([k8s[n], k8s[n + 1]], axis=0).astype(BF16),
                      _bdiag(v8s[n].astype(BF16), v8s[n + 1].astype(BF16)))
        s_out_ref[0, r, j] = sv[n] * ev[n] + upd[:, 0:HEAD_DIM]
        s_out_ref[0, r2, j2] = sv[n + 1] * ev[n + 1] + upd[:, HEAD_DIM:2 * HEAD_DIM]
    o_scr[hh, pl.ds(b0, rows), :] = o_t[...]

    @pl.when(jnp.logical_and(i == n_tiles - 1, hh == n_split - 1))
    def _():
        pb = jnp.zeros((x_ref.shape[0], d), F32)
        for hd in range(N_HEADS):
            lo, hi = hd * HEAD_DIM, (hd + 1) * HEAD_DIM
            half, off = hd // heads_per, (hd % heads_per) * HEAD_DIM
            o = o_scr[half, :, off:off + HEAD_DIM]
            yb = (_rms(o, gdn_g_ref[...]) * _silu(zb_scr[:, lo:hi])).astype(BF16)
            pb = pb + jnp.dot(yb, pltpu.bitcast(w_pb_ref[lo // 2:hi // 2, :], BF16),
                              preferred_element_type=F32)
        merged = sga_pa_scr[...] + sgb_scr[...] * pb
        out = jnp.dot(merged.astype(BF16), _w(w_out_ref), preferred_element_type=F32)
        y = xs_scr[...] + _rms(out, post_g_ref[...])
        xs_scr[...] = y
        y_ref[...] = y


_SAMPLE_PARAMS = ("pre_g", "w_uvz", "w_qkvz", "w_gates", "w_ab", "gmlp_g", "ws00", "bs0", "conv_w", "a_log", "dt_bias",
                  "gdn_g", "w_pa", "w_pb", "w_out", "post_g")


def _sample_path(x, state_gdn, state_conv, wts):
    n, d = x.shape
    depth = state_gdn.shape[0]
    kdim = N_HEADS * HEAD_DIM
    conv_dim = 3 * kdim
    heads_per = N_HEADS // HEAD_SPLIT
    rows = SAMPLE_ROWS
    params = [wts[name] for name in _SAMPLE_PARAMS]
    s_spec = pl.BlockSpec((1, rows, heads_per, HEAD_DIM, HEAD_DIM), lambda l, i, j: (l, i, j, 0, 0))
    c_spec = pl.BlockSpec((1, CONV_W - 1, rows, conv_dim), lambda l, i, j: (l, 0, i, 0))
    in_specs = [pl.BlockSpec((n, d), lambda l, i, j: (0, 0)), s_spec, c_spec]
    in_specs += [_layer_spec(p, None, 3) for p in params]
    out_specs = [
        pl.BlockSpec((n, d), lambda l, i, j: (0, 0)),
        s_spec,
        c_spec,
        pl.BlockSpec((1, n, d), lambda l, i, j: (l, 0, 0)),
    ]
    out_shape = [
        jax.ShapeDtypeStruct((n, d), F32),
        jax.ShapeDtypeStruct(state_gdn.shape, F32),
        jax.ShapeDtypeStruct(state_conv.shape, F32),
        jax.ShapeDtypeStruct((depth, n, d), F32),
    ]
    scratch = [
        pltpu.VMEM((n, d), F32),
        pltpu.VMEM((n, conv_dim), F32),
        pltpu.VMEM((n, kdim), F32),
        pltpu.VMEM((n, d), F32),
        pltpu.VMEM((n, d), F32),
        pltpu.VMEM((n, LANES), F32),
        pltpu.VMEM((n, LANES), F32),
        pltpu.VMEM((HEAD_SPLIT, n, heads_per * HEAD_DIM), F32),
        pltpu.VMEM((HEAD_SPLIT, rows, heads_per * HEAD_DIM), F32),
        pltpu.VMEM((HEAD_SPLIT, rows, heads_per * HEAD_DIM), F32),
        pltpu.VMEM((HEAD_SPLIT, rows, heads_per * HEAD_DIM), F32),
        pltpu.VMEM((N_HEADS, rows, HEAD_DIM), F32),
        pltpu.VMEM((N_HEADS, rows, HEAD_DIM), F32),
        pltpu.VMEM((rows, heads_per * HEAD_DIM), F32),
    ]
    return pl.pallas_call(
        _sample_kernel,
        grid=(depth, n // rows, HEAD_SPLIT),
        in_specs=in_specs,
        out_specs=out_specs,
        out_shape=out_shape,
        scratch_shapes=scratch,
        compiler_params=pltpu.CompilerParams(
            dimension_semantics=("arbitrary", "arbitrary", "arbitrary"),
            vmem_limit_bytes=VMEM_LIMIT),
        name="sample_path",
    )(x, state_gdn, state_conv, *params)


def _to_kernel_order(x):
    b, l, d = x.shape
    return x.reshape(b, l // CHUNK, SUBLANES, VROWS, d).swapaxes(2, 3).reshape(b, l, d)


def _from_kernel_order(x):
    b, l, d = x.shape
    return x.reshape(b, l // CHUNK, VROWS, SUBLANES, d).swapaxes(2, 3).reshape(b, l, d)


def _pack(x):
    return pltpu.bitcast(x.astype(BF16), jnp.uint32)


def _cast_w_in_kernel(wt_ref, uvz_ref, qkvz_ref, gates_ref, ab_ref):
    n_uvz, n_qkvz, n_gates = uvz_ref.shape[2], qkvz_ref.shape[2], gates_ref.shape[2]
    ab0 = n_uvz + n_qkvz
    n_ab = 2 * N_HEADS
    uvz_ref[0] = _pack(wt_ref[0, 0:n_uvz, :].T)
    qkvz_ref[0] = _pack(wt_ref[0, n_uvz:ab0, :].T)
    gates_ref[0] = _pack(wt_ref[0, ab0 + n_ab:ab0 + n_ab + n_gates, :].T)
    ab_blk = wt_ref[0, ab0:ab0 + LANES, :].T
    lane = _iota2(ab_blk.shape, 1)
    ab_ref[0] = jnp.where(lane < n_ab, ab_blk, 0.0).astype(BF16)


def _cast_square_kernel(a_ref, b_ref, c_ref, oa_ref, ob_ref, oc_ref):
    oa_ref[0] = _pack(a_ref[0])
    ob_ref[0] = _pack(b_ref[0])
    oc_ref[0] = _pack(c_ref[0])


def _cast_weights(w_in, w_proj_a, w_proj_b, w_out):
    depth, d, in_dim = w_in.shape
    kdim = N_HEADS * HEAD_DIM
    rb = 256
    grid = (depth, d // rb)
    u32 = lambda n: jax.ShapeDtypeStruct((depth, d // 2, n), jnp.uint32)
    out_block = lambda n: pl.BlockSpec((1, rb // 2, n), lambda l, i: (l, i, 0))
    params = pltpu.CompilerParams(dimension_semantics=("arbitrary", "arbitrary"),
                                  vmem_limit_bytes=VMEM_LIMIT)
    uvz, qkvz, gates, ab = pl.pallas_call(
        _cast_w_in_kernel,
        grid=grid,
        in_specs=[pl.BlockSpec((1, in_dim, rb), lambda l, i: (l, 0, i))],
        out_specs=[out_block(3 * d), out_block(4 * kdim), out_block(2 * d),
                   pl.BlockSpec((1, rb, LANES), lambda l, i: (l, i, 0))],
        out_shape=[u32(3 * d), u32(4 * kdim), u32(2 * d),
                   jax.ShapeDtypeStruct((depth, d, LANES), BF16)],
        compiler_params=params,
        name="cast_w_in",
    )(jnp.swapaxes(w_in, 1, 2))
    sq_in = pl.BlockSpec((1, rb, d), lambda l, i: (l, i, 0))
    pa, pb, out = pl.pallas_call(
        _cast_square_kernel,
        grid=grid,
        in_specs=[sq_in, sq_in, sq_in],
        out_specs=[out_block(d)] * 3,
        out_shape=[u32(d)] * 3,
        compiler_params=params,
        name="cast_w_square",
    )(w_proj_a, w_proj_b, w_out)
    return {"w_uvz": uvz, "w_qkvz": qkvz, "w_gates": gates, "w_ab": ab,
            "w_pa": pa, "w_pb": pb, "w_out": out}


def _prepare_weights(pre_norm, w_in, gmlp_norm, w_spatial, b_spatial, conv_w, a_log, dt_bias,
                     gdn_norm, w_proj_a, w_proj_b, w_out, post_norm):
    depth = w_in.shape[0]
    pad_lanes = lambda a: jnp.pad(a, ((0, 0), (0, LANES - N_HEADS)))[:, None, :]
    rows = jnp.arange(CHUNK)
    tok = (rows % SUBLANES) * VROWS + rows // SUBLANES
    return {
        **_cast_weights(w_in, w_proj_a, w_proj_b, w_out),
        "pre_g": pre_norm[:, None, :],
        "gmlp_g": gmlp_norm[:, None, :],
        "w_s": w_spatial[:, :, tok, :][:, :, :, tok],
        "b_s": jnp.broadcast_to(b_spatial[:, :, tok, None], (depth, N_HEADS, CHUNK, HEAD_DIM)),
        "ws00": jnp.repeat(w_spatial[:, :, 0, 0], HEAD_DIM, axis=1)[:, None, :],
        "bs0": jnp.repeat(b_spatial[:, :, 0], HEAD_DIM, axis=1)[:, None, :],
        "conv_w": conv_w,
        "a_log": pad_lanes(a_log),
        "dt_bias": pad_lanes(dt_bias),
        "gdn_g": gdn_norm[:, None, :],
        "post_g": post_norm[:, None, :],
    }


def kernel(x_prompt, x_sample, state_gdn, state_conv, pre_norm, w_in, gmlp_norm, w_spatial, b_spatial, conv_w, a_log, dt_bias, gdn_norm, w_proj_a, w_proj_b, w_out, post_norm):
    depth = w_in.shape[0]
    wts = _prepare_weights(pre_norm, w_in, gmlp_norm, w_spatial, b_spatial, conv_w, a_log, dt_bias,
                           gdn_norm, w_proj_a, w_proj_b, w_out, post_norm)
    xp = _to_kernel_order(x_prompt)
    gdn_p, conv_p = [], []
    for l in range(depth):
        xp, sg_p, cb_p = _prompt_layer(xp, wts, l)
        gdn_p.append(sg_p)
        conv_p.append(cb_p)
    xp = _from_kernel_order(xp)
    ys, gdn_s, conv_s, vrows_s = _sample_path(x_sample[:, 0, :], state_gdn,
                                              jnp.swapaxes(state_conv, 1, 2), wts)
    return (xp, ys[:, None, :], jnp.stack(gdn_p), jnp.stack(conv_p), gdn_s,
            jnp.swapaxes(conv_s, 1, 2), vrows_s[:, :, None, :])
```

```python
import jax
import jax.numpy as jnp
from jax import lax
from jax.experimental import pallas as pl
from jax.experimental.pallas import tpu as pltpu

F32 = jnp.float32
BF16 = jnp.bfloat16
EPS = 1e-6

LANES = 128
SUBLANES = 8
HEAD_DIM = 128
N_HEADS = 8
CONV_W = 4
CHUNK = 128
TILE_M = 256
VROWS = CHUNK // SUBLANES
SAMPLE_ROWS = 8
HEAD_SPLIT = 1
VMEM_LIMIT = 56 * 1024 * 1024


def _dot(a, b):
    return jnp.dot(a.astype(BF16), b.astype(BF16), preferred_element_type=F32)


def _dot_nt(a, b):
    return lax.dot_general(a.astype(BF16), b.astype(BF16), (((1,), (1,)), ((), ())),
                           preferred_element_type=F32)


def _dot_tn(a, b):
    return lax.dot_general(a.astype(BF16), b.astype(BF16), (((0,), (0,)), ((), ())),
                           preferred_element_type=F32)


def _bdiag(a, b):
    z = jnp.zeros_like(a)
    return jnp.concatenate([jnp.concatenate([a, z], axis=1), jnp.concatenate([z, b], axis=1)], axis=0)


def _w(w_ref):
    return pltpu.bitcast(w_ref[...], BF16)


def _split(a):
    hi = a.astype(BF16)
    lo = (a - hi.astype(F32)).astype(BF16)
    return hi, lo


def _dot_exact(a, b):
    return jnp.dot(a, b, preferred_element_type=F32, precision=lax.Precision.HIGHEST)


def _rms(x, g):
    return x * lax.rsqrt(jnp.mean(x * x, axis=-1, keepdims=True) + EPS) * g


def _silu(x):
    return x * jax.nn.sigmoid(x)


def _softplus(x):
    return jnp.maximum(x, 0.0) + jnp.log(1.0 + jnp.exp(-jnp.abs(x)))


def _l2norm(x):
    return x * lax.rsqrt(jnp.sum(x * x, axis=-1, keepdims=True) + EPS)


def _iota2(shape, dim):
    return lax.broadcasted_iota(jnp.int32, shape, dim)


def _alternate(*stage_gens):
    pending = list(stage_gens)
    while pending:
        for gen in list(pending):
            try:
                next(gen)
            except StopIteration:
                pending.remove(gen)


def _prompt_kernel(x_ref, pre_g_ref, w_uvz_ref, w_qkvz_ref, w_gates_ref, w_ab_ref, gmlp_g_ref, ws_ref, bs_ref,
                   convw_ref, alog_ref, dtb_ref, gdn_g_ref, w_pa_ref, w_pb_ref, w_out_ref,
                   post_g_ref,
                   y_ref, s_out_ref, conv_out_ref,
                   s_scr, carry_scr, ya_scr, yb_scr, qkv_scr):
    t = pl.program_id(1)
    nt = pl.num_programs(1)
    tm = x_ref.shape[1]
    d = x_ref.shape[2]
    n_chunks = tm // CHUNK
    kdim = N_HEADS * HEAD_DIM

    @pl.when(t == 0)
    def _():
        s_scr[...] = jnp.zeros_like(s_scr)
        carry_scr[...] = jnp.zeros_like(carry_scr)

    x = x_ref[0]
    h = _rms(x, pre_g_ref[...]).astype(BF16)

    row = _iota2((CHUNK, CHUNK), 0)
    col = _iota2((CHUNK, CHUNK), 1)
    token_of = lambda r: (r % SUBLANES) * VROWS + r // SUBLANES
    row_tok, col_tok = token_of(row), token_of(col)
    incl = row_tok >= col_tok
    strict = row_tok > col_tok

    qkvz = jnp.dot(h, jnp.concatenate([_w(w_qkvz_ref), w_ab_ref[...]], axis=1),
                   preferred_element_type=F32)
    uvz = jnp.dot(h, _w(w_uvz_ref), preferred_element_type=F32)
    rowid = _iota2((SUBLANES, kdim), 0)
    n_tail = CONV_W - 1
    prev = [carry_scr[j:j + 1, :] for j in range(n_tail)]

    def conv_gen(c):
        r0 = c * CHUNK
        last_rows = [[] for _ in range(n_tail)]
        for lo in range(0, 3 * kdim, kdim):
            pre = qkvz[r0:r0 + CHUNK, lo:lo + kdim]
            wrapped = []
            for j in range(n_tail):
                blk = pre[(VROWS - n_tail + j) * SUBLANES:(VROWS - n_tail + j + 1) * SUBLANES, :]
                shifted = pltpu.roll(blk, 1, axis=0)
                wrapped.append(jnp.where(rowid == 0, jnp.broadcast_to(prev[j][:, lo:lo + kdim], blk.shape),
                                         shifted))
                last_rows[j].append(blk[SUBLANES - 1:SUBLANES, :])
            conv = pre * convw_ref[CONV_W - 1:CONV_W, lo:lo + kdim]
            for k in range(1, CONV_W):
                back_k = jnp.concatenate(wrapped[n_tail - k:] + [pre[0:(VROWS - k) * SUBLANES, :]], axis=0)
                conv = conv + back_k * convw_ref[CONV_W - 1 - k:CONV_W - k, lo:lo + kdim]
            qkv_scr[r0:r0 + CHUNK, lo:lo + kdim] = _silu(conv)
            yield
        prev[:] = [jnp.concatenate(rows, axis=1) for rows in last_rows]

    ab =qkvz[:, 4 * kdim:4 * kdim + LANES]
    g_all = -jnp.exp(alog_ref[...]) * _softplus(ab + dtb_ref[...])
    beta_all = jax.nn.sigmoid(ab)
    g_all_t = g_all.T
    ltri = incl.astype(F32)
    utri = (row_tok <= col_tok).astype(F32)
    eye = (row == col).astype(F32)
    scale = HEAD_DIM ** -0.5

    pair_heads = range(0, N_HEADS, 2)
    left = lambda a: a[:, 0:HEAD_DIM]
    right = lambda a: a[:, HEAD_DIM:2 * HEAD_DIM]
    both = lambda a, b: jnp.concatenate([a, b], axis=1)
    stack = lambda a, b: jnp.concatenate([a, b], axis=0)
    unpair = lambda pairs: [half(p) for p in pairs for half in (left, right)]
    st = [dict() for _ in range(n_chunks)]
    branch_a = {}

    def front_gen(c):
        r0, r1 = c * CHUNK, (c + 1) * CHUNK
        gc = _dot_exact(ltri, g_all[r0:r1, :])
        gct = _dot_exact(g_all_t[0:N_HEADS, r0:r1], utri)
        egc = jnp.exp(gc)
        qs, ks = [], []
        for hd in range(N_HEADS):
            lo, hi = hd * HEAD_DIM, (hd + 1) * HEAD_DIM
            qs.append(_l2norm(qkv_scr[r0:r1, lo:hi]) * scale)
            ks.append(_l2norm(qkv_scr[r0:r1, kdim + lo:kdim + hi]))
            if hd % 2:
                yield
        kkqks = []
        for hd in pair_heads:
            k1, k2 = ks[hd].astype(BF16), ks[hd + 1].astype(BF16)
            lhs = both(stack(k1, qs[hd].astype(BF16)), stack(k2, qs[hd + 1].astype(BF16)))
            kkqks.append(_dot_nt(lhs, _bdiag(k1, k2)))
        kkqks = unpair(kkqks)
        yield
        a_mats, attns, rhss = [], [], []
        for hd in range(N_HEADS):
            lo, hi = hd * HEAD_DIM, (hd + 1) * HEAD_DIM
            gcol = gc[:, hd:hd + 1]
            grow = gct[hd:hd + 1, :]
            bcol = beta_all[r0:r1, N_HEADS + hd:N_HEADS + hd + 1]
            dec = jnp.where(incl, jnp.exp(jnp.where(incl, gcol - grow, 0.0)), 0.0)
            a_mats.append(jnp.where(strict, bcol * kkqks[hd][0:CHUNK] * dec, 0.0))
            attns.append(kkqks[hd][CHUNK:2 * CHUNK] * dec)
            vh = qkv_scr[r0:r1, 2 * kdim + lo:2 * kdim + hi]
            rhss.append(jnp.concatenate([vh * bcol, ks[hd] * (bcol * egc[:, hd:hd + 1])], axis=1))
            if hd % 2:
                yield
        st[c].update(gc=gc, egc=egc, qs=qs, ks=ks, a_mats=a_mats, attns=attns, rhss=rhss)

    def solve_gen(c):
        a_mats, rhss = st[c]["a_mats"], st[c]["rhss"]
        p_pairs = [both(eye - a_mats[hd], eye - a_mats[hd + 1]) for hd in pair_heads]
        b_pairs = [both(a_mats[hd], a_mats[hd + 1]).astype(BF16) for hd in pair_heads]
        b_pairs = [_dot(b, _bdiag(left(b), right(b))).astype(BF16) for b in b_pairs]
        yield
        n_sq = (CHUNK // 2 - 1).bit_length() - 1
        for _ in range(n_sq - 1):
            prods = [_dot(stack(p.astype(BF16), b), _bdiag(left(b), right(b)))
                     for p, b in zip(p_pairs, b_pairs)]
            p_pairs = [p + pr[0:CHUNK] for p, pr in zip(p_pairs, prods)]
            b_pairs = [pr[CHUNK:2 * CHUNK].astype(BF16) for pr in prods]
            yield
        p_pairs = [p + _dot(p, _bdiag(left(b), right(b))) for p, b in zip(p_pairs, b_pairs)]
        p_mats = [p.astype(BF16) for p in unpair(p_pairs)]
        yield
        uws = [_dot(p, r) for p, r in zip(p_mats, rhss)]
        yield
        resids = [r - uw - _dot(a, uw) for a, uw, r in zip(a_mats, uws, rhss)]
        yield
        st[c]["uws"] = [uw + _dot(p, rs) for uw, p, rs in zip(uws, p_mats, resids)]

    def recur_gen(c):
        r0, r1 = c * CHUNK, (c + 1) * CHUNK
        gc, egc, qs, ks, attns, uws = (st[c][n] for n in ("gc", "egc", "qs", "ks", "attns", "uws"))
        s_olds = [s_scr[hd] for hd in range(N_HEADS)]
        wq_l = lambda hd: stack(uws[hd][:, HEAD_DIM:], qs[hd] * egc[:, hd:hd + 1]).astype(BF16)
        wqs = unpair([_dot(both(wq_l(hd), wq_l(hd + 1)),
                           _bdiag(s_olds[hd].astype(BF16), s_olds[hd + 1].astype(BF16)))
                      for hd in pair_heads])
        yield
        v_news = [uws[hd][:, 0:HEAD_DIM] - wqs[hd][0:CHUNK] for hd in range(N_HEADS)]
        avs = unpair([_dot(both(attns[hd], attns[hd + 1]),
                           _bdiag(v_news[hd].astype(BF16), v_news[hd + 1].astype(BF16)))
                      for hd in pair_heads])
        for hd in range(N_HEADS):
            glast = gc[CHUNK - 1:CHUNK, hd:hd + 1]
            kd = ks[hd] * jnp.exp(glast - gc[:, hd:hd + 1])
            s_scr[hd] = s_olds[hd] * jnp.exp(glast) + _dot_tn(kd, v_news[hd])
        yield
        for hd in range(N_HEADS):
            lo, hi = hd * HEAD_DIM, (hd + 1) * HEAD_DIM
            o = wqs[hd][CHUNK:2 * CHUNK] + avs[hd]
            zb = qkvz[r0:r1, 3 * kdim + lo:3 * kdim + hi]
            yb_scr[r0:r1, lo:hi] = (_rms(o, gdn_g_ref[...]) * _silu(zb)).astype(BF16)
            if hd % 2:
                yield

    def branch_a_gen():
        vn = _rms(uvz[:, d:2 * d], gmlp_g_ref[...])
        vnb = vn.astype(BF16)
        yield
        for g in range(N_HEADS):
            wsg = jnp.where(incl, ws_ref[g], 0.0).astype(BF16)
            lo, hi = g * HEAD_DIM, (g + 1) * HEAD_DIM
            mixed_all = jnp.dot(
                wsg, jnp.concatenate([vnb[c * CHUNK:(c + 1) * CHUNK, lo:hi] for c in range(n_chunks)], axis=1),
                preferred_element_type=F32)
            for c in range(n_chunks):
                r0, r1 = c * CHUNK, (c + 1) * CHUNK
                mixed = mixed_all[:, c * HEAD_DIM:(c + 1) * HEAD_DIM] + bs_ref[g]
                u = uvz[r0:r1, lo:hi]
                z = uvz[r0:r1, 2 * d + lo:2 * d + hi]
                ya_scr[r0:r1, lo:hi] = (u * mixed * _silu(z)).astype(BF16)
            yield
        branch_a["pa"] = jnp.dot(ya_scr[...], _w(w_pa_ref), preferred_element_type=F32)

    def gates_gen():
        gates = jnp.dot(h, _w(w_gates_ref), preferred_element_type=F32)
        yield
        branch_a["gated_pa"] = jax.nn.sigmoid(gates[:, 0:d]) * branch_a["pa"]
        yield
        branch_a["gate_b"] = jax.nn.sigmoid(gates[:, d:2 * d])

    def chain(*gens):
        for g in gens:
            yield from g

    _alternate(chain(conv_gen(0), front_gen(0)))
    for c in range(n_chunks):
        fillers = []
        if c + 1 < n_chunks:
            fillers.append(chain(conv_gen(c + 1), front_gen(c + 1)))
        if c > 0:
            fillers.append(recur_gen(c - 1))
        if c == n_chunks - 1:
            fillers.append(branch_a_gen())
        _alternate(solve_gen(c), chain(*fillers))
    _alternate(recur_gen(n_chunks - 1), gates_gen())
    gated_pa, gate_b = branch_a["gated_pa"], branch_a["gate_b"]

    tail = jnp.concatenate(prev, axis=0)
    carry_scr[0:n_tail, :] = tail

    @pl.when(t == nt - 1)
    def _():
        conv_out_ref[0] = tail

    @pl.when(t == nt - 1)
    def _():
        s_out_ref[0] = s_scr[...]

    pb = jnp.dot(yb_scr[...], _w(w_pb_ref), preferred_element_type=F32)

    merged = gated_pa + gate_b * pb
    out = jnp.dot(merged.astype(BF16), _w(w_out_ref), preferred_element_type=F32)
    y_ref[0] = x + _rms(out, post_g_ref[...])


def _layer_spec(arr, layer, n_grid):
    shape = (None,) + tuple(arr.shape[1:])
    zeros = (0,) * (arr.ndim - 1)
    if n_grid == 2:
        index_map = lambda b, t: (layer,) + zeros
    else:
        index_map = lambda l, i, j: (l,) + zeros
    return pl.BlockSpec(shape, index_map, pipeline_mode=pl.Buffered(1))


_PROMPT_PARAMS = ("pre_g", "w_uvz", "w_qkvz", "w_gates", "w_ab", "gmlp_g", "w_s", "b_s", "conv_w", "a_log", "dt_bias",
                  "gdn_g", "w_pa", "w_pb", "w_out", "post_g")


def _prompt_layer(x, wts, layer):
    bsz, seq, d = x.shape
    tm = min(TILE_M, seq)
    nt = seq // tm
    kdim = N_HEADS * HEAD_DIM
    conv_dim = 3 * kdim
    params = [wts[name] for name in _PROMPT_PARAMS]
    in_specs = [pl.BlockSpec((1, tm, d), lambda b, t: (b, t, 0))]
    in_specs += [_layer_spec(p, layer, 2) for p in params]
    out_specs = [
        pl.BlockSpec((1, tm, d), lambda b, t: (b, t, 0)),
        pl.BlockSpec((1, N_HEADS, HEAD_DIM, HEAD_DIM), lambda b, t: (b, 0, 0, 0)),
        pl.BlockSpec((1, CONV_W - 1, conv_dim), lambda b, t: (b, 0, 0)),
    ]
    out_shape = [
        jax.ShapeDtypeStruct((bsz, seq, d), F32),
        jax.ShapeDtypeStruct((bsz, N_HEADS, HEAD_DIM, HEAD_DIM), F32),
        jax.ShapeDtypeStruct((bsz, CONV_W - 1, conv_dim), F32),
    ]
    scratch = [
        pltpu.VMEM((N_HEADS, HEAD_DIM, HEAD_DIM), F32),
        pltpu.VMEM((SUBLANES, conv_dim), F32),
        pltpu.VMEM((tm, d), BF16),
        pltpu.VMEM((tm, kdim), BF16),
        pltpu.VMEM((tm, conv_dim), F32),
    ]
    return pl.pallas_call(
        _prompt_kernel,
        grid=(bsz, nt),
        in_specs=in_specs,
        out_specs=out_specs,
        out_shape=out_shape,
        scratch_shapes=scratch,
        compiler_params=pltpu.CompilerParams(
            dimension_semantics=("arbitrary", "arbitrary"),
            vmem_limit_bytes=VMEM_LIMIT),
        name="prompt_layer",
    )(x, *params)


def _sample_kernel(x_ref, s_ref, cs_ref, pre_g_ref, w_uvz_ref, w_qkvz_ref, w_gates_ref, w_ab_ref, gmlp_g_ref, ws00_ref,
                   bs0_ref, convw_ref, alog_ref, dtb_ref, gdn_g_ref, w_pa_ref, w_pb_ref, w_out_ref,
                   post_g_ref,
                   y_ref, s_out_ref, conv_out_ref, vrows_ref,
                   xs_scr, qkvn_scr, zb_scr, sga_pa_scr, sgb_scr, beta_scr, eg_scr, o_scr,
                   q_t, k_t, v_t, beta_t, eg_t, o_t):
    layer = pl.program_id(0)
    i = pl.program_id(1)
    hh = pl.program_id(2)
    n_tiles = pl.num_programs(1)
    n_split = pl.num_programs(2)
    d = x_ref.shape[1]
    kdim = N_HEADS * HEAD_DIM
    heads_per = N_HEADS // HEAD_SPLIT
    wid = heads_per * HEAD_DIM
    rows = SAMPLE_ROWS
    scale = HEAD_DIM ** -0.5

    @pl.when(jnp.logical_and(i == 0, hh == 0))
    def _():
        @pl.when(layer == 0)
        def _():
            xs_scr[...] = x_ref[...]

        x = xs_scr[...]
        h = _rms(x, pre_g_ref[...]).astype(BF16)
        uvz = jnp.dot(h, _w(w_uvz_ref), preferred_element_type=F32)
        vn = _rms(uvz[:, d:2 * d], gmlp_g_ref[...])
        vrows_ref[0] = vn
        mixed = ws00_ref[...] * vn + bs0_ref[...]
        ya = uvz[:, 0:d] * mixed * _silu(uvz[:, 2 * d:3 * d])
        pa = jnp.dot(ya.astype(BF16), _w(w_pa_ref), preferred_element_type=F32)
        qkvz = jnp.dot(h, _w(w_qkvz_ref), preferred_element_type=F32)
        qkvn_scr[...] = qkvz[:, 0:3 * kdim]
        zb_scr[...] = qkvz[:, 3 * kdim:4 * kdim]
        ab = jnp.dot(h, w_ab_ref[...], preferred_element_type=F32)
        eg_scr[...] = jnp.exp(-jnp.exp(alog_ref[...]) * _softplus(ab + dtb_ref[...]))
        beta_scr[...] = jax.nn.sigmoid(ab)
        gates = jnp.dot(h, _w(w_gates_ref),
                        preferred_element_type=F32)
        sga_pa_scr[...] = jax.nn.sigmoid(gates[:, 0:d]) * pa
        sgb_scr[...] = jax.nn.sigmoid(gates[:, d:2 * d])

    b0 = pl.multiple_of(i * rows, rows)

    @pl.when(hh == 0)
    def _():
        cs = cs_ref[0]
        qkv_new = qkvn_scr[pl.ds(b0, rows), :]
        conv = (cs[0] * convw_ref[0:1, :] + cs[1] * convw_ref[1:2, :]
                + cs[2] * convw_ref[2:3, :] + qkv_new * convw_ref[3:4, :])
        conv_out_ref[0, 0] = cs[1]
        conv_out_ref[0, 1] = cs[2]
        conv_out_ref[0, 2] = qkv_new
        qkv = _silu(conv)
        beta_rows = beta_scr[pl.ds(b0, rows), :]
        eg_rows = eg_scr[pl.ds(b0, rows), :]
        for hd in range(N_HEADS):
            lo, hi = hd * HEAD_DIM, (hd + 1) * HEAD_DIM
            half, off = hd // heads_per, (hd % heads_per) * HEAD_DIM
            q_t[half, :, off:off + HEAD_DIM] = _l2norm(qkv[:, lo:hi]) * scale
            k_t[half, :, off:off + HEAD_DIM] = _l2norm(qkv[:, kdim + lo:kdim + hi])
            v_t[half, :, off:off + HEAD_DIM] = qkv[:, 2 * kdim + lo:2 * kdim + hi]
            beta_t[hd] = jnp.broadcast_to(beta_rows[:, N_HEADS + hd:N_HEADS + hd + 1], (rows, HEAD_DIM))
            eg_t[hd] = jnp.broadcast_to(eg_rows[:, hd:hd + 1], (rows, HEAD_DIM))

    rowid = _iota2((SUBLANES, HEAD_DIM), 0)
    states = [(r, j) for r in range(rows) for j in range(heads_per)]
    qv, kv, vv, bv, ev, sv, ksqs = [], [], [], [], [], [], []
    for r, j in states:
        off = j * HEAD_DIM
        q = q_t[hh, r:r + 1, off:off + HEAD_DIM]
        k = k_t[hh, r:r + 1, off:off + HEAD_DIM]
        qv.append(q)
        kv.append(k)
        vv.append(v_t[hh, r:r + 1, off:off + HEAD_DIM])
        bv.append(beta_t[hh * heads_per + j, r:r + 1, :])
        ev.append(eg_t[hh * heads_per + j, r:r + 1, :])
        s = s_ref[0, r, j]
        sv.append(s)
        kq8 = jnp.where(rowid == 0, jnp.broadcast_to(k, (SUBLANES, HEAD_DIM)),
                        jnp.where(rowid == 1, jnp.broadcast_to(q, (SUBLANES, HEAD_DIM)), 0.0))
        ksqs.append(_dot(kq8, s))
    k8s, v8s = [], []
    for n, (r, j) in enumerate(states):
        v_new = bv[n] * (vv[n] - ev[n] * ksqs[n][0:1, :])
        qk = jnp.sum(qv[n] * kv[n], axis=-1, keepdims=True)
        o_t[r:r + 1, j * HEAD_DIM:(j + 1) * HEAD_DIM] = ev[n] * ksqs[n][1:2, :] + qk * v_new
        k_hi, k_lo = _split(kv[n])
        v_hi, v_lo = _split(v_new)
        bc = lambda a: jnp.broadcast_to(a.astype(F32), (SUBLANES, HEAD_DIM))
        k8s.append(jnp.where(rowid <= 1, bc(k_hi), jnp.where(rowid == 2, bc(k_lo), 0.0)))
        v8s.append(jnp.where(rowid == 0, bc(v_hi), jnp.where(rowid == 1, bc(v_lo),
                                                              jnp.where(rowid == 2, bc(v_hi), 0.0))))
    for n in range(0, len(states), 2):
        (r, j), (r2, j2) = states[n], states[n + 1]
        upd = _dot_tn(jnp.concatenate([k8s[n], k8s[n + 1]], axis=0).astype(BF16),
                      _bdiag(v8s[n].astype(BF16), v8s[n + 1].astype(BF16)))
        s_out_ref[0, r, j] = sv[n] * ev[n] + upd[:, 0:HEAD_DIM]
        s_out_ref[0, r2, j2] = sv[n + 1] * ev[n + 1] + upd[:, HEAD_DIM:2 * HEAD_DIM]
    o_scr[hh, pl.ds(b0, rows), :] = o_t[...]

    @pl.when(jnp.logical_and(i == n_tiles - 1, hh == n_split - 1))
    def _():
        pb = jnp.zeros((x_ref.shape[0], d), F32)
        for hd in range(N_HEADS):
            lo, hi = hd * HEAD_DIM, (hd + 1) * HEAD_DIM
            half, off = hd // heads_per, (hd % heads_per) * HEAD_DIM
            o = o_scr[half, :, off:off + HEAD_DIM]
            yb = (_rms(o, gdn_g_ref[...]) * _silu(zb_scr[:, lo:hi])).astype(BF16)
            pb = pb + jnp.dot(yb, pltpu.bitcast(w_pb_ref[lo // 2:hi // 2, :], BF16),
                              preferred_element_type=F32)
        merged = sga_pa_scr[...] + sgb_scr[...] * pb
        out = jnp.dot(merged.astype(BF16), _w(w_out_ref), preferred_element_type=F32)
        y = xs_scr[...] + _rms(out, post_g_ref[...])
        xs_scr[...] = y
        y_ref[...] = y


_SAMPLE_PARAMS = ("pre_g", "w_uvz", "w_qkvz", "w_gates", "w_ab", "gmlp_g", "ws00", "bs0", "conv_w", "a_log", "dt_bias",
                  "gdn_g", "w_pa", "w_pb", "w_out", "post_g")


def _sample_path(x, state_gdn, state_conv, wts):
    n, d = x.shape
    depth = state_gdn.shape[0]
    kdim = N_HEADS * HEAD_DIM
    conv_dim = 3 * kdim
    heads_per = N_HEADS // HEAD_SPLIT
    rows = SAMPLE_ROWS
    params = [wts[name] for name in _SAMPLE_PARAMS]
    s_spec = pl.BlockSpec((1, rows, heads_per, HEAD_DIM, HEAD_DIM), lambda l, i, j: (l, i, j, 0, 0))
    c_spec = pl.BlockSpec((1, CONV_W - 1, rows, conv_dim), lambda l, i, j: (l, 0, i, 0))
    in_specs = [pl.BlockSpec((n, d), lambda l, i, j: (0, 0)), s_spec, c_spec]
    in_specs += [_layer_spec(p, None, 3) for p in params]
    out_specs = [
        pl.BlockSpec((n, d), lambda l, i, j: (0, 0)),
        s_spec,
        c_spec,
        pl.BlockSpec((1, n, d), lambda l, i, j: (l, 0, 0)),
    ]
    out_shape = [
        jax.ShapeDtypeStruct((n, d), F32),
        jax.ShapeDtypeStruct(state_gdn.shape, F32),
        jax.ShapeDtypeStruct(state_conv.shape, F32),
        jax.ShapeDtypeStruct((depth, n, d), F32),
    ]
    scratch = [
        pltpu.VMEM((n, d), F32),
        pltpu.VMEM((n, conv_dim), F32),
        pltpu.VMEM((n, kdim), F32),
        pltpu.VMEM((n, d), F32),
        pltpu.VMEM((n, d), F32),
        pltpu.VMEM((n, LANES), F32),
        pltpu.VMEM((n, LANES), F32),
        pltpu.VMEM((HEAD_SPLIT, n, heads_per * HEAD_DIM), F32),
        pltpu.VMEM((HEAD_SPLIT, rows, heads_per * HEAD_DIM), F32),
        pltpu.VMEM((HEAD_SPLIT, rows, heads_per * HEAD_DIM), F32),
        pltpu.VMEM((HEAD_SPLIT, rows, heads_per * HEAD_DIM), F32),
        pltpu.VMEM((N_HEADS, rows, HEAD_DIM), F32),
        pltpu.VMEM((N_HEADS, rows, HEAD_DIM), F32),
        pltpu.VMEM((rows, heads_per * HEAD_DIM), F32),
    ]
    return pl.pallas_call(
        _sample_kernel,
        grid=(depth, n // rows, HEAD_SPLIT),
        in_specs=in_specs,
        out_specs=out_specs,
        out_shape=out_shape,
        scratch_shapes=scratch,
        compiler_params=pltpu.CompilerParams(
            dimension_semantics=("arbitrary", "arbitrary", "arbitrary"),
            vmem_limit_bytes=VMEM_LIMIT),
        name="sample_path",
    )(x, state_gdn, state_conv, *params)


def _to_kernel_order(x):
    b, l, d = x.shape
    return x.reshape(b, l // CHUNK, SUBLANES, VROWS, d).swapaxes(2, 3).reshape(b, l, d)


def _from_kernel_order(x):
    b, l, d = x.shape
    return x.reshape(b, l // CHUNK, VROWS, SUBLANES, d).swapaxes(2, 3).reshape(b, l, d)


def _pack(x):
    return pltpu.bitcast(x.astype(BF16), jnp.uint32)


def _cast_w_in_kernel(wt_ref, uvz_ref, qkvz_ref, gates_ref, ab_ref):
    n_uvz, n_qkvz, n_gates = uvz_ref.shape[2], qkvz_ref.shape[2], gates_ref.shape[2]
    ab0 = n_uvz + n_qkvz
    n_ab = 2 * N_HEADS
    uvz_ref[0] = _pack(wt_ref[0, 0:n_uvz, :].T)
    qkvz_ref[0] = _pack(wt_ref[0, n_uvz:ab0, :].T)
    gates_ref[0] = _pack(wt_ref[0, ab0 + n_ab:ab0 + n_ab + n_gates, :].T)
    ab_blk = wt_ref[0, ab0:ab0 + LANES, :].T
    lane = _iota2(ab_blk.shape, 1)
    ab_ref[0] = jnp.where(lane < n_ab, ab_blk, 0.0).astype(BF16)


def _cast_square_kernel(a_ref, b_ref, c_ref, oa_ref, ob_ref, oc_ref):
    oa_ref[0] = _pack(a_ref[0])
    ob_ref[0] = _pack(b_ref[0])
    oc_ref[0] = _pack(c_ref[0])


def _cast_weights(w_in, w_proj_a, w_proj_b, w_out):
    depth, d, in_dim = w_in.shape
    kdim = N_HEADS * HEAD_DIM
    rb = 256
    grid = (depth, d // rb)
    u32 = lambda n: jax.ShapeDtypeStruct((depth, d // 2, n), jnp.uint32)
    out_block = lambda n: pl.BlockSpec((1, rb // 2, n), lambda l, i: (l, i, 0))
    params = pltpu.CompilerParams(dimension_semantics=("arbitrary", "arbitrary"),
                                  vmem_limit_bytes=VMEM_LIMIT)
    uvz, qkvz, gates, ab = pl.pallas_call(
        _cast_w_in_kernel,
        grid=grid,
        in_specs=[pl.BlockSpec((1, in_dim, rb), lambda l, i: (l, 0, i))],
        out_specs=[out_block(3 * d), out_block(4 * kdim), out_block(2 * d),
                   pl.BlockSpec((1, rb, LANES), lambda l, i: (l, i, 0))],
        out_shape=[u32(3 * d), u32(4 * kdim), u32(2 * d),
                   jax.ShapeDtypeStruct((depth, d, LANES), BF16)],
        compiler_params=params,
        name="cast_w_in",
    )(jnp.swapaxes(w_in, 1, 2))
    sq_in = pl.BlockSpec((1, rb, d), lambda l, i: (l, i, 0))
    pa, pb, out = pl.pallas_call(
        _cast_square_kernel,
        grid=grid,
        in_specs=[sq_in, sq_in, sq_in],
        out_specs=[out_block(d)] * 3,
        out_shape=[u32(d)] * 3,
        compiler_params=params,
        name="cast_w_square",
    )(w_proj_a, w_proj_b, w_out)
    return {"w_uvz": uvz, "w_qkvz": qkvz, "w_gates": gates, "w_ab": ab,
            "w_pa": pa, "w_pb": pb, "w_out": out}


def _prepare_weights(pre_norm, w_in, gmlp_norm, w_spatial, b_spatial, conv_w, a_log, dt_bias,
                     gdn_norm, w_proj_a, w_proj_b, w_out, post_norm):
    depth = w_in.shape[0]
    pad_lanes = lambda a: jnp.pad(a, ((0, 0), (0, LANES - N_HEADS)))[:, None, :]
    rows = jnp.arange(CHUNK)
    tok = (rows % SUBLANES) * VROWS + rows // SUBLANES
    return {
        **_cast_weights(w_in, w_proj_a, w_proj_b, w_out),
        "pre_g": pre_norm[:, None, :],
        "gmlp_g": gmlp_norm[:, None, :],
        "w_s": w_spatial[:, :, tok, :][:, :, :, tok],
        "b_s": jnp.broadcast_to(b_spatial[:, :, tok, None], (depth, N_HEADS, CHUNK, HEAD_DIM)),
        "ws00": jnp.repeat(w_spatial[:, :, 0, 0], HEAD_DIM, axis=1)[:, None, :],
        "bs0": jnp.repeat(b_spatial[:, :, 0], HEAD_DIM, axis=1)[:, None, :],
        "conv_w": conv_w,
        "a_log": pad_lanes(a_log),
        "dt_bias": pad_lanes(dt_bias),
        "gdn_g": gdn_norm[:, None, :],
        "post_g": post_norm[:, None, :],
    }


def kernel(x_prompt, x_sample, state_gdn, state_conv, pre_norm, w_in, gmlp_norm, w_spatial, b_spatial, conv_w, a_log, dt_bias, gdn_norm, w_proj_a, w_proj_b, w_out, post_norm):
    depth = w_in.shape[0]
    wts = _prepare_weights(pre_norm, w_in, gmlp_norm, w_spatial, b_spatial, conv_w, a_log, dt_bias,
                           gdn_norm, w_proj_a, w_proj_b, w_out, post_norm)
    xp = _to_kernel_order(x_prompt)
    gdn_p, conv_p = [], []
    for l in range(depth):
        xp, sg_p, cb_p = _prompt_layer(xp, wts, l)
        gdn_p.append(sg_p)
        conv_p.append(cb_p)
    xp = _from_kernel_order(xp)
    ys, gdn_s, conv_s, vrows_s = _sample_path(x_sample[:, 0, :], state_gdn,
                                              jnp.swapaxes(state_conv, 1, 2), wts)
    return (xp, ys[:, None, :], jnp.stack(gdn_p), jnp.stack(conv_p), gdn_s,
            jnp.swapaxes(conv_s, 1, 2), vrows_s[:, :, None, :])
```

```python
import jax
import jax.numpy as jnp
from jax import lax
from jax.experimental import pallas as pl
from jax.experimental.pallas import tpu as pltpu

F32 = jnp.float32
BF16 = jnp.bfloat16
EPS = 1e-6

LANES = 128
SUBLANES = 8
HEAD_DIM = 128
N_HEADS = 8
CONV_W = 4
CHUNK = 128
TILE_M = 256
VROWS = CHUNK // SUBLANES
SAMPLE_ROWS = 8
HEAD_SPLIT = 1
VMEM_LIMIT = 56 * 1024 * 1024


def _dot(a, b):
    return jnp.dot(a.astype(BF16), b.astype(BF16), preferred_element_type=F32)


def _dot_nt(a, b):
    return lax.dot_general(a.astype(BF16), b.astype(BF16), (((1,), (1,)), ((), ())),
                           preferred_element_type=F32)


def _dot_tn(a, b):
    return lax.dot_general(a.astype(BF16), b.astype(BF16), (((0,), (0,)), ((), ())),
                           preferred_element_type=F32)


def _bdiag(a, b):
    z = jnp.zeros_like(a)
    return jnp.concatenate([jnp.concatenate([a, z], axis=1), jnp.concatenate([z, b], axis=1)], axis=0)


def _w(w_ref):
    return pltpu.bitcast(w_ref[...], BF16)


def _split(a):
    hi = a.astype(BF16)
    lo = (a - hi.astype(F32)).astype(BF16)
    return hi, lo


def _dot_exact(a, b):
    return jnp.dot(a, b, preferred_element_type=F32, precision=lax.Precision.HIGHEST)


def _rms(x, g):
    return x * lax.rsqrt(jnp.mean(x * x, axis=-1, keepdims=True) + EPS) * g


def _silu(x):
    return x * jax.nn.sigmoid(x)


def _softplus(x):
    return jnp.maximum(x, 0.0) + jnp.log(1.0 + jnp.exp(-jnp.abs(x)))


def _l2norm(x):
    return x * lax.rsqrt(jnp.sum(x * x, axis=-1, keepdims=True) + EPS)


def _iota2(shape, dim):
    return lax.broadcasted_iota(jnp.int32, shape, dim)


def _alternate(*stage_gens):
    pending = list(stage_gens)
    while pending:
        for gen in list(pending):
            try:
                next(gen)
            except StopIteration:
                pending.remove(gen)


def _prompt_kernel(x_ref, pre_g_ref, w_uvz_ref, w_qkvz_ref, w_gates_ref, w_ab_ref, gmlp_g_ref, ws_ref, bs_ref,
                   convw_ref, alog_ref, dtb_ref, gdn_g_ref, w_pa_ref, w_pb_ref, w_out_ref,
                   post_g_ref,
                   y_ref, s_out_ref, conv_out_ref,
                   s_scr, carry_scr, ya_scr, yb_scr, qkv_scr):
    t = pl.program_id(1)
    nt = pl.num_programs(1)
    tm = x_ref.shape[1]
    d = x_ref.shape[2]
    n_chunks = tm // CHUNK
    kdim = N_HEADS * HEAD_DIM

    @pl.when(t == 0)
    def _():
        s_scr[...] = jnp.zeros_like(s_scr)
        carry_scr[...] = jnp.zeros_like(carry_scr)

    x = x_ref[0]
    h = _rms(x, pre_g_ref[...]).astype(BF16)

    row = _iota2((CHUNK, CHUNK), 0)
    col = _iota2((CHUNK, CHUNK), 1)
    token_of = lambda r: (r % SUBLANES) * VROWS + r // SUBLANES
    row_tok, col_tok = token_of(row), token_of(col)
    incl = row_tok >= col_tok
    strict = row_tok > col_tok

    qkvz = jnp.dot(h, jnp.concatenate([_w(w_qkvz_ref), w_ab_ref[...]], axis=1),
                   preferred_element_type=F32)
    uvz = jnp.dot(h, _w(w_uvz_ref), preferred_element_type=F32)
    rowid = _iota2((SUBLANES, kdim), 0)
    n_tail = CONV_W - 1
    prev = [carry_scr[j:j + 1, :] for j in range(n_tail)]

    def conv_gen(c):
        r0 = c * CHUNK
        last_rows = [[] for _ in range(n_tail)]
        for lo in range(0, 3 * kdim, kdim):
            pre = qkvz[r0:r0 + CHUNK, lo:lo + kdim]
            wrapped = []
            for j in range(n_tail):
                blk = pre[(VROWS - n_tail + j) * SUBLANES:(VROWS - n_tail + j + 1) * SUBLANES, :]
                shifted = pltpu.roll(blk, 1, axis=0)
                wrapped.append(jnp.where(rowid == 0, jnp.broadcast_to(prev[j][:, lo:lo + kdim], blk.shape),
                                         shifted))
                last_rows[j].append(blk[SUBLANES - 1:SUBLANES, :])
            conv = pre * convw_ref[CONV_W - 1:CONV_W, lo:lo + kdim]
            for k in range(1, CONV_W):
                back_k = jnp.concatenate(wrapped[n_tail - k:] + [pre[0:(VROWS - k) * SUBLANES, :]], axis=0)
                conv = conv + back_k * convw_ref[CONV_W - 1 - k:CONV_W - k, lo:lo + kdim]
            qkv_scr[r0:r0 + CHUNK, lo:lo + kdim] = _silu(conv)
            yield
        prev[:] = [jnp.concatenate(rows, axis=1) for rows in last_rows]

    ab =qkvz[:, 4 * kdim:4 * kdim + LANES]
    g_all = -jnp.exp(alog_ref[...]) * _softplus(ab + dtb_ref[...])
    beta_all = jax.nn.sigmoid(ab)
    g_all_t = g_all.T
    ltri = incl.astype(F32)
    utri = (row_tok <= col_tok).astype(F32)
    eye = (row == col).astype(F32)
    scale = HEAD_DIM ** -0.5

    pair_heads = range(0, N_HEADS, 2)
    left = lambda a: a[:, 0:HEAD_DIM]
    right = lambda a: a[:, HEAD_DIM:2 * HEAD_DIM]
    both = lambda a, b: jnp.concatenate([a, b], axis=1)
    stack = lambda a, b: jnp.concatenate([a, b], axis=0)
    unpair = lambda pairs: [half(p) for p in pairs for half in (left, right)]
    st = [dict() for _ in range(n_chunks)]
    branch_a = {}

    def front_gen(c):
        r0, r1 = c * CHUNK, (c + 1) * CHUNK
        gc = _dot_exact(ltri, g_all[r0:r1, :])
        gct = _dot_exact(g_all_t[0:N_HEADS, r0:r1], utri)
        egc = jnp.exp(gc)
        qs, ks = [], []
        for hd in range(N_HEADS):
            lo, hi = hd * HEAD_DIM, (hd + 1) * HEAD_DIM
            qs.append(_l2norm(qkv_scr[r0:r1, lo:hi]) * scale)
            ks.append(_l2norm(qkv_scr[r0:r1, kdim + lo:kdim + hi]))
            if hd % 2:
                yield
        kkqks = []
        for hd in pair_heads:
            k1, k2 = ks[hd].astype(BF16), ks[hd + 1].astype(BF16)
            lhs = both(stack(k1, qs[hd].astype(BF16)), stack(k2, qs[hd + 1].astype(BF16)))
            kkqks.append(_dot_nt(lhs, _bdiag(k1, k2)))
        kkqks = unpair(kkqks)
        yield
        a_mats, attns, rhss = [], [], []
        for hd in range(N_HEADS):
            lo, hi = hd * HEAD_DIM, (hd + 1) * HEAD_DIM
            gcol = gc[:, hd:hd + 1]
            grow = gct[hd:hd + 1, :]
            bcol = beta_all[r0:r1, N_HEADS + hd:N_HEADS + hd + 1]
            dec = jnp.where(incl, jnp.exp(jnp.where(incl, gcol - grow, 0.0)), 0.0)
            a_mats.append(jnp.where(strict, bcol * kkqks[hd][0:CHUNK] * dec, 0.0))
            attns.append(kkqks[hd][CHUNK:2 * CHUNK] * dec)
            vh = qkv_scr[r0:r1, 2 * kdim + lo:2 * kdim + hi]
            rhss.append(jnp.concatenate([vh * bcol, ks[hd] * (bcol * egc[:, hd:hd + 1])], axis=1))
            if hd % 2:
                yield
        st[c].update(gc=gc, egc=egc, qs=qs, ks=ks, a_mats=a_mats, attns=attns, rhss=rhss)

    def solve_gen(c):
        a_mats, rhss = st[c]["a_mats"], st[c]["rhss"]
        p_pairs = [both(eye - a_mats[hd], eye - a_mats[hd + 1]) for hd in pair_heads]
        b_pairs = [both(a_mats[hd], a_mats[hd + 1]).astype(BF16) for hd in pair_heads]
        b_pairs = [_dot(b, _bdiag(left(b), right(b))).astype(BF16) for b in b_pairs]
        yield
        n_sq = (CHUNK // 2 - 1).bit_length() - 1
        for _ in range(n_sq - 1):
            prods = [_dot(stack(p.astype(BF16), b), _bdiag(left(b), right(b)))
                     for p, b in zip(p_pairs, b_pairs)]
            p_pairs = [p + pr[0:CHUNK] for p, pr in zip(p_pairs, prods)]
            b_pairs = [pr[CHUNK:2 * CHUNK].astype(BF16) for pr in prods]
            yield
        p_pairs = [p + _dot(p, _bdiag(left(b), right(b))) for p, b in zip(p_pairs, b_pairs)]
        p_mats = [p.astype(BF16) for p in unpair(p_pairs)]
        yield
        uws = [_dot(p, r) for p, r in zip(p_mats, rhss)]
        yield
        resids = [r - uw - _dot(a, uw) for a, uw, r in zip(a_mats, uws, rhss)]
        yield
        st[c]["uws"] = [uw + _dot(p, rs) for uw, p, rs in zip(uws, p_mats, resids)]

    def recur_gen(c):
        r0, r1 = c * CHUNK, (c + 1) * CHUNK
        gc, egc, qs, ks, attns, uws = (st[c][n] for n in ("gc", "egc", "qs", "ks", "attns", "uws"))
        s_olds = [s_scr[hd] for hd in range(N_HEADS)]
        wq_l = lambda hd: stack(uws[hd][:, HEAD_DIM:], qs[hd] * egc[:, hd:hd + 1]).astype(BF16)
        wqs = unpair([_dot(both(wq_l(hd), wq_l(hd + 1)),
                           _bdiag(s_olds[hd].astype(BF16), s_olds[hd + 1].astype(BF16)))
                      for hd in pair_heads])
        yield
        v_news = [uws[hd][:, 0:HEAD_DIM] - wqs[hd][0:CHUNK] for hd in range(N_HEADS)]
        avs = unpair([_dot(both(attns[hd], attns[hd + 1]),
                           _bdiag(v_news[hd].astype(BF16), v_news[hd + 1].astype(BF16)))
                      for hd in pair_heads])
        for hd in range(N_HEADS):
            glast = gc[CHUNK - 1:CHUNK, hd:hd + 1]
            kd = ks[hd] * jnp.exp(glast - gc[:, hd:hd + 1])
            s_scr[hd] = s_olds[hd] * jnp.exp(glast) + _dot_tn(kd, v_news[hd])
        yield
        for hd in range(N_HEADS):
            lo, hi = hd * HEAD_DIM, (hd + 1) * HEAD_DIM
            o = wqs[hd][CHUNK:2 * CHUNK] + avs[hd]
            zb = qkvz[r0:r1, 3 * kdim + lo:3 * kdim + hi]
            yb_scr[r0:r1, lo:hi] = (_rms(o, gdn_g_ref[...]) * _silu(zb)).astype(BF16)
            if hd % 2:
                yield

    def branch_a_gen():
        vn = _rms(uvz[:, d:2 * d], gmlp_g_ref[...])
        vnb = vn.astype(BF16)
        yield
        for g in range(N_HEADS):
            wsg = jnp.where(incl, ws_ref[g], 0.0).astype(BF16)
            lo, hi = g * HEAD_DIM, (g + 1) * HEAD_DIM
            mixed_all = jnp.dot(
                wsg, jnp.concatenate([vnb[c * CHUNK:(c + 1) * CHUNK, lo:hi] for c in range(n_chunks)], axis=1),
                preferred_element_type=F32)
            for c in range(n_chunks):
                r0, r1 = c * CHUNK, (c + 1) * CHUNK
                mixed = mixed_all[:, c * HEAD_DIM:(c + 1) * HEAD_DIM] + bs_ref[g]
                u = uvz[r0:r1, lo:hi]
                z = uvz[r0:r1, 2 * d + lo:2 * d + hi]
                ya_scr[r0:r1, lo:hi] = (u * mixed * _silu(z)).astype(BF16)
            yield

    def gates_gen():
        gates = jnp.dot(h, _w(w_gates_ref), preferred_element_type=F32)
        yield
        pa = jnp.dot(ya_scr[...], _w(w_pa_ref), preferred_element_type=F32)
        yield
        branch_a["gated_pa"] = jax.nn.sigmoid(gates[:, 0:d]) * pa
        yield
        branch_a["gate_b"] = jax.nn.sigmoid(gates[:, d:2 * d])

    def chain(*gens):
        for g in gens:
            yield from g

    _alternate(chain(conv_gen(0), front_gen(0)))
    for c in range(n_chunks):
        fillers = []
        if c + 1 < n_chunks:
            fillers.append(chain(conv_gen(c + 1), front_gen(c + 1)))
        if c > 0:
            fillers.append(recur_gen(c - 1))
        if c == n_chunks - 1:
            fillers.append(branch_a_gen())
        _alternate(solve_gen(c), chain(*fillers))
    _alternate(recur_gen(n_chunks - 1), gates_gen())
    gated_pa, gate_b = branch_a["gated_pa"], branch_a["gate_b"]

    tail = jnp.concatenate(prev, axis=0)
    carry_scr[0:n_tail, :] = tail

    @pl.when(t == nt - 1)
    def _():
        conv_out_ref[0] = tail

    @pl.when(t == nt - 1)
    def _():
        s_out_ref[0] = s_scr[...]

    pb = jnp.dot(yb_scr[...], _w(w_pb_ref), preferred_element_type=F32)

    merged = gated_pa + gate_b * pb
    out = jnp.dot(merged.astype(BF16), _w(w_out_ref), preferred_element_type=F32)
    y_ref[0] = x + _rms(out, post_g_ref[...])


def _layer_spec(arr, layer, n_grid):
    shape = (None,) + tuple(arr.shape[1:])
    zeros = (0,) * (arr.ndim - 1)
    if n_grid == 2:
        index_map = lambda b, t: (layer,) + zeros
    else:
        index_map = lambda l, i, j: (l,) + zeros
    return pl.BlockSpec(shape, index_map, pipeline_mode=pl.Buffered(1))


_PROMPT_PARAMS = ("pre_g", "w_uvz", "w_qkvz", "w_gates", "w_ab", "gmlp_g", "w_s", "b_s", "conv_w", "a_log", "dt_bias",
                  "gdn_g", "w_pa", "w_pb", "w_out", "post_g")


def _prompt_layer(x, wts, layer):
    bsz, seq, d = x.shape
    tm = min(TILE_M, seq)
    nt = seq // tm
    kdim = N_HEADS * HEAD_DIM
    conv_dim = 3 * kdim
    params = [wts[name] for name in _PROMPT_PARAMS]
    in_specs = [pl.BlockSpec((1, tm, d), lambda b, t: (b, t, 0))]
    in_specs += [_layer_spec(p, layer, 2) for p in params]
    out_specs = [
        pl.BlockSpec((1, tm, d), lambda b, t: (b, t, 0)),
        pl.BlockSpec((1, N_HEADS, HEAD_DIM, HEAD_DIM), lambda b, t: (b, 0, 0, 0)),
        pl.BlockSpec((1, CONV_W - 1, conv_dim), lambda b, t: (b, 0, 0)),
    ]
    out_shape = [
        jax.ShapeDtypeStruct((bsz, seq, d), F32),
        jax.ShapeDtypeStruct((bsz, N_HEADS, HEAD_DIM, HEAD_DIM), F32),
        jax.ShapeDtypeStruct((bsz, CONV_W - 1, conv_dim), F32),
    ]
    scratch = [
        pltpu.VMEM((N_HEADS, HEAD_DIM, HEAD_DIM), F32),
        pltpu.VMEM((SUBLANES, conv_dim), F32),
        pltpu.VMEM((tm, d), BF16),
        pltpu.VMEM((tm, kdim), BF16),
        pltpu.VMEM((tm, conv_dim), F32),
    ]
    return pl.pallas_call(
        _prompt_kernel,
        grid=(bsz, nt),
        in_specs=in_specs,
        out_specs=out_specs,
        out_shape=out_shape,
        scratch_shapes=scratch,
        compiler_params=pltpu.CompilerParams(
            dimension_semantics=("arbitrary", "arbitrary"),
            vmem_limit_bytes=VMEM_LIMIT),
        name="prompt_layer",
    )(x, *params)


def _sample_kernel(x_ref, s_ref, cs_ref, pre_g_ref, w_uvz_ref, w_qkvz_ref, w_gates_ref, w_ab_ref, gmlp_g_ref, ws00_ref,
                   bs0_ref, convw_ref, alog_ref, dtb_ref, gdn_g_ref, w_pa_ref, w_pb_ref, w_out_ref,
                   post_g_ref,
                   y_ref, s_out_ref, conv_out_ref, vrows_ref,
                   xs_scr, qkvn_scr, zb_scr, sga_pa_scr, sgb_scr, beta_scr, eg_scr, o_scr,
                   q_t, k_t, v_t, beta_t, eg_t, o_t):
    layer = pl.program_id(0)
    i = pl.program_id(1)
    hh = pl.program_id(2)
    n_tiles = pl.num_programs(1)
    n_split = pl.num_programs(2)
    d = x_ref.shape[1]
    kdim = N_HEADS * HEAD_DIM
    heads_per = N_HEADS // HEAD_SPLIT
    wid = heads_per * HEAD_DIM
    rows = SAMPLE_ROWS
    scale = HEAD_DIM ** -0.5

    @pl.when(jnp.logical_and(i == 0, hh == 0))
    def _():
        @pl.when(layer == 0)
        def _():
            xs_scr[...] = x_ref[...]

        x = xs_scr[...]
        h = _rms(x, pre_g_ref[...]).astype(BF16)
        uvz = jnp.dot(h, _w(w_uvz_ref), preferred_element_type=F32)
        vn = _rms(uvz[:, d:2 * d], gmlp_g_ref[...])
        vrows_ref[0] = vn
        mixed = ws00_ref[...] * vn + bs0_ref[...]
        ya = uvz[:, 0:d] * mixed * _silu(uvz[:, 2 * d:3 * d])
        pa = jnp.dot(ya.astype(BF16), _w(w_pa_ref), preferred_element_type=F32)
        qkvz = jnp.dot(h, _w(w_qkvz_ref), preferred_element_type=F32)
        qkvn_scr[...] = qkvz[:, 0:3 * kdim]
        zb_scr[...] = qkvz[:, 3 * kdim:4 * kdim]
        ab = jnp.dot(h, w_ab_ref[...], preferred_element_type=F32)
        eg_scr[...] = jnp.exp(-jnp.exp(alog_ref[...]) * _softplus(ab + dtb_ref[...]))
        beta_scr[...] = jax.nn.sigmoid(ab)
        gates = jnp.dot(h, _w(w_gates_ref),
                        preferred_element_type=F32)
        sga_pa_scr[...] = jax.nn.sigmoid(gates[:, 0:d]) * pa
        sgb_scr[...] = jax.nn.sigmoid(gates[:, d:2 * d])

    b0 = pl.multiple_of(i * rows, rows)

    @pl.when(hh == 0)
    def _():
        cs = cs_ref[0]
        qkv_new = qkvn_scr[pl.ds(b0, rows), :]
        conv = (cs[0] * convw_ref[0:1, :] + cs[1] * convw_ref[1:2, :]
                + cs[2] * convw_ref[2:3, :] + qkv_new * convw_ref[3:4, :])
        conv_out_ref[0, 0] = cs[1]
        conv_out_ref[0, 1] = cs[2]
        conv_out_ref[0, 2] = qkv_new
        qkv = _silu(conv)
        beta_rows = beta_scr[pl.ds(b0, rows), :]
        eg_rows = eg_scr[pl.ds(b0, rows), :]
        for hd in range(N_HEADS):
            lo, hi = hd * HEAD_DIM, (hd + 1) * HEAD_DIM
            half, off = hd // heads_per, (hd % heads_per) * HEAD_DIM
            q_t[half, :, off:off + HEAD_DIM] = _l2norm(qkv[:, lo:hi]) * scale
            k_t[half, :, off:off + HEAD_DIM] = _l2norm(qkv[:, kdim + lo:kdim + hi])
            v_t[half, :, off:off + HEAD_DIM] = qkv[:, 2 * kdim + lo:2 * kdim + hi]
            beta_t[hd] = jnp.broadcast_to(beta_rows[:, N_HEADS + hd:N_HEADS + hd + 1], (rows, HEAD_DIM))
            eg_t[hd] = jnp.broadcast_to(eg_rows[:, hd:hd + 1], (rows, HEAD_DIM))

    rowid = _iota2((SUBLANES, HEAD_DIM), 0)
    states = [(r, j) for r in range(rows) for j in range(heads_per)]
    qv, kv, vv, bv, ev, sv, ksqs = [], [], [], [], [], [], []
    for r, j in states:
        off = j * HEAD_DIM
        q = q_t[hh, r:r + 1, off:off + HEAD_DIM]
        k = k_t[hh, r:r + 1, off:off + HEAD_DIM]
        qv.append(q)
        kv.append(k)
        vv.append(v_t[hh, r:r + 1, off:off + HEAD_DIM])
        bv.append(beta_t[hh * heads_per + j, r:r + 1, :])
        ev.append(eg_t[hh * heads_per + j, r:r + 1, :])
        s = s_ref[0, r, j]
        sv.append(s)
        kq8 = jnp.where(rowid == 0, jnp.broadcast_to(k, (SUBLANES, HEAD_DIM)),
                        jnp.where(rowid == 1, jnp.broadcast_to(q, (SUBLANES, HEAD_DIM)), 0.0))
        ksqs.append(_dot(kq8, s))
    k8s, v8s = [], []
    for n, (r, j) in enumerate(states):
        v_new = bv[n] * (vv[n] - ev[n] * ksqs[n][0:1, :])
        qk = jnp.sum(qv[n] * kv[n], axis=-1, keepdims=True)
        o_t[r:r + 1, j * HEAD_DIM:(j + 1) * HEAD_DIM] = ev[n] * ksqs[n][1:2, :] + qk * v_new
        k_hi, k_lo = _split(kv[n])
        v_hi, v_lo = _split(v_new)
        bc = lambda a: jnp.broadcast_to(a.astype(F32), (SUBLANES, HEAD_DIM))
        k8s.append(jnp.where(rowid <= 1, bc(k_hi), jnp.where(rowid == 2, bc(k_lo), 0.0)))
        v8s.append(jnp.where(rowid == 0, bc(v_hi), jnp.where(rowid == 1, bc(v_lo),
                                                              jnp.where(rowid == 2, bc(v_hi), 0.0))))
    for n in range(0, len(states), 2):
        (r, j), (r2, j2) = states[n], states[n + 1]
        upd = _dot_tn(jnp.concatenate([k8s[n], k8s[n + 1]], axis=0).astype(BF16),
                      _bdiag(v8s[n].astype(BF16), v8s[n + 1].astype(BF16)))
        s_out_ref[0, r, j] = sv[n] * ev[n] + upd[:, 0:HEAD_DIM]
        s_out_ref[0, r2, j2] = sv[n + 1] * ev[n + 1] + upd[:, HEAD_DIM:2 * HEAD_DIM]
    o_scr[hh, pl.ds(b0, rows), :] = o_t[...]

    @pl.when(jnp.logical_and(i == n_tiles - 1, hh == n_split - 1))
    def _():
        pb = jnp.zeros((x_ref.shape[0], d), F32)
        for hd in range(N_HEADS):
            lo, hi = hd * HEAD_DIM, (hd + 1) * HEAD_DIM
            half, off = hd // heads_per, (hd % heads_per) * HEAD_DIM
            o = o_scr[half, :, off:off + HEAD_DIM]
            yb = (_rms(o, gdn_g_ref[...]) * _silu(zb_scr[:, lo:hi])).astype(BF16)
            pb = pb + jnp.dot(yb, pltpu.bitcast(w_pb_ref[lo // 2:hi // 2, :], BF16),
                              preferred_element_type=F32)
        merged = sga_pa_scr[...] + sgb_scr[...] * pb
        out = jnp.dot(merged.astype(BF16), _w(w_out_ref), preferred_element_type=F32)
        y = xs_scr[...] + _rms(out, post_g_ref[...])
        xs_scr[...] = y
        y_ref[...] = y


_SAMPLE_PARAMS = ("pre_g", "w_uvz", "w_qkvz", "w_gates", "w_ab", "gmlp_g", "ws00", "bs0", "conv_w", "a_log", "dt_bias",
                  "gdn_g", "w_pa", "w_pb", "w_out", "post_g")


def _sample_path(x, state_gdn, state_conv, wts):
    n, d = x.shape
    depth = state_gdn.shape[0]
    kdim = N_HEADS * HEAD_DIM
    conv_dim = 3 * kdim
    heads_per = N_HEADS // HEAD_SPLIT
    rows = SAMPLE_ROWS
    params = [wts[name] for name in _SAMPLE_PARAMS]
    s_spec = pl.BlockSpec((1, rows, heads_per, HEAD_DIM, HEAD_DIM), lambda l, i, j: (l, i, j, 0, 0))
    c_spec = pl.BlockSpec((1, CONV_W - 1, rows, conv_dim), lambda l, i, j: (l, 0, i, 0))
    in_specs = [pl.BlockSpec((n, d), lambda l, i, j: (0, 0)), s_spec, c_spec]
    in_specs += [_layer_spec(p, None, 3) for p in params]
    out_specs = [
        pl.BlockSpec((n, d), lambda l, i, j: (0, 0)),
        s_spec,
        c_spec,
        pl.BlockSpec((1, n, d), lambda l, i, j: (l, 0, 0)),
    ]
    out_shape = [
        jax.ShapeDtypeStruct((n, d), F32),
        jax.ShapeDtypeStruct(state_gdn.shape, F32),
        jax.ShapeDtypeStruct(state_conv.shape, F32),
        jax.ShapeDtypeStruct((depth, n, d), F32),
    ]
    scratch = [
        pltpu.VMEM((n, d), F32),
        pltpu.VMEM((n, conv_dim), F32),
        pltpu.VMEM((n, kdim), F32),
        pltpu.VMEM((n, d), F32),
        pltpu.VMEM((n, d), F32),
        pltpu.VMEM((n, LANES), F32),
        pltpu.VMEM((n, LANES), F32),
        pltpu.VMEM((HEAD_SPLIT, n, heads_per * HEAD_DIM), F32),
        pltpu.VMEM((HEAD_SPLIT, rows, heads_per * HEAD_DIM), F32),
        pltpu.VMEM((HEAD_SPLIT, rows, heads_per * HEAD_DIM), F32),
        pltpu.VMEM((HEAD_SPLIT, rows, heads_per * HEAD_DIM), F32),
        pltpu.VMEM((N_HEADS, rows, HEAD_DIM), F32),
        pltpu.VMEM((N_HEADS, rows, HEAD_DIM), F32),
        pltpu.VMEM((rows, heads_per * HEAD_DIM), F32),
    ]
    return pl.pallas_call(
        _sample_kernel,
        grid=(depth, n // rows, HEAD_SPLIT),
        in_specs=in_specs,
        out_specs=out_specs,
        out_shape=out_shape,
        scratch_shapes=scratch,
        compiler_params=pltpu.CompilerParams(
            dimension_semantics=("arbitrary", "arbitrary", "arbitrary"),
            vmem_limit_bytes=VMEM_LIMIT),
        name="sample_path",
    )(x, state_gdn, state_conv, *params)


def _to_kernel_order(x):
    b, l, d = x.shape
    return x.reshape(b, l // CHUNK, SUBLANES, VROWS, d).swapaxes(2, 3).reshape(b, l, d)


def _from_kernel_order(x):
    b, l, d = x.shape
    return x.reshape(b, l // CHUNK, VROWS, SUBLANES, d).swapaxes(2, 3).reshape(b, l, d)


def _pack(x):
    return pltpu.bitcast(x.astype(BF16), jnp.uint32)


def _cast_w_in_kernel(wt_ref, uvz_ref, qkvz_ref, gates_ref, ab_ref):
    n_uvz, n_qkvz, n_gates = uvz_ref.shape[2], qkvz_ref.shape[2], gates_ref.shape[2]
    ab0 = n_uvz + n_qkvz
    n_ab = 2 * N_HEADS
    uvz_ref[0] = _pack(wt_ref[0, 0:n_uvz, :].T)
    qkvz_ref[0] = _pack(wt_ref[0, n_uvz:ab0, :].T)
    gates_ref[0] = _pack(wt_ref[0, ab0 + n_ab:ab0 + n_ab + n_gates, :].T)
    ab_blk = wt_ref[0, ab0:ab0 + LANES, :].T
    lane = _iota2(ab_blk.shape, 1)
    ab_ref[0] = jnp.where(lane < n_ab, ab_blk, 0.0).astype(BF16)


def _cast_square_kernel(a_ref, b_ref, c_ref, oa_ref, ob_ref, oc_ref):
    oa_ref[0] = _pack(a_ref[0])
    ob_ref[0] = _pack(b_ref[0])
    oc_ref[0] = _pack(c_ref[0])


def _cast_weights(w_in, w_proj_a, w_proj_b, w_out):
    depth, d, in_dim = w_in.shape
    kdim = N_HEADS * HEAD_DIM
    rb = 256
    grid = (depth, d // rb)
    u32 = lambda n: jax.ShapeDtypeStruct((depth, d // 2, n), jnp.uint32)
    out_block = lambda n: pl.BlockSpec((1, rb // 2, n), lambda l, i: (l, i, 0))
    params = pltpu.CompilerParams(dimension_semantics=("arbitrary", "arbitrary"),
                                  vmem_limit_bytes=VMEM_LIMIT)
    uvz, qkvz, gates, ab = pl.pallas_call(
        _cast_w_in_kernel,
        grid=grid,
        in_specs=[pl.BlockSpec((1, in_dim, rb), lambda l, i: (l, 0, i))],
        out_specs=[out_block(3 * d), out_block(4 * kdim), out_block(2 * d),
                   pl.BlockSpec((1, rb, LANES), lambda l, i: (l, i, 0))],
        out_shape=[u32(3 * d), u32(4 * kdim), u32(2 * d),
                   jax.ShapeDtypeStruct((depth, d, LANES), BF16)],
        compiler_params=params,
        name="cast_w_in",
    )(jnp.swapaxes(w_in, 1, 2))
    sq_in = pl.BlockSpec((1, rb, d), lambda l, i: (l, i, 0))
    pa, pb, out = pl.pallas_call(
        _cast_square_kernel,
        grid=grid,
        in_specs=[sq_in, sq_in, sq_in],
        out_specs=[out_block(d)] * 3,
        out_shape=[u32(d)] * 3,
        compiler_params=params,
        name="cast_w_square",
    )(w_proj_a, w_proj_b, w_out)
    return {"w_uvz": uvz, "w_qkvz": qkvz, "w_gates": gates, "w_ab": ab,
            "w_pa": pa, "w_pb": pb, "w_out": out}


def _prepare_weights(pre_norm, w_in, gmlp_norm, w_spatial, b_spatial, conv_w, a_log, dt_bias,
                     gdn_norm, w_proj_a, w_proj_b, w_out, post_norm):
    depth = w_in.shape[0]
    pad_lanes = lambda a: jnp.pad(a, ((0, 0), (0, LANES - N_HEADS)))[:, None, :]
    rows = jnp.arange(CHUNK)
    tok = (rows % SUBLANES) * VROWS + rows // SUBLANES
    return {
        **_cast_weights(w_in, w_proj_a, w_proj_b, w_out),
        "pre_g": pre_norm[:, None, :],
        "gmlp_g": gmlp_norm[:, None, :],
        "w_s": w_spatial[:, :, tok, :][:, :, :, tok],
        "b_s": jnp.broadcast_to(b_spatial[:, :, tok, None], (depth, N_HEADS, CHUNK, HEAD_DIM)),
        "ws00": jnp.repeat(w_spatial[:, :, 0, 0], HEAD_DIM, axis=1)[:, None, :],
        "bs0": jnp.repeat(b_spatial[:, :, 0], HEAD_DIM, axis=1)[:, None, :],
        "conv_w": conv_w,
        "a_log": pad_lanes(a_log),
        "dt_bias": pad_lanes(dt_bias),
        "gdn_g": gdn_norm[:, None, :],
        "post_g": post_norm[:, None, :],
    }


def kernel(x_prompt, x_sample, state_gdn, state_conv, pre_norm, w_in, gmlp_norm, w_spatial, b_spatial, conv_w, a_log, dt_bias, gdn_norm, w_proj_a, w_proj_b, w_out, post_norm):
    depth = w_in.shape[0]
    wts = _prepare_weights(pre_norm, w_in, gmlp_norm, w_spatial, b_spatial, conv_w, a_log, dt_bias,
                           gdn_norm, w_proj_a, w_proj_b, w_out, post_norm)
    xp = _to_kernel_order(x_prompt)
    gdn_p, conv_p = [], []
    for l in range(depth):
        xp, sg_p, cb_p = _prompt_layer(xp, wts, l)
        gdn_p.append(sg_p)
        conv_p.append(cb_p)
    xp = _from_kernel_order(xp)
    ys, gdn_s, conv_s, vrows_s = _sample_path(x_sample[:, 0, :], state_gdn,
                                              jnp.swapaxes(state_conv, 1, 2), wts)
    return (xp, ys[:, None, :], jnp.stack(gdn_p), jnp.stack(conv_p), gdn_s,
            jnp.swapaxes(conv_s, 1, 2), vrows_s[:, :, None, :])
```

```python
import jax
import jax.numpy as jnp
from jax import lax
from jax.experimental import pallas as pl
from jax.experimental.pallas import tpu as pltpu

F32 = jnp.float32
BF16 = jnp.bfloat16
EPS = 1e-6

LANES = 128
SUBLANES = 8
HEAD_DIM = 128
N_HEADS = 8
CONV_W = 4
CHUNK = 128
TILE_M = 256
VROWS = CHUNK // SUBLANES
SAMPLE_ROWS = 8
HEAD_SPLIT = 1
VMEM_LIMIT = 56 * 1024 * 1024


def _dot(a, b):
    return jnp.dot(a.astype(BF16), b.astype(BF16), preferred_element_type=F32)


def _dot_nt(a, b):
    return lax.dot_general(a.astype(BF16), b.astype(BF16), (((1,), (1,)), ((), ())),
                           preferred_element_type=F32)


def _dot_tn(a, b):
    return lax.dot_general(a.astype(BF16), b.astype(BF16), (((0,), (0,)), ((), ())),
                           preferred_element_type=F32)


def _bdiag(a, b):
    z = jnp.zeros_like(a)
    return jnp.concatenate([jnp.concatenate([a, z], axis=1), jnp.concatenate([z, b], axis=1)], axis=0)


def _w(w_ref):
    return pltpu.bitcast(w_ref[...], BF16)


def _split(a):
    hi = a.astype(BF16)
    lo = (a - hi.astype(F32)).astype(BF16)
    return hi, lo


def _dot_exact(a, b):
    return jnp.dot(a, b, preferred_element_type=F32, precision=lax.Precision.HIGHEST)


def _rms(x, g):
    return x * lax.rsqrt(jnp.mean(x * x, axis=-1, keepdims=True) + EPS) * g


def _silu(x):
    return x * jax.nn.sigmoid(x)


def _softplus(x):
    return jnp.maximum(x, 0.0) + jnp.log(1.0 + jnp.exp(-jnp.abs(x)))


def _l2norm(x):
    return x * lax.rsqrt(jnp.sum(x * x, axis=-1, keepdims=True) + EPS)


def _iota2(shape, dim):
    return lax.broadcasted_iota(jnp.int32, shape, dim)


def _alternate(*stage_gens):
    pending = list(stage_gens)
    while pending:
        for gen in list(pending):
            try:
                next(gen)
            except StopIteration:
                pending.remove(gen)


def _prompt_kernel(x_ref, x_next_ref, pre_g_ref, w_uvz_ref, w_qkvz_ref, w_gates_ref, w_ab_ref, gmlp_g_ref, ws_ref, bs_ref,
                   convw_ref, alog_ref, dtb_ref, gdn_g_ref, w_pa_ref, w_pb_ref, w_out_ref,
                   post_g_ref,
                   y_ref, s_out_ref, conv_out_ref,
                   s_scr, carry_scr, ya_scr, yb_scr, qkv_scr, h_scr):
    t = pl.program_id(1)
    nt = pl.num_programs(1)
    tm = x_ref.shape[1]
    d = x_ref.shape[2]
    n_chunks = tm // CHUNK
    kdim = N_HEADS * HEAD_DIM

    @pl.when(t == 0)
    def _():
        s_scr[...] = jnp.zeros_like(s_scr)
        carry_scr[...] = jnp.zeros_like(carry_scr)

    x = x_ref[0]

    @pl.when(t == 0)
    def _():
        h_scr[...] = _rms(x, pre_g_ref[...]).astype(BF16)

    h = h_scr[...]

    row = _iota2((CHUNK, CHUNK), 0)
    col = _iota2((CHUNK, CHUNK), 1)
    token_of = lambda r: (r % SUBLANES) * VROWS + r // SUBLANES
    row_tok, col_tok = token_of(row), token_of(col)
    incl = row_tok >= col_tok
    strict = row_tok > col_tok

    qkvz = jnp.dot(h, jnp.concatenate([_w(w_qkvz_ref), w_ab_ref[...]], axis=1),
                   preferred_element_type=F32)
    uvz = jnp.dot(h, _w(w_uvz_ref), preferred_element_type=F32)
    rowid = _iota2((SUBLANES, kdim), 0)
    n_tail = CONV_W - 1
    prev = [carry_scr[j:j + 1, :] for j in range(n_tail)]

    def conv_gen(c):
        r0 = c * CHUNK
        last_rows = [[] for _ in range(n_tail)]
        for lo in range(0, 3 * kdim, kdim):
            pre = qkvz[r0:r0 + CHUNK, lo:lo + kdim]
            wrapped = []
            for j in range(n_tail):
                blk = pre[(VROWS - n_tail + j) * SUBLANES:(VROWS - n_tail + j + 1) * SUBLANES, :]
                shifted = pltpu.roll(blk, 1, axis=0)
                wrapped.append(jnp.where(rowid == 0, jnp.broadcast_to(prev[j][:, lo:lo + kdim], blk.shape),
                                         shifted))
                last_rows[j].append(blk[SUBLANES - 1:SUBLANES, :])
            conv = pre * convw_ref[CONV_W - 1:CONV_W, lo:lo + kdim]
            for k in range(1, CONV_W):
                back_k = jnp.concatenate(wrapped[n_tail - k:] + [pre[0:(VROWS - k) * SUBLANES, :]], axis=0)
                conv = conv + back_k * convw_ref[CONV_W - 1 - k:CONV_W - k, lo:lo + kdim]
            qkv_scr[r0:r0 + CHUNK, lo:lo + kdim] = _silu(conv)
            yield
        prev[:] = [jnp.concatenate(rows, axis=1) for rows in last_rows]

    ab =qkvz[:, 4 * kdim:4 * kdim + LANES]
    g_all = -jnp.exp(alog_ref[...]) * _softplus(ab + dtb_ref[...])
    beta_all = jax.nn.sigmoid(ab)
    g_all_t = g_all.T
    ltri = incl.astype(F32)
    utri = (row_tok <= col_tok).astype(F32)
    eye = (row == col).astype(F32)
    scale = HEAD_DIM ** -0.5

    pair_heads = range(0, N_HEADS, 2)
    left = lambda a: a[:, 0:HEAD_DIM]
    right = lambda a: a[:, HEAD_DIM:2 * HEAD_DIM]
    both = lambda a, b: jnp.concatenate([a, b], axis=1)
    stack = lambda a, b: jnp.concatenate([a, b], axis=0)
    unpair = lambda pairs: [half(p) for p in pairs for half in (left, right)]
    st = [dict() for _ in range(n_chunks)]
    branch_a = {}

    def front_gen(c):
        r0, r1 = c * CHUNK, (c + 1) * CHUNK
        gc = _dot_exact(ltri, g_all[r0:r1, :])
        gct = _dot_exact(g_all_t[0:N_HEADS, r0:r1], utri)
        egc = jnp.exp(gc)
        qs, ks = [], []
        for hd in range(N_HEADS):
            lo, hi = hd * HEAD_DIM, (hd + 1) * HEAD_DIM
            qs.append(_l2norm(qkv_scr[r0:r1, lo:hi]) * scale)
            ks.append(_l2norm(qkv_scr[r0:r1, kdim + lo:kdim + hi]))
            if hd % 2:
                yield
        kkqks = []
        for hd in pair_heads:
            k1, k2 = ks[hd].astype(BF16), ks[hd + 1].astype(BF16)
            lhs = both(stack(k1, qs[hd].astype(BF16)), stack(k2, qs[hd + 1].astype(BF16)))
            kkqks.append(_dot_nt(lhs, _bdiag(k1, k2)))
        kkqks = unpair(kkqks)
        yield
        a_mats, attns, rhss = [], [], []
        for hd in range(N_HEADS):
            lo, hi = hd * HEAD_DIM, (hd + 1) * HEAD_DIM
            gcol = gc[:, hd:hd + 1]
            grow = gct[hd:hd + 1, :]
            bcol = beta_all[r0:r1, N_HEADS + hd:N_HEADS + hd + 1]
            dec = jnp.where(incl, jnp.exp(jnp.where(incl, gcol - grow, 0.0)), 0.0)
            a_mats.append(jnp.where(strict, bcol * kkqks[hd][0:CHUNK] * dec, 0.0))
            attns.append(kkqks[hd][CHUNK:2 * CHUNK] * dec)
            vh = qkv_scr[r0:r1, 2 * kdim + lo:2 * kdim + hi]
            rhss.append(jnp.concatenate([vh * bcol, ks[hd] * (bcol * egc[:, hd:hd + 1])], axis=1))
            if hd % 2:
                yield
        st[c].update(gc=gc, egc=egc, qs=qs, ks=ks, a_mats=a_mats, attns=attns, rhss=rhss)

    def solve_gen(c):
        a_mats, rhss = st[c]["a_mats"], st[c]["rhss"]
        p_pairs = [both(eye - a_mats[hd], eye - a_mats[hd + 1]) for hd in pair_heads]
        b_pairs = [both(a_mats[hd], a_mats[hd + 1]).astype(BF16) for hd in pair_heads]
        b_pairs = [_dot(b, _bdiag(left(b), right(b))).astype(BF16) for b in b_pairs]
        yield
        n_sq = (CHUNK // 2 - 1).bit_length() - 1
        for _ in range(n_sq - 1):
            prods = [_dot(stack(p.astype(BF16), b), _bdiag(left(b), right(b)))
                     for p, b in zip(p_pairs, b_pairs)]
            p_pairs = [p + pr[0:CHUNK] for p, pr in zip(p_pairs, prods)]
            b_pairs = [pr[CHUNK:2 * CHUNK].astype(BF16) for pr in prods]
            yield
        p_pairs = [p + _dot(p, _bdiag(left(b), right(b))) for p, b in zip(p_pairs, b_pairs)]
        p_mats = [p.astype(BF16) for p in unpair(p_pairs)]
        yield
        uws = [_dot(p, r) for p, r in zip(p_mats, rhss)]
        yield
        resids = [r - uw - _dot(a, uw) for a, uw, r in zip(a_mats, uws, rhss)]
        yield
        st[c]["uws"] = [uw + _dot(p, rs) for uw, p, rs in zip(uws, p_mats, resids)]

    def recur_gen(c):
        r0, r1 = c * CHUNK, (c + 1) * CHUNK
        gc, egc, qs, ks, attns, uws = (st[c][n] for n in ("gc", "egc", "qs", "ks", "attns", "uws"))
        s_olds = [s_scr[hd] for hd in range(N_HEADS)]
        wq_l = lambda hd: stack(uws[hd][:, HEAD_DIM:], qs[hd] * egc[:, hd:hd + 1]).astype(BF16)
        wqs = unpair([_dot(both(wq_l(hd), wq_l(hd + 1)),
                           _bdiag(s_olds[hd].astype(BF16), s_olds[hd + 1].astype(BF16)))
                      for hd in pair_heads])
        yield
        v_news = [uws[hd][:, 0:HEAD_DIM] - wqs[hd][0:CHUNK] for hd in range(N_HEADS)]
        avs = unpair([_dot(both(attns[hd], attns[hd + 1]),
                           _bdiag(v_news[hd].astype(BF16), v_news[hd + 1].astype(BF16)))
                      for hd in pair_heads])
        for hd in range(N_HEADS):
            glast = gc[CHUNK - 1:CHUNK, hd:hd + 1]
            kd = ks[hd] * jnp.exp(glast - gc[:, hd:hd + 1])
            s_scr[hd] = s_olds[hd] * jnp.exp(glast) + _dot_tn(kd, v_news[hd])
        yield
        for hd in range(N_HEADS):
            lo, hi = hd * HEAD_DIM, (hd + 1) * HEAD_DIM
            o = wqs[hd][CHUNK:2 * CHUNK] + avs[hd]
            zb = qkvz[r0:r1, 3 * kdim + lo:3 * kdim + hi]
            yb_scr[r0:r1, lo:hi] = (_rms(o, gdn_g_ref[...]) * _silu(zb)).astype(BF16)
            if hd % 2:
                yield

    def branch_a_gen():
        vn = _rms(uvz[:, d:2 * d], gmlp_g_ref[...])
        vnb = vn.astype(BF16)
        yield
        for g in range(N_HEADS):
            wsg = jnp.where(incl, ws_ref[g], 0.0).astype(BF16)
            lo, hi = g * HEAD_DIM, (g + 1) * HEAD_DIM
            mixed_all = jnp.dot(
                wsg, jnp.concatenate([vnb[c * CHUNK:(c + 1) * CHUNK, lo:hi] for c in range(n_chunks)], axis=1),
                preferred_element_type=F32)
            for c in range(n_chunks):
                r0, r1 = c * CHUNK, (c + 1) * CHUNK
                mixed = mixed_all[:, c * HEAD_DIM:(c + 1) * HEAD_DIM] + bs_ref[g]
                u = uvz[r0:r1, lo:hi]
                z = uvz[r0:r1, 2 * d + lo:2 * d + hi]
                ya_scr[r0:r1, lo:hi] = (u * mixed * _silu(z)).astype(BF16)
            yield
        branch_a["pa"] = jnp.dot(ya_scr[...], _w(w_pa_ref), preferred_element_type=F32)

    def gates_gen():
        gates = jnp.dot(h, _w(w_gates_ref), preferred_element_type=F32)
        yield
        branch_a["gated_pa"] = jax.nn.sigmoid(gates[:, 0:d]) * branch_a["pa"]
        yield
        branch_a["gate_b"] = jax.nn.sigmoid(gates[:, d:2 * d])

    def chain(*gens):
        for g in gens:
            yield from g

    _alternate(chain(conv_gen(0), front_gen(0)))
    for c in range(n_chunks):
        fillers = []
        if c + 1 < n_chunks:
            fillers.append(chain(conv_gen(c + 1), front_gen(c + 1)))
        if c > 0:
            fillers.append(recur_gen(c - 1))
        if c == n_chunks - 1:
            fillers.append(branch_a_gen())
        _alternate(solve_gen(c), chain(*fillers))
    def next_h_gen():
        yield
        h_scr[...] = _rms(x_next_ref[0], pre_g_ref[...]).astype(BF16)

    _alternate(recur_gen(n_chunks - 1), gates_gen(), next_h_gen())
    gated_pa, gate_b = branch_a["gated_pa"], branch_a["gate_b"]

    tail = jnp.concatenate(prev, axis=0)
    carry_scr[0:n_tail, :] = tail

    @pl.when(t == nt - 1)
    def _():
        conv_out_ref[0] = tail

    @pl.when(t == nt - 1)
    def _():
        s_out_ref[0] = s_scr[...]

    pb = jnp.dot(yb_scr[...], _w(w_pb_ref), preferred_element_type=F32)

    merged = gated_pa + gate_b * pb
    out = jnp.dot(merged.astype(BF16), _w(w_out_ref), preferred_element_type=F32)
    y_ref[0] = x + _rms(out, post_g_ref[...])


def _layer_spec(arr, layer, n_grid):
    shape = (None,) + tuple(arr.shape[1:])
    zeros = (0,) * (arr.ndim - 1)
    if n_grid == 2:
        index_map = lambda b, t: (layer,) + zeros
    else:
        index_map = lambda l, i, j: (l,) + zeros
    return pl.BlockSpec(shape, index_map, pipeline_mode=pl.Buffered(1))


_PROMPT_PARAMS = ("pre_g", "w_uvz", "w_qkvz", "w_gates", "w_ab", "gmlp_g", "w_s", "b_s", "conv_w", "a_log", "dt_bias",
                  "gdn_g", "w_pa", "w_pb", "w_out", "post_g")


def _prompt_layer(x, wts, layer):
    bsz, seq, d = x.shape
    tm = min(TILE_M, seq)
    nt = seq // tm
    kdim = N_HEADS * HEAD_DIM
    conv_dim = 3 * kdim
    params = [wts[name] for name in _PROMPT_PARAMS]
    in_specs = [pl.BlockSpec((1, tm, d), lambda b, t: (b, t, 0)),
                pl.BlockSpec((1, tm, d), lambda b, t: (b, jnp.minimum(t + 1, nt - 1), 0))]
    in_specs += [_layer_spec(p, layer, 2) for p in params]
    out_specs = [
        pl.BlockSpec((1, tm, d), lambda b, t: (b, t, 0)),
        pl.BlockSpec((1, N_HEADS, HEAD_DIM, HEAD_DIM), lambda b, t: (b, 0, 0, 0)),
        pl.BlockSpec((1, CONV_W - 1, conv_dim), lambda b, t: (b, 0, 0)),
    ]
    out_shape = [
        jax.ShapeDtypeStruct((bsz, seq, d), F32),
        jax.ShapeDtypeStruct((bsz, N_HEADS, HEAD_DIM, HEAD_DIM), F32),
        jax.ShapeDtypeStruct((bsz, CONV_W - 1, conv_dim), F32),
    ]
    scratch = [
        pltpu.VMEM((N_HEADS, HEAD_DIM, HEAD_DIM), F32),
        pltpu.VMEM((SUBLANES, conv_dim), F32),
        pltpu.VMEM((tm, d), BF16),
        pltpu.VMEM((tm, kdim), BF16),
        pltpu.VMEM((tm, conv_dim), F32),
        pltpu.VMEM((tm, d), BF16),
    ]
    return pl.pallas_call(
        _prompt_kernel,
        grid=(bsz, nt),
        in_specs=in_specs,
        out_specs=out_specs,
        out_shape=out_shape,
        scratch_shapes=scratch,
        compiler_params=pltpu.CompilerParams(
            dimension_semantics=("arbitrary", "arbitrary"),
            vmem_limit_bytes=VMEM_LIMIT),
        name="prompt_layer",
    )(x, x, *params)


def _sample_kernel(x_ref, s_ref, cs_ref, pre_g_ref, w_uvz_ref, w_qkvz_ref, w_gates_ref, w_ab_ref, gmlp_g_ref, ws00_ref,
                   bs0_ref, convw_ref, alog_ref, dtb_ref, gdn_g_ref, w_pa_ref, w_pb_ref, w_out_ref,
                   post_g_ref,
                   y_ref, s_out_ref, conv_out_ref, vrows_ref,
                   xs_scr, qkvn_scr, zb_scr, sga_pa_scr, sgb_scr, beta_scr, eg_scr, o_scr,
                   q_t, k_t, v_t, beta_t, eg_t, o_t):
    layer = pl.program_id(0)
    i = pl.program_id(1)
    hh = pl.program_id(2)
    n_tiles = pl.num_programs(1)
    n_split = pl.num_programs(2)
    d = x_ref.shape[1]
    kdim = N_HEADS * HEAD_DIM
    heads_per = N_HEADS // HEAD_SPLIT
    wid = heads_per * HEAD_DIM
    rows = SAMPLE_ROWS
    scale = HEAD_DIM ** -0.5

    @pl.when(jnp.logical_and(i == 0, hh == 0))
    def _():
        @pl.when(layer == 0)
        def _():
            xs_scr[...] = x_ref[...]

        x = xs_scr[...]
        h = _rms(x, pre_g_ref[...]).astype(BF16)
        uvz = jnp.dot(h, _w(w_uvz_ref), preferred_element_type=F32)
        vn = _rms(uvz[:, d:2 * d], gmlp_g_ref[...])
        vrows_ref[0] = vn
        mixed = ws00_ref[...] * vn + bs0_ref[...]
        ya = uvz[:, 0:d] * mixed * _silu(uvz[:, 2 * d:3 * d])
        pa = jnp.dot(ya.astype(BF16), _w(w_pa_ref), preferred_element_type=F32)
        qkvz = jnp.dot(h, _w(w_qkvz_ref), preferred_element_type=F32)
        qkvn_scr[...] = qkvz[:, 0:3 * kdim]
        zb_scr[...] = qkvz[:, 3 * kdim:4 * kdim]
        ab = jnp.dot(h, w_ab_ref[...], preferred_element_type=F32)
        eg_scr[...] = jnp.exp(-jnp.exp(alog_ref[...]) * _softplus(ab + dtb_ref[...]))
        beta_scr[...] = jax.nn.sigmoid(ab)
        gates = jnp.dot(h, _w(w_gates_ref),
                        preferred_element_type=F32)
        sga_pa_scr[...] = jax.nn.sigmoid(gates[:, 0:d]) * pa
        sgb_scr[...] = jax.nn.sigmoid(gates[:, d:2 * d])

    b0 = pl.multiple_of(i * rows, rows)

    @pl.when(hh == 0)
    def _():
        cs = cs_ref[0]
        qkv_new = qkvn_scr[pl.ds(b0, rows), :]
        conv = (cs[0] * convw_ref[0:1, :] + cs[1] * convw_ref[1:2, :]
                + cs[2] * convw_ref[2:3, :] + qkv_new * convw_ref[3:4, :])
        conv_out_ref[0, 0] = cs[1]
        conv_out_ref[0, 1] = cs[2]
        conv_out_ref[0, 2] = qkv_new
        qkv = _silu(conv)
        beta_rows = beta_scr[pl.ds(b0, rows), :]
        eg_rows = eg_scr[pl.ds(b0, rows), :]
        for hd in range(N_HEADS):
            lo, hi = hd * HEAD_DIM, (hd + 1) * HEAD_DIM
            half, off = hd // heads_per, (hd % heads_per) * HEAD_DIM
            q_t[half, :, off:off + HEAD_DIM] = _l2norm(qkv[:, lo:hi]) * scale
            k_t[half, :, off:off + HEAD_DIM] = _l2norm(qkv[:, kdim + lo:kdim + hi])
            v_t[half, :, off:off + HEAD_DIM] = qkv[:, 2 * kdim + lo:2 * kdim + hi]
            beta_t[hd] = jnp.broadcast_to(beta_rows[:, N_HEADS + hd:N_HEADS + hd + 1], (rows, HEAD_DIM))
            eg_t[hd] = jnp.broadcast_to(eg_rows[:, hd:hd + 1], (rows, HEAD_DIM))

    rowid = _iota2((SUBLANES, HEAD_DIM), 0)
    states = [(r, j) for r in range(rows) for j in range(heads_per)]
    qv, kv, vv, bv, ev, sv, ksqs = [], [], [], [], [], [], []
    for r, j in states:
        off = j * HEAD_DIM
        q = q_t[hh, r:r + 1, off:off + HEAD_DIM]
        k = k_t[hh, r:r + 1, off:off + HEAD_DIM]
        qv.append(q)
        kv.append(k)
        vv.append(v_t[hh, r:r + 1, off:off + HEAD_DIM])
        bv.append(beta_t[hh * heads_per + j, r:r + 1, :])
        ev.append(eg_t[hh * heads_per + j, r:r + 1, :])
        s = s_ref[0, r, j]
        sv.append(s)
        kq8 = jnp.where(rowid == 0, jnp.broadcast_to(k, (SUBLANES, HEAD_DIM)),
                        jnp.where(rowid == 1, jnp.broadcast_to(q, (SUBLANES, HEAD_DIM)), 0.0))
        ksqs.append(_dot(kq8, s))
    k8s, v8s = [], []
    for n, (r, j) in enumerate(states):
        v_new = bv[n] * (vv[n] - ev[n] * ksqs[n][0:1, :])
        qk = jnp.sum(qv[n] * kv[n], axis=-1, keepdims=True)
        o_t[r:r + 1, j * HEAD_DIM:(j + 1) * HEAD_DIM] = ev[n] * ksqs[n][1:2, :] + qk * v_new
        k_hi, k_lo = _split(kv[n])
        v_hi, v_lo = _split(v_new)
        bc = lambda a: jnp.broadcast_to(a.astype(F32), (SUBLANES, HEAD_DIM))
        k8s.append(jnp.where(rowid <= 1, bc(k_hi), jnp.where(rowid == 2, bc(k_lo), 0.0)))
        v8s.append(jnp.where(rowid == 0, bc(v_hi), jnp.where(rowid == 1, bc(v_lo),
                                                              jnp.where(rowid == 2, bc(v_hi), 0.0))))
    for n in range(0, len(states), 2):
        (r, j), (r2, j2) = states[n], states[n + 1]
        upd = _dot_tn(jnp.concatenate([k8s[n], k8s[n + 1]], axis=0).astype(BF16),
                      _bdiag(v8s[n].astype(BF16), v8s[n + 1].astype(BF16)))
        s_out_ref[0, r, j] = sv[n] * ev[n] + upd[:, 0:HEAD_DIM]
        s_out_ref[0, r2, j2] = sv[n + 1] * ev[n + 1] + upd[:, HEAD_DIM:2 * HEAD_DIM]
    o_scr[hh, pl.ds(b0, rows), :] = o_t[...]

    @pl.when(jnp.logical_and(i == n_tiles - 1, hh == n_split - 1))
    def _():
        pb = jnp.zeros((x_ref.shape[0], d), F32)
        for hd in range(N_HEADS):
            lo, hi = hd * HEAD_DIM, (hd + 1) * HEAD_DIM
            half, off = hd // heads_per, (hd % heads_per) * HEAD_DIM
            o = o_scr[half, :, off:off + HEAD_DIM]
            yb = (_rms(o, gdn_g_ref[...]) * _silu(zb_scr[:, lo:hi])).astype(BF16)
            pb = pb + jnp.dot(yb, pltpu.bitcast(w_pb_ref[lo // 2:hi // 2, :], BF16),
                              preferred_element_type=F32)
        merged = sga_pa_scr[...] + sgb_scr[...] * pb
        out = jnp.dot(merged.astype(BF16), _w(w_out_ref), preferred_element_type=F32)
        y = xs_scr[...] + _rms(out, post_g_ref[...])
        xs_scr[...] = y
        y_ref[...] = y


_SAMPLE_PARAMS = ("pre_g", "w_uvz", "w_qkvz", "w_gates", "w_ab", "gmlp_g", "ws00", "bs0", "conv_w", "a_log", "dt_bias",
                  "gdn_g", "w_pa", "w_pb", "w_out", "post_g")


def _sample_path(x, state_gdn, state_conv, wts):
    n, d = x.shape
    depth = state_gdn.shape[0]
    kdim = N_HEADS * HEAD_DIM
    conv_dim = 3 * kdim
    heads_per = N_HEADS // HEAD_SPLIT
    rows = SAMPLE_ROWS
    params = [wts[name] for name in _SAMPLE_PARAMS]
    s_spec = pl.BlockSpec((1, rows, heads_per, HEAD_DIM, HEAD_DIM), lambda l, i, j: (l, i, j, 0, 0))
    c_spec = pl.BlockSpec((1, CONV_W - 1, rows, conv_dim), lambda l, i, j: (l, 0, i, 0))
    in_specs = [pl.BlockSpec((n, d), lambda l, i, j: (0, 0)), s_spec, c_spec]
    in_specs += [_layer_spec(p, None, 3) for p in params]
    out_specs = [
        pl.BlockSpec((n, d), lambda l, i, j: (0, 0)),
        s_spec,
        c_spec,
        pl.BlockSpec((1, n, d), lambda l, i, j: (l, 0, 0)),
    ]
    out_shape = [
        jax.ShapeDtypeStruct((n, d), F32),
        jax.ShapeDtypeStruct(state_gdn.shape, F32),
        jax.ShapeDtypeStruct(state_conv.shape, F32),
        jax.ShapeDtypeStruct((depth, n, d), F32),
    ]
    scratch = [
        pltpu.VMEM((n, d), F32),
        pltpu.VMEM((n, conv_dim), F32),
        pltpu.VMEM((n, kdim), F32),
        pltpu.VMEM((n, d), F32),
        pltpu.VMEM((n, d), F32),
        pltpu.VMEM((n, LANES), F32),
        pltpu.VMEM((n, LANES), F32),
        pltpu.VMEM((HEAD_SPLIT, n, heads_per * HEAD_DIM), F32),
        pltpu.VMEM((HEAD_SPLIT, rows, heads_per * HEAD_DIM), F32),
        pltpu.VMEM((HEAD_SPLIT, rows, heads_per * HEAD_DIM), F32),
        pltpu.VMEM((HEAD_SPLIT, rows, heads_per * HEAD_DIM), F32),
        pltpu.VMEM((N_HEADS, rows, HEAD_DIM), F32),
        pltpu.VMEM((N_HEADS, rows, HEAD_DIM), F32),
        pltpu.VMEM((rows, heads_per * HEAD_DIM), F32),
    ]
    return pl.pallas_call(
        _sample_kernel,
        grid=(depth, n // rows, HEAD_SPLIT),
        in_specs=in_specs,
        out_specs=out_specs,
        out_shape=out_shape,
        scratch_shapes=scratch,
        compiler_params=pltpu.CompilerParams(
            dimension_semantics=("arbitrary", "arbitrary", "arbitrary"),
            vmem_limit_bytes=VMEM_LIMIT),
        name="sample_path",
    )(x, state_gdn, state_conv, *params)


def _to_kernel_order(x):
    b, l, d = x.shape
    return x.reshape(b, l // CHUNK, SUBLANES, VROWS, d).swapaxes(2, 3).reshape(b, l, d)


def _from_kernel_order(x):
    b, l, d = x.shape
    return x.reshape(b, l // CHUNK, VROWS, SUBLANES, d).swapaxes(2, 3).reshape(b, l, d)


def _pack(x):
    return pltpu.bitcast(x.astype(BF16), jnp.uint32)


def _cast_w_in_kernel(wt_ref, uvz_ref, qkvz_ref, gates_ref, ab_ref):
    n_uvz, n_qkvz, n_gates = uvz_ref.shape[2], qkvz_ref.shape[2], gates_ref.shape[2]
    ab0 = n_uvz + n_qkvz
    n_ab = 2 * N_HEADS
    uvz_ref[0] = _pack(wt_ref[0, 0:n_uvz, :].T)
    qkvz_ref[0] = _pack(wt_ref[0, n_uvz:ab0, :].T)
    gates_ref[0] = _pack(wt_ref[0, ab0 + n_ab:ab0 + n_ab + n_gates, :].T)
    ab_blk = wt_ref[0, ab0:ab0 + LANES, :].T
    lane = _iota2(ab_blk.shape, 1)
    ab_ref[0] = jnp.where(lane < n_ab, ab_blk, 0.0).astype(BF16)


def _cast_square_kernel(a_ref, b_ref, c_ref, oa_ref, ob_ref, oc_ref):
    oa_ref[0] = _pack(a_ref[0])
    ob_ref[0] = _pack(b_ref[0])
    oc_ref[0] = _pack(c_ref[0])


def _cast_weights(w_in, w_proj_a, w_proj_b, w_out):
    depth, d, in_dim = w_in.shape
    kdim = N_HEADS * HEAD_DIM
    rb = 256
    grid = (depth, d // rb)
    u32 = lambda n: jax.ShapeDtypeStruct((depth, d // 2, n), jnp.uint32)
    out_block = lambda n: pl.BlockSpec((1, rb // 2, n), lambda l, i: (l, i, 0))
    params = pltpu.CompilerParams(dimension_semantics=("arbitrary", "arbitrary"),
                                  vmem_limit_bytes=VMEM_LIMIT)
    uvz, qkvz, gates, ab = pl.pallas_call(
        _cast_w_in_kernel,
        grid=grid,
        in_specs=[pl.BlockSpec((1, in_dim, rb), lambda l, i: (l, 0, i))],
        out_specs=[out_block(3 * d), out_block(4 * kdim), out_block(2 * d),
                   pl.BlockSpec((1, rb, LANES), lambda l, i: (l, i, 0))],
        out_shape=[u32(3 * d), u32(4 * kdim), u32(2 * d),
                   jax.ShapeDtypeStruct((depth, d, LANES), BF16)],
        compiler_params=params,
        name="cast_w_in",
    )(jnp.swapaxes(w_in, 1, 2))
    sq_in = pl.BlockSpec((1, rb, d), lambda l, i: (l, i, 0))
    pa, pb, out = pl.pallas_call(
        _cast_square_kernel,
        grid=grid,
        in_specs=[sq_in, sq_in, sq_in],
        out_specs=[out_block(d)] * 3,
        out_shape=[u32(d)] * 3,
        compiler_params=params,
        name="cast_w_square",
    )(w_proj_a, w_proj_b, w_out)
    return {"w_uvz": uvz, "w_qkvz": qkvz, "w_gates": gates, "w_ab": ab,
            "w_pa": pa, "w_pb": pb, "w_out": out}


def _prepare_weights(pre_norm, w_in, gmlp_norm, w_spatial, b_spatial, conv_w, a_log, dt_bias,
                     gdn_norm, w_proj_a, w_proj_b, w_out, post_norm):
    depth = w_in.shape[0]
    pad_lanes = lambda a: jnp.pad(a, ((0, 0), (0, LANES - N_HEADS)))[:, None, :]
    rows = jnp.arange(CHUNK)
    tok = (rows % SUBLANES) * VROWS + rows // SUBLANES
    return {
        **_cast_weights(w_in, w_proj_a, w_proj_b, w_out),
        "pre_g": pre_norm[:, None, :],
        "gmlp_g": gmlp_norm[:, None, :],
        "w_s": w_spatial[:, :, tok, :][:, :, :, tok],
        "b_s": jnp.broadcast_to(b_spatial[:, :, tok, None], (depth, N_HEADS, CHUNK, HEAD_DIM)),
        "ws00": jnp.repeat(w_spatial[:, :, 0, 0], HEAD_DIM, axis=1)[:, None, :],
        "bs0": jnp.repeat(b_spatial[:, :, 0], HEAD_DIM, axis=1)[:, None, :],
        "conv_w": conv_w,
        "a_log": pad_lanes(a_log),
        "dt_bias": pad_lanes(dt_bias),
        "gdn_g": gdn_norm[:, None, :],
        "post_g": post_norm[:, None, :],
    }


def kernel(x_prompt, x_sample, state_gdn, state_conv, pre_norm, w_in, gmlp_norm, w_spatial, b_spatial, conv_w, a_log, dt_bias, gdn_norm, w_proj_a, w_proj_b, w_out, post_norm):
    depth = w_in.shape[0]
    wts = _prepare_weights(pre_norm, w_in, gmlp_norm, w_spatial, b_spatial, conv_w, a_log, dt_bias,
                           gdn_norm, w_proj_a, w_proj_b, w_out, post_norm)
    xp = _to_kernel_order(x_prompt)
    gdn_p, conv_p = [], []
    for l in range(depth):
        xp, sg_p, cb_p = _prompt_layer(xp, wts, l)
        gdn_p.append(sg_p)
        conv_p.append(cb_p)
    xp = _from_kernel_order(xp)
    ys, gdn_s, conv_s, vrows_s = _sample_path(x_sample[:, 0, :], state_gdn,
                                              jnp.swapaxes(state_conv, 1, 2), wts)
    return (xp, ys[:, None, :], jnp.stack(gdn_p), jnp.stack(conv_p), gdn_s,
            jnp.swapaxes(conv_s, 1, 2), vrows_s[:, :, None, :])
```
